```python
import jax
import jax.numpy as jnp
from jax import lax
import numpy as np

D_MODEL = 1024
BATCH = 8
SEQ = 4096
DEPTH = 2

GRID_W = 64
CTX_LEN = 256
EPS = 1e-6

MLA_HEADS = 6
Q_LORA = 384
KV_LORA = 256
NOPE_DIM = 128
ROPE_DIM = 64
V_DIM = 128
QK_DIM = NOPE_DIM + ROPE_DIM
ROPE_BASE = 10000.0
Q_BLOCK = 128
FOURIER_GROUPS = 4
FOURIER_GROUP_DIM = 64
FOURIER_WIDTH = FOURIER_GROUPS * FOURIER_GROUP_DIM
MLA_IN = Q_LORA + KV_LORA + ROPE_DIM
EVEN_IN = MLA_IN + FOURIER_WIDTH
EVEN_MIX = MLA_HEADS * V_DIM + FOURIER_WIDTH
GLA_HEADS = 4
GLA_DK = 128
GLA_DV = 256
GATE_RANK = 16
GATE_NORMALIZER = 16.0
GLA_CHUNK = 64
GLA_KW = GLA_HEADS * GLA_DK
GLA_VW = GLA_HEADS * GLA_DV
ODD_IN = 2 * GLA_KW + 2 * GLA_VW + 2 * GATE_RANK
ODD_MIX = GLA_VW
N_EXPERTS = 32
TOP_K = 4
D_FF = 1024
SWIGLU_ALPHA = 1.702
SWIGLU_LIMIT = 7.0
EXPERT_BLOCK = 128

kernel_name = "hybrid_mla_fnet_gla_moe_dit"


def rmsnorm(x, g):
    xf = x.astype(jnp.float32)
    y = xf * lax.rsqrt(jnp.mean(xf * xf, axis=-1, keepdims=True) + EPS)
    return (y * g.astype(jnp.float32)).astype(x.dtype)


def modulate(x, g, shift, scale):
    return rmsnorm(x, g) * (1 + scale) + shift


def adaln(cvec, mod_w, mod_b):
    m = jax.nn.silu(cvec) @ mod_w + mod_b
    return jnp.split(m[..., None, :], 6, axis=-1)


def _flip(a):
    return a[:, ::-1]


def axial_rope_tables(n_tokens):
    rows = n_tokens // GRID_W
    row, col = jnp.meshgrid(jnp.arange(rows, dtype=jnp.float32),
                            jnp.arange(GRID_W, dtype=jnp.float32), indexing="ij")
    axis_dim = ROPE_DIM // 2
    inv_freq = ROPE_BASE ** (-jnp.arange(0, axis_dim, 2, dtype=jnp.float32) / axis_dim)
    ang_r = row.reshape(-1, 1) * inv_freq
    ang_c = col.reshape(-1, 1) * inv_freq
    ang = jnp.concatenate([ang_r, ang_r, ang_c, ang_c], axis=-1)
    return jnp.cos(ang), jnp.sin(ang)


def apply_rope(x, cos, sin):
    r1, r2, c1, c2 = jnp.split(x, 4, axis=-1)
    rot = jnp.concatenate([-r2, r1, -c2, c1], axis=-1)
    return (x * cos + rot * sin).astype(x.dtype)


def mla_queries(c_q, q_norm_g, w_q_up, cos, sin):
    b, n, _ = c_q.shape
    q = (rmsnorm(c_q, q_norm_g) @ w_q_up).reshape(b, n, MLA_HEADS, QK_DIM)
    q_nope, q_rope = q[..., :NOPE_DIM], q[..., NOPE_DIM:]
    if cos is not None:
        q_rope = apply_rope(q_rope, cos[:, None, :], sin[:, None, :])
    return jnp.concatenate([q_nope, q_rope], axis=-1)


def mla_keys_values(u_kv, kv_norm_g, w_kv_up, cos, sin):
    b, n, _ = u_kv.shape
    c_kv, k_rope = u_kv[..., :KV_LORA], u_kv[..., KV_LORA:]
    kv = (rmsnorm(c_kv, kv_norm_g) @ w_kv_up).reshape(b, n, MLA_HEADS, NOPE_DIM + V_DIM)
    k_nope, v = kv[..., :NOPE_DIM], kv[..., NOPE_DIM:]
    if cos is not None:
        k_rope = apply_rope(k_rope, cos, sin)
    k = jnp.concatenate([k_nope, jnp.broadcast_to(k_rope[:, :, None, :], (b, n, MLA_HEADS, ROPE_DIM))], axis=-1)
    return k, v


def attend(q, k, v):
    s = jnp.einsum("bqhd,bkhd->bhqk", q.astype(jnp.float32), k.astype(jnp.float32)) * (QK_DIM ** -0.5)
    p = jax.nn.softmax(s, axis=-1).astype(v.dtype)
    return jnp.einsum("bhqk,bkhd->bqhd", p, v)


def blocked_attend(q, k, v):
    b, n, h, dk = q.shape
    nb = n // Q_BLOCK
    qb = q.reshape(b, nb, Q_BLOCK, h, dk).transpose(1, 0, 2, 3, 4)
    ob = lax.map(lambda qi: attend(qi, k, v), qb)
    return ob.transpose(1, 0, 2, 3, 4).reshape(b, n, h, -1)


def fourier_mix(u):
    b, n, _ = u.shape
    ug = u.reshape(b, n, FOURIER_GROUPS, FOURIER_GROUP_DIM).astype(jnp.float32)
    f = jnp.fft.fft2(ug, axes=(1, 3), norm="ortho").real
    return f.reshape(b, n, FOURIER_WIDTH).astype(u.dtype)


def even_mixer(h_lat, h_ctx, w_in, q_norm_g, w_q_up, kv_norm_g, w_kv_up, w_out, ctx_out):
    b, n, _ = h_lat.shape
    cos, sin = axial_rope_tables(n)
    u_lat = h_lat @ w_in
    u_ctx = h_ctx @ w_in
    q_lat = mla_queries(u_lat[..., :Q_LORA], q_norm_g, w_q_up, cos, sin)
    k_lat, v_lat = mla_keys_values(u_lat[..., Q_LORA:MLA_IN], kv_norm_g, w_kv_up, cos, sin)
    k_ctx, v_ctx = mla_keys_values(u_ctx[..., Q_LORA:MLA_IN], kv_norm_g, w_kv_up, None, None)
    a_lat = blocked_attend(q_lat, jnp.concatenate([k_ctx, k_lat], axis=1),
                           jnp.concatenate([v_ctx, v_lat], axis=1))
    y_lat = jnp.concatenate([a_lat.reshape(b, n, -1), fourier_mix(u_lat[..., MLA_IN:])], axis=-1) @ w_out
    y_ctx = None
    if ctx_out:
        q_ctx = mla_queries(u_ctx[..., :Q_LORA], q_norm_g, w_q_up, None, None)
        a_ctx = attend(q_ctx, k_ctx, v_ctx)
        y_ctx = jnp.concatenate([a_ctx.reshape(b, h_ctx.shape[1], -1), fourier_mix(u_ctx[..., MLA_IN:])], axis=-1) @ w_out
    return y_lat, y_ctx


def gla_project(u, w_gk_fwd, b_gk_fwd, w_gk_bwd, b_gk_bwd):
    b, n, _ = u.shape
    q, k, v, g_out, gd_fwd, gd_bwd = jnp.split(
        u, [GLA_KW, 2 * GLA_KW, 2 * GLA_KW + GLA_VW, 2 * GLA_KW + 2 * GLA_VW,
            2 * GLA_KW + 2 * GLA_VW + GATE_RANK], axis=-1)

    def heads(a, d):
        return a.reshape(b, n, GLA_HEADS, d).astype(jnp.float32)

    def decay(gd, w_up, b_up):
        z = (gd @ w_up + b_up).astype(jnp.float32)
        return heads(jax.nn.log_sigmoid(z) / GATE_NORMALIZER, GLA_DK)

    return (heads(q, GLA_DK) * (GLA_DK ** -0.5), heads(k, GLA_DK), heads(v, GLA_DV), g_out,
            decay(gd_fwd, w_gk_fwd, b_gk_fwd), decay(gd_bwd, w_gk_bwd, b_gk_bwd))


def gla_chunked(q, k, v, g, s0):
    b, n, h, _ = q.shape
    nc = n // GLA_CHUNK

    def to_chunks(a):
        return a.reshape(b, nc, GLA_CHUNK, h, a.shape[-1]).transpose(1, 0, 3, 2, 4)

    tril = jnp.tril(jnp.ones((GLA_CHUNK, GLA_CHUNK), dtype=bool))

    def step(state, inp):
        qc, kc, vc, gc = inp
        cum = jnp.cumsum(gc, axis=2)
        last = cum[:, :, -1:, :]
        q_dec = qc * jnp.exp(cum)
        k_dec = kc * jnp.exp(-cum)
        scores = jnp.where(tril, jnp.einsum("bhid,bhjd->bhij", q_dec, k_dec), 0.0)
        out = jnp.einsum("bhid,bhde->bhie", q_dec, state) + jnp.einsum("bhij,bhje->bhie", scores, vc)
        state = (jnp.exp(last)[:, :, 0, :, None] * state
                 + jnp.einsum("bhjd,bhje->bhde", kc * jnp.exp(last - cum), vc))
        return state, out

    s_fin, o = lax.scan(step, s0, (to_chunks(q), to_chunks(k), to_chunks(v), to_chunks(g)))
    return o.transpose(1, 0, 3, 2, 4).reshape(b, n, h, -1), s_fin


def gla_final_state(k, v, g):
    cum = jnp.cumsum(g, axis=1)
    return jnp.einsum("bnhd,bnhe->bhde", k * jnp.exp(cum[:, -1:] - cum), v)


def gla_output(o, g_out, gnorm_g, w_out):
    b, n = o.shape[:2]
    o = rmsnorm(o, gnorm_g).reshape(b, n, GLA_VW) * jax.nn.silu(g_out.astype(jnp.float32))
    return o.astype(w_out.dtype) @ w_out


def odd_mixer(h_lat, h_ctx, w_in, w_gk_fwd, b_gk_fwd, w_gk_bwd, b_gk_bwd, gnorm_g, w_out, ctx_out):
    gates = (w_gk_fwd, b_gk_fwd, w_gk_bwd, b_gk_bwd)
    q_c, k_c, v_c, go_c, gf_c, gb_c = gla_project(h_ctx @ w_in, *gates)
    y_ctx = None
    if ctx_out:
        zero = jnp.zeros((h_ctx.shape[0], GLA_HEADS, GLA_DK, GLA_DV), jnp.float32)
        o_f, s_fwd = gla_chunked(q_c, k_c, v_c, gf_c, zero)
        o_b, s_bwd = gla_chunked(_flip(q_c), _flip(k_c), _flip(v_c), _flip(gb_c), zero)
        y_ctx = gla_output(o_f + _flip(o_b), go_c, gnorm_g, w_out)
    else:
        s_fwd = gla_final_state(k_c, v_c, gf_c)
        s_bwd = gla_final_state(_flip(k_c), _flip(v_c), _flip(gb_c))
    q, k, v, go, gf, gb = gla_project(h_lat @ w_in, *gates)
    o_f, _ = gla_chunked(q, k, v, gf, s_fwd)
    o_b, _ = gla_chunked(_flip(q), _flip(k), _flip(v), _flip(gb), s_bwd)
    return gla_output(o_f + _flip(o_b), go, gnorm_g, w_out), y_ctx


def clamped_swiglu(gu):
    x_glu, x_lin = gu[..., ::2], gu[..., 1::2]
    x_glu = jnp.minimum(x_glu, SWIGLU_LIMIT)
    x_lin = jnp.clip(x_lin, -SWIGLU_LIMIT, SWIGLU_LIMIT)
    return x_glu * jax.nn.sigmoid(SWIGLU_ALPHA * x_glu) * (x_lin + 1)


def moe_ffn(h, router_w, router_b, w_gu, b_gu, w_down, b_down):
    n_tok, d = h.shape
    logits = (h @ router_w + router_b).astype(jnp.float32)
    top_val, top_idx = lax.top_k(logits, TOP_K)
    gates = jax.nn.softmax(top_val, axis=-1)
    n_rows = n_tok * TOP_K
    e_flat = top_idx.reshape(-1)
    tok_flat = jnp.repeat(jnp.arange(n_tok, dtype=jnp.int32), TOP_K)
    order = jnp.argsort(e_flat)
    e_sorted = e_flat[order]
    counts = jnp.bincount(e_flat, length=N_EXPERTS)
    starts = jnp.cumsum(counts) - counts
    padded = (counts + EXPERT_BLOCK - 1) // EXPERT_BLOCK * EXPERT_BLOCK
    pad_ends = jnp.cumsum(padded)
    pad_starts = pad_ends - padded
    dest = pad_starts[e_sorted] + jnp.arange(n_rows, dtype=jnp.int32) - starts[e_sorted]
    n_blocks = -(-(n_rows + N_EXPERTS * (EXPERT_BLOCK - 1)) // EXPERT_BLOCK)
    n_pad = n_blocks * EXPERT_BLOCK
    row_tok = jnp.full((n_pad,), n_tok, jnp.int32).at[dest].set(tok_flat[order])
    row_gate = jnp.zeros((n_pad,), jnp.float32).at[dest].set(gates.reshape(-1)[order])
    block_e = jnp.minimum(
        jnp.searchsorted(pad_ends, jnp.arange(n_blocks, dtype=jnp.int32) * EXPERT_BLOCK, side="right"),
        N_EXPERTS - 1)
    h_pad = jnp.concatenate([h, jnp.zeros((1, d), h.dtype)], axis=0)
    xb = h_pad[row_tok].reshape(n_blocks, EXPERT_BLOCK, d)

    def expert_block(args):
        xe, e = args
        gu = xe @ w_gu[e] + b_gu[e]
        return clamped_swiglu(gu) @ w_down[e] + b_down[e]

    yb = lax.map(expert_block, (xb, block_e)).reshape(n_pad, d)
    y = jax.ops.segment_sum(yb * row_gate[:, None].astype(yb.dtype), row_tok, num_segments=n_tok + 1)
    return y[:n_tok]


def setup_inputs(seed: int = 0) -> dict:
    key = jax.random.key(seed)
    keys = list(jax.random.split(key, 48))

    def normal(shape, scale=1.0):
        return scale * jax.random.normal(keys.pop(), shape, jnp.float32)

    def dense(shape, fan_in, scale=1.0):
        return normal(shape, scale * fan_in ** -0.5)

    def gain(n):
        return 1.0 + normal((n,), 0.05)

    D, F, E = D_MODEL, D_FF, N_EXPERTS
    inp = {}
    inp["x"] = normal((BATCH, SEQ, D))
    inp["c"] = normal((BATCH, D))
    inp["ctx"] = normal((BATCH, CTX_LEN, D))
    inp["c_ctx"] = normal((D,))
    inp["final_norm_g"] = gain(D)
    inp["l0_mod_w"] = dense((D, 6 * D), D, 0.5)
    inp["l0_mod_b"] = normal((6 * D,), 0.1)
    inp["l0_norm1_g"] = gain(D)
    inp["l0_w_in"] = dense((D, EVEN_IN), D)
    inp["l0_q_norm_g"] = gain(Q_LORA)
    inp["l0_w_q_up"] = dense((Q_LORA, MLA_HEADS * QK_DIM), Q_LORA)
    inp["l0_kv_norm_g"] = gain(KV_LORA)
    inp["l0_w_kv_up"] = dense((KV_LORA, MLA_HEADS * (NOPE_DIM + V_DIM)), KV_LORA)
    inp["l0_w_out"] = dense((EVEN_MIX, D), EVEN_MIX)
    inp["l0_norm2_g"] = gain(D)
    inp["l0_router_w"] = dense((D, E), D)
    inp["l0_router_b"] = normal((E,), 0.01)
    inp["l0_w_gu"] = dense((E, D, 2 * F), D)
    inp["l0_b_gu"] = normal((E, 2 * F), 0.02)
    inp["l0_w_down"] = dense((E, F, D), F)
    inp["l0_b_down"] = normal((E, D), 0.02)
    inp["l1_mod_w"] = dense((D, 6 * D), D, 0.5)
    inp["l1_mod_b"] = normal((6 * D,), 0.1)
    inp["l1_norm1_g"] = gain(D)
    inp["l1_w_in"] = dense((D, ODD_IN), D)
    inp["l1_w_gk_fwd"] = dense((GATE_RANK, GLA_KW), GATE_RANK)
    inp["l1_b_gk_fwd"] = normal((GLA_KW,), 0.1)
    inp["l1_w_gk_bwd"] = dense((GATE_RANK, GLA_KW), GATE_RANK)
    inp["l1_b_gk_bwd"] = normal((GLA_KW,), 0.1)
    inp["l1_gnorm_g"] = gain(GLA_DV)
    inp["l1_w_out"] = dense((ODD_MIX, D), ODD_MIX)
    inp["l1_norm2_g"] = gain(D)
    inp["l1_router_w"] = dense((D, E), D)
    inp["l1_router_b"] = normal((E,), 0.01)
    inp["l1_w_gu"] = dense((E, D, 2 * F), D)
    inp["l1_b_gu"] = normal((E, 2 * F), 0.02)
    inp["l1_w_down"] = dense((E, F, D), F)
    inp["l1_b_down"] = normal((E, D), 0.02)
    return inp


def reference(x, c, ctx, c_ctx, final_norm_g,
              l0_mod_w, l0_mod_b, l0_norm1_g, l0_w_in, l0_q_norm_g, l0_w_q_up, l0_kv_norm_g, l0_w_kv_up,
              l0_w_out, l0_norm2_g, l0_router_w, l0_router_b, l0_w_gu, l0_b_gu, l0_w_down, l0_b_down,
              l1_mod_w, l1_mod_b, l1_norm1_g, l1_w_in, l1_w_gk_fwd, l1_b_gk_fwd, l1_w_gk_bwd, l1_b_gk_bwd,
              l1_gnorm_g, l1_w_out, l1_norm2_g, l1_router_w, l1_router_b, l1_w_gu, l1_b_gu, l1_w_down,
              l1_b_down):
    common = [(l0_mod_w, l0_mod_b, l0_norm1_g, l0_norm2_g),
              (l1_mod_w, l1_mod_b, l1_norm1_g, l1_norm2_g)]
    mixers = [(even_mixer, (l0_w_in, l0_q_norm_g, l0_w_q_up, l0_kv_norm_g, l0_w_kv_up, l0_w_out)),
              (odd_mixer, (l1_w_in, l1_w_gk_fwd, l1_b_gk_fwd, l1_w_gk_bwd, l1_b_gk_bwd, l1_gnorm_g, l1_w_out))]
    experts = [(l0_router_w, l0_router_b, l0_w_gu, l0_b_gu, l0_w_down, l0_b_down),
               (l1_router_w, l1_router_b, l1_w_gu, l1_b_gu, l1_w_down, l1_b_down)]
    x_lat, x_ctx = x, ctx
    for i in range(DEPTH):
        last = i == DEPTH - 1
        mod_w, mod_b, norm1_g, norm2_g = common[i]
        mixer, mixer_params = mixers[i]
        sh1, sc1, g1, sh2, sc2, g2 = adaln(c, mod_w, mod_b)
        csh1, csc1, cg1, csh2, csc2, cg2 = adaln(c_ctx, mod_w, mod_b)
        y_lat, y_ctx = mixer(modulate(x_lat, norm1_g, sh1, sc1), modulate(x_ctx, norm1_g, csh1, csc1),
                             *mixer_params, ctx_out=not last)
        x_lat = x_lat + g1 * y_lat
        h_lat = modulate(x_lat, norm2_g, sh2, sc2)
        b, n, d = h_lat.shape
        if last:
            x_lat = x_lat + g2 * moe_ffn(h_lat.reshape(b * n, d), *experts[i]).reshape(b, n, d)
        else:
            x_ctx = x_ctx + cg1 * y_ctx
            h_ctx = modulate(x_ctx, norm2_g, csh2, csc2)
            m = h_ctx.shape[1]
            y = moe_ffn(jnp.concatenate([h_ctx.reshape(b * m, d), h_lat.reshape(b * n, d)], axis=0), *experts[i])
            x_ctx = x_ctx + cg2 * y[: b * m].reshape(b, m, d)
            x_lat = x_lat + g2 * y[b * m:].reshape(b, n, d)
    return rmsnorm(x_lat, final_norm_g)
```

```python
import functools

import jax
import jax.numpy as jnp
import numpy as np
from jax import lax
from jax.experimental import pallas as pl
from jax.experimental.pallas import tpu as pltpu

D = 1024
B = 8
SEQ = 4096
CTX = 256
GRID_W = 64
EPS = 1e-6
HEADS = 6
Q_LORA = 384
KV_LORA = 256
NOPE = 128
ROPE = 64
VDIM = 128
QK = NOPE + ROPE
ROPE_BASE = 10000.0
FG = 4
FGD = 64
FW = FG * FGD
GH = 4
GDK = 128
GDV = 256
GRANK = 16
GNORM = 16.0
GCHUNK = 64
GKW = GH * GDK
GVW = GH * GDV
NE = 32
TOPK = 4
DFF = 1024
ALPHA = 1.702
LIMIT = 7.0

LANES = 128
TILE = 256
ROWS_B = CTX + SEQ
TPB = ROWS_B // TILE
T_ALL = B * ROWS_B
NT_ALL = T_ALL // TILE
T_LAT = B * SEQ
NT_LAT = T_LAT // TILE
LPB = SEQ // TILE
SUPER = 2048
SUB = 256
NSUB = SUPER // SUB
EBLK = 128
KEYMUL = 4096
MAXI = SUPER * TOPK // EBLK + NE
NCH_B = ROWS_B // GCHUNK
NCH_CTX = CTX // GCHUNK
VMEM_BIG = 56 * 1024 * 1024
VMEM_MID = 40 * 1024 * 1024

F32 = jnp.float32
BF16 = jnp.bfloat16
HI = lax.Precision.HIGHEST
NT_DIMS = (((1,), (1,)), ((), ()))
TN_DIMS = (((0,), (0,)), ((), ()))


def _cp(sem, vmem=VMEM_MID):
    return pltpu.CompilerParams(dimension_semantics=sem, vmem_limit_bytes=vmem)


def _rms(x, g):
    return x * lax.rsqrt(jnp.mean(x * x, axis=-1, keepdims=True) + EPS) * g


def _modulate(x, g, sh, sc):
    return _rms(x, g) * (1.0 + sc) + sh


def _silu(x):
    return x * jax.nn.sigmoid(x)


def _mod_index(i):
    return jnp.where(i % TPB == 0, B, i // TPB)


def _lat_tile(n):
    return n + n // LPB + 1


def _adaln_body(c_ref, w_ref, b_ref, o_ref):
    s = _silu(c_ref[...])
    o_ref[...] = jnp.dot(s, w_ref[...], precision=HI, preferred_element_type=F32) + b_ref[...]


def adaln(cc, mod_w, mod_b):
    tn = 1536
    out = pl.pallas_call(
        _adaln_body,
        grid=(6 * D // tn,),
        in_specs=[pl.BlockSpec((16, D), lambda j: (0, 0)),
                  pl.BlockSpec((D, tn), lambda j: (0, j)),
                  pl.BlockSpec((1, tn), lambda j: (0, j))],
        out_specs=pl.BlockSpec((16, tn), lambda j: (0, j)),
        out_shape=jax.ShapeDtypeStruct((16, 6 * D), F32),
        compiler_params=_cp(("arbitrary",)),
        name="adaln",
    )(cc, mod_w, mod_b.reshape(1, 6 * D))
    return out.reshape(16, 6, D)


def _even_in_body(x_ref, mod_ref, n1_ref, wqa_ref, wkva_ref, wkr_ref, wf_ref, qg_ref, wqup_ref,
                  kvg_ref, wkvup_ref, cos_ref, sin_ref, q_ref, k_ref, v_ref, uf_ref):
    m = mod_ref[0]
    h = _modulate(x_ref[...], n1_ref[...], m[0:1], m[1:2]).astype(BF16)
    cq = jnp.dot(h, wqa_ref[...], preferred_element_type=F32)
    ckv = jnp.dot(h, wkva_ref[...], preferred_element_type=F32)
    kr2 = jnp.dot(h, wkr_ref[...], preferred_element_type=F32)
    uf_ref[...] = jnp.dot(h, wf_ref[...], preferred_element_type=F32).astype(BF16)
    cos6 = cos_ref[...]
    sin6 = sin_ref[...]
    scale = QK ** -0.5
    qall = jnp.dot(_rms(cq, qg_ref[...]).astype(BF16), wqup_ref[...], preferred_element_type=F32)
    qn = qall[:, :HEADS * NOPE] * scale
    qr = (qall[:, HEADS * NOPE:HEADS * QK] * cos6 + qall[:, HEADS * QK:] * sin6) * scale
    kvall = jnp.dot(_rms(ckv, kvg_ref[...]).astype(BF16), wkvup_ref[...], preferred_element_type=F32)
    kr = (kr2[:, :ROPE] * cos6[:, :ROPE] + kr2[:, ROPE:] * sin6[:, :ROPE]).astype(BF16)
    for hd in range(HEADS):
        q_ref[0, hd, :, 0:NOPE] = qn[:, hd * NOPE:(hd + 1) * NOPE].astype(BF16)
        q_ref[0, hd, :, NOPE:QK] = qr[:, hd * ROPE:(hd + 1) * ROPE].astype(BF16)
        k_ref[0, hd, :, 0:NOPE] = kvall[:, hd * NOPE:(hd + 1) * NOPE].astype(BF16)
        k_ref[0, hd, :, NOPE:QK] = kr
        v_ref[0, hd] = kvall[:, HEADS * NOPE + hd * VDIM:HEADS * NOPE + (hd + 1) * VDIM].astype(BF16)


def even_in(x, mod, n1g, wqa, wkva, wkr2, wf, qg, wqup, kvg, wkvup, cos6, sin6):
    full = lambda a: pl.BlockSpec(a.shape, lambda i: (0,) * a.ndim)
    hs = lambda w: pl.BlockSpec((1, HEADS, TILE, w), lambda i: (i // TPB, 0, i % TPB, 0))
    return pl.pallas_call(
        _even_in_body,
        grid=(NT_ALL,),
        in_specs=[pl.BlockSpec((TILE, D), lambda i: (i, 0)),
                  pl.BlockSpec((1, 6, D), lambda i: (_mod_index(i), 0, 0)),
                  full(n1g), full(wqa), full(wkva), full(wkr2), full(wf), full(qg), full(wqup),
                  full(kvg), full(wkvup),
                  pl.BlockSpec((TILE, HEADS * ROPE), lambda i: (i % TPB, 0)),
                  pl.BlockSpec((TILE, HEADS * ROPE), lambda i: (i % TPB, 0))],
        out_specs=[hs(QK), hs(QK), hs(VDIM), pl.BlockSpec((TILE, FW), lambda i: (i, 0))],
        out_shape=[jax.ShapeDtypeStruct((B, HEADS, ROWS_B, QK), BF16),
                   jax.ShapeDtypeStruct((B, HEADS, ROWS_B, QK), BF16),
                   jax.ShapeDtypeStruct((B, HEADS, ROWS_B, VDIM), BF16),
                   jax.ShapeDtypeStruct((T_ALL, FW), BF16)],
        compiler_params=_cp(("arbitrary",)),
        name="even_in",
    )(x, mod, n1g, wqa, wkva, wkr2, wf, qg, wqup, kvg, wkvup, cos6, sin6)


def _softmax_pv(s, v):
    m = jnp.max(s, axis=-1, keepdims=True)
    p = jnp.exp(s - m)
    l = jnp.sum(p, axis=-1, keepdims=True)
    return jnp.dot(p.astype(BF16), v, preferred_element_type=F32) / l


def _attn_body(q_ref, k_ref, v_ref, o_ref):
    qt = pl.program_id(2)
    q = q_ref[0, 0]

    @pl.when(qt == 0)
    def _():
        s = lax.dot_general(q, k_ref[0, 0, 0:CTX, :], NT_DIMS, preferred_element_type=F32)
        o_ref[0] = _softmax_pv(s, v_ref[0, 0, 0:CTX, :]).astype(BF16)

    @pl.when(qt > 0)
    def _():
        s = lax.dot_general(q, k_ref[0, 0], NT_DIMS, preferred_element_type=F32)
        o_ref[0] = _softmax_pv(s, v_ref[0, 0]).astype(BF16)


def attention(q, k, v):
    return pl.pallas_call(
        _attn_body,
        grid=(B, HEADS, TPB),
        in_specs=[pl.BlockSpec((1, 1, TILE, QK), lambda b, h, t: (b, h, t, 0)),
                  pl.BlockSpec((1, 1, ROWS_B, QK), lambda b, h, t: (b, h, 0, 0)),
                  pl.BlockSpec((1, 1, ROWS_B, VDIM), lambda b, h, t: (b, h, 0, 0))],
        out_specs=pl.BlockSpec((1, TILE, VDIM), lambda b, h, t: (b, t, h)),
        out_shape=jax.ShapeDtypeStruct((B, ROWS_B, HEADS * VDIM), BF16),
        compiler_params=_cp(("arbitrary", "arbitrary", "arbitrary"), VMEM_BIG),
        name="attention",
    )(q, k, v)


def _fourier_body(u_ref, cn_ref, sn_ref, c2_ref, s2_ref, cc_ref, sc_ref, o_ref):
    i = pl.program_id(0)

    def finish(a, b):
        o_ref[0] = (jnp.dot(a.astype(BF16), cc_ref[...], preferred_element_type=F32)
                    - jnp.dot(b.astype(BF16), sc_ref[...], preferred_element_type=F32)).astype(BF16)

    @pl.when(i == 0)
    def _():
        u = u_ref[0, 0:CTX, :]
        finish(jnp.dot(c2_ref[...], u, preferred_element_type=F32),
               jnp.dot(s2_ref[...], u, preferred_element_type=F32))

    @pl.when(i > 0)
    def _():
        u = u_ref[0, CTX:ROWS_B, :]
        finish(jnp.dot(cn_ref[...], u, preferred_element_type=F32),
               jnp.dot(sn_ref[...], u, preferred_element_type=F32))


def fourier(uf, cn, sn, c2, s2, ccb, scb):
    full = lambda a: pl.BlockSpec(a.shape, lambda i, b: (0,) * a.ndim)
    return pl.pallas_call(
        _fourier_body,
        grid=(TPB, B),
        in_specs=[pl.BlockSpec((1, ROWS_B, FW), lambda i, b: (b, 0, 0)),
                  pl.BlockSpec((TILE, SEQ), lambda i, b: (jnp.maximum(i - 1, 0), 0)),
                  pl.BlockSpec((TILE, SEQ), lambda i, b: (jnp.maximum(i - 1, 0), 0)),
                  full(c2), full(s2), full(ccb), full(scb)],
        out_specs=pl.BlockSpec((1, TILE, FW), lambda i, b: (b, i, 0)),
        out_shape=jax.ShapeDtypeStruct((B, ROWS_B, FW), BF16),
        compiler_params=_cp(("arbitrary", "arbitrary")),
        name="fourier",
    )(uf, cn, sn, c2, s2, ccb, scb)


def _route(h2, rw_ref, rb_ref, carry_ref, first, key_ref, gate_ref, cum_ref):
    @pl.when(first)
    def _():
        carry_ref[...] = jnp.zeros_like(carry_ref)

    logits = jnp.dot(h2, rw_ref[...], precision=HI, preferred_element_type=F32) + rb_ref[...]
    lane = lax.broadcasted_iota(jnp.int32, logits.shape, 1)
    l = logits
    vals, idxs = [], []
    for _ in range(TOPK):
        mx = jnp.max(l, axis=-1, keepdims=True)
        am = jnp.min(jnp.where(l == mx, lane, LANES), axis=-1, keepdims=True)
        vals.append(mx)
        idxs.append(am)
        l = jnp.where(lane == am, -jnp.inf, l)
    ex = [jnp.exp(vv - vals[0]) for vv in vals]
    den = ex[0] + ex[1] + ex[2] + ex[3]
    onehot = jnp.zeros(logits.shape, F32)
    for am in idxs:
        onehot = onehot + (lane == am).astype(F32)
    r = lax.broadcasted_iota(jnp.int32, (TILE, TILE), 0)
    c = lax.broadcasted_iota(jnp.int32, (TILE, TILE), 1)
    strict = (c < r).astype(BF16)
    carry = carry_ref[0:1, :]
    excl = jnp.dot(strict, onehot.astype(BF16), preferred_element_type=F32) + carry
    key = jnp.zeros(logits.shape, jnp.int32)
    gate = jnp.zeros(logits.shape, F32)
    for kk in range(TOPK):
        pos = jnp.sum(jnp.where(lane == idxs[kk], excl, 0.0), axis=-1, keepdims=True)
        kv = idxs[kk] * KEYMUL + pos.astype(jnp.int32)
        key = jnp.where(lane == kk, kv, key)
        gate = jnp.where(lane == kk, ex[kk] / den, gate)
    key_ref[...] = key
    gate_ref[...] = gate
    new = carry + jnp.sum(onehot, axis=0, keepdims=True)
    cum_ref[0] = jnp.concatenate([carry, new, jnp.zeros((6, LANES), F32)], axis=0)
    carry_ref[0:1, :] = new


def _route_out_specs(n_tiles):
    specs = [pl.BlockSpec((TILE, LANES), lambda i: (i, 0)),
             pl.BlockSpec((TILE, LANES), lambda i: (i, 0)),
             pl.BlockSpec((1, 8, LANES), lambda i: (i, 0, 0))]
    shapes = [jax.ShapeDtypeStruct((n_tiles * TILE, LANES), jnp.int32),
              jax.ShapeDtypeStruct((n_tiles * TILE, LANES), F32),
              jax.ShapeDtypeStruct((n_tiles, 8, LANES), F32)]
    return specs, shapes


def _out0_body(a_ref, f_ref, x_ref, mod_ref, woa_ref, wof_ref, n2_ref, rw_ref, rb_ref,
               x1_ref, h2_ref, key_ref, gate_ref, cum_ref, carry_ref):
    i = pl.program_id(0)
    m = mod_ref[0]
    y = (jnp.dot(a_ref[...], woa_ref[...], preferred_element_type=F32)
         + jnp.dot(f_ref[...], wof_ref[...], preferred_element_type=F32))
    x1 = x_ref[...] + m[2:3] * y
    x1_ref[...] = x1
    h2 = _modulate(x1, n2_ref[...], m[3:4], m[4:5])
    h2_ref[...] = h2.astype(BF16)
    _route(h2, rw_ref, rb_ref, carry_ref, i % NSUB == 0, key_ref, gate_ref, cum_ref)


def out0(a, f, x, mod, woa, wof, n2g, rw, rb):
    full = lambda w: pl.BlockSpec(w.shape, lambda i: (0,) * w.ndim)
    rspecs, rshapes = _route_out_specs(NT_ALL)
    return pl.pallas_call(
        _out0_body,
        grid=(NT_ALL,),
        in_specs=[pl.BlockSpec((TILE, HEADS * VDIM), lambda i: (i, 0)),
                  pl.BlockSpec((TILE, FW), lambda i: (i, 0)),
                  pl.BlockSpec((TILE, D), lambda i: (i, 0)),
                  pl.BlockSpec((1, 6, D), lambda i: (_mod_index(i), 0, 0)),
                  full(woa), full(wof), full(n2g), full(rw), full(rb)],
        out_specs=[pl.BlockSpec((TILE, D), lambda i: (i, 0)),
                   pl.BlockSpec((TILE, D), lambda i: (i, 0))] + rspecs,
        out_shape=[jax.ShapeDtypeStruct((T_ALL, D), F32),
                   jax.ShapeDtypeStruct((T_ALL, D), BF16)] + rshapes,
        scratch_shapes=[pltpu.VMEM((8, LANES), F32)],
        compiler_params=_cp(("arbitrary",)),
        name="out0",
    )(a, f, x, mod, woa, wof, n2g, rw, rb)


def _moe_body(ie_ref, ij_ref, ilo_ref, ihi_ref, iv_ref,
              h_ref, keyt_ref, gatet_ref, key_ref, wg_ref, wl_ref, bg_ref, bl_ref, wd_ref, bd_ref,
              y_ref, xb_ref, gr_ref):
    s = pl.program_id(0)
    i = pl.program_id(1)
    it = s * MAXI + i

    @pl.when(i == 0)
    def _():
        y_ref[...] = jnp.zeros_like(y_ref)

    @pl.when(iv_ref[it] == 1)
    def _():
        base = ie_ref[it] * KEYMUL + ij_ref[it] * EBLK
        lo = ilo_ref[it]
        hi = ihi_ref[it]
        rowkey = base + lax.broadcasted_iota(jnp.int32, (EBLK, SUB), 0)
        xb_ref[...] = jnp.zeros_like(xb_ref)
        gr_ref[...] = jnp.zeros_like(gr_ref)

        def gather(u, carry):
            kt = keyt_ref[u]
            gt = gatet_ref[u]
            hit = jnp.zeros((EBLK, SUB), jnp.bool_)
            gsum = jnp.zeros((EBLK, SUB), F32)
            for kk in range(TOPK):
                eq = kt[kk:kk + 1, :] == rowkey
                hit = jnp.logical_or(hit, eq)
                gsum = gsum + jnp.where(eq, gt[kk:kk + 1, :], 0.0)
            off = pl.multiple_of(u * SUB, SUB)
            xb_ref[...] += jnp.dot(hit.astype(BF16), h_ref[pl.ds(off, SUB), :],
                                   preferred_element_type=F32)
            gr_ref[...] += gsum
            return carry

        lax.fori_loop(lo, hi, gather, 0)
        xb = xb_ref[...].astype(BF16)
        glu = jnp.dot(xb, wg_ref[0], preferred_element_type=F32) + bg_ref[0]
        lin = jnp.dot(xb, wl_ref[0], preferred_element_type=F32) + bl_ref[0]
        glu = jnp.minimum(glu, LIMIT)
        lin = jnp.clip(lin, -LIMIT, LIMIT)
        act = glu * jax.nn.sigmoid(ALPHA * glu) * (lin + 1.0)
        out = jnp.dot(act.astype(BF16), wd_ref[0], preferred_element_type=F32) + bd_ref[0]
        rowgate = jnp.sum(gr_ref[...], axis=-1, keepdims=True)
        outg = (out * rowgate).astype(BF16)
        colkey = base + lax.broadcasted_iota(jnp.int32, (SUB, EBLK), 1)

        def combine(u, carry):
            off = pl.multiple_of(u * SUB, SUB)
            ks = key_ref[pl.ds(off, SUB), :]
            hit = jnp.zeros((SUB, EBLK), jnp.bool_)
            for kk in range(TOPK):
                hit = jnp.logical_or(hit, ks[:, kk:kk + 1] == colkey)
            y_ref[pl.ds(off, SUB), :] += jnp.dot(hit.astype(BF16), outg, preferred_element_type=F32)
            return carry

        lax.fori_loop(lo, hi, combine, 0)


def moe(h2, key, gate, cum, wg, wl, bg, bl, wd, bd):
    n_tok = h2.shape[0]
    n_super = n_tok // SUPER
    cum = cum[:, :2, :NE].astype(jnp.int32).reshape(n_super, NSUB, 2, NE)
    cs, ce = cum[:, :, 0, :], cum[:, :, 1, :]
    counts = ce[:, -1, :]
    nblk = (counts + EBLK - 1) // EBLK
    bend = jnp.cumsum(nblk, axis=1)
    bstart = bend - nblk
    total = bend[:, -1]
    ii = jnp.arange(MAXI, dtype=jnp.int32)

    def items(cs_s, ce_s, counts_s, bend_s, bstart_s, total_s):
        valid = ii < total_s
        e = jnp.minimum(jnp.searchsorted(bend_s, ii, side="right"), NE - 1).astype(jnp.int32)
        e_last = jnp.minimum(jnp.searchsorted(bend_s, total_s - 1, side="right"), NE - 1).astype(jnp.int32)
        e = jnp.where(valid, e, e_last)
        j = jnp.where(valid, ii - bstart_s[e], 0)
        r0 = j * EBLK
        r1 = jnp.minimum(r0 + EBLK, counts_s[e])
        lo = jnp.sum(ce_s[:, e] <= r0[None, :], axis=0)
        hi = jnp.sum(cs_s[:, e] < r1[None, :], axis=0)
        return e, j, lo.astype(jnp.int32), hi.astype(jnp.int32), valid.astype(jnp.int32)

    ie, ij, ilo, ihi, iv = jax.vmap(items)(cs, ce, counts, bend, bstart, total)
    flat = lambda a: a.reshape(-1)
    key4 = key[:, :TOPK]
    keyt = key4.reshape(n_tok // SUB, SUB, TOPK).transpose(0, 2, 1)
    gatet = gate[:, :TOPK].reshape(n_tok // SUB, SUB, TOPK).transpose(0, 2, 1)

    wspec = lambda shp: pl.BlockSpec((1,) + shp, lambda s, i, ie, ij, ilo, ihi, iv: (ie[s * MAXI + i], 0, 0))
    grid_spec = pltpu.PrefetchScalarGridSpec(
        num_scalar_prefetch=5,
        grid=(n_super, MAXI),
        in_specs=[pl.BlockSpec((SUPER, D), lambda s, i, *_: (s, 0)),
                  pl.BlockSpec((NSUB, TOPK, SUB), lambda s, i, *_: (s, 0, 0)),
                  pl.BlockSpec((NSUB, TOPK, SUB), lambda s, i, *_: (s, 0, 0)),
                  pl.BlockSpec((SUPER, LANES), lambda s, i, *_: (s, 0)),
                  wspec((D, DFF)), wspec((D, DFF)), wspec((1, DFF)), wspec((1, DFF)),
                  wspec((DFF, D)), wspec((1, D))],
        out_specs=pl.BlockSpec((SUPER, D), lambda s, i, *_: (s, 0)),
        scratch_shapes=[pltpu.VMEM((EBLK, D), F32), pltpu.VMEM((EBLK, SUB), F32)],
    )
    return pl.pallas_call(
        _moe_body,
        grid_spec=grid_spec,
        out_shape=jax.ShapeDtypeStruct((n_tok, D), F32),
        compiler_params=_cp(("arbitrary", "arbitrary"), VMEM_BIG),
        name="moe",
    )(flat(ie), flat(ij), flat(ilo), flat(ihi), flat(iv),
      h2, keyt, gatet, key, wg, wl, bg, bl, wd, bd)


def _odd_in_body(x_ref, y_ref, mod0_ref, mod1_ref, n1_ref, wq_ref, wk_ref, wv_ref, wgo_ref, wgd_ref,
                 wgkf_ref, bgkf_ref, wgkb_ref, bgkb_ref,
                 x2_ref, q_ref, k_ref, v_ref, go_ref, gf_ref, gb_ref):
    m0 = mod0_ref[0]
    m1 = mod1_ref[0]
    x2 = x_ref[...] + m0[5:6] * y_ref[...]
    x2_ref[...] = x2
    h = _modulate(x2, n1_ref[...], m1[0:1], m1[1:2]).astype(BF16)
    q_ref[...] = jnp.dot(h, wq_ref[...], preferred_element_type=F32) * (GDK ** -0.5)
    k_ref[...] = jnp.dot(h, wk_ref[...], preferred_element_type=F32)
    v_ref[...] = jnp.dot(h, wv_ref[...], preferred_element_type=F32).astype(BF16)
    go_ref[...] = jnp.dot(h, wgo_ref[...], preferred_element_type=F32).astype(BF16)
    gd = jnp.dot(h, wgd_ref[...], preferred_element_type=F32)

    def decay(w_ref, b_ref):
        z = jnp.dot(gd, w_ref[...], precision=HI, preferred_element_type=F32) + b_ref[...]
        return (jnp.minimum(z, 0.0) - jnp.log(1.0 + jnp.exp(-jnp.abs(z)))) / GNORM

    gf_ref[...] = decay(wgkf_ref, bgkf_ref)
    gb_ref[...] = decay(wgkb_ref, bgkb_ref)


def odd_in(x1, y0, mod0, mod1, n1g, wq, wk, wv, wgo, wgd, wgkf, bgkf, wgkb, bgkb):
    full = lambda w: pl.BlockSpec(w.shape, lambda i: (0,) * w.ndim)
    row = lambda w: pl.BlockSpec((TILE, w), lambda i: (i, 0))
    modspec = pl.BlockSpec((1, 6, D), lambda i: (_mod_index(i), 0, 0))
    return pl.pallas_call(
        _odd_in_body,
        grid=(NT_ALL,),
        in_specs=[row(D), row(D), modspec, modspec, full(n1g), full(wq), full(wk), full(wv), full(wgo),
                  full(wgd), full(wgkf), full(bgkf), full(wgkb), full(bgkb)],
        out_specs=[row(D), row(GKW), row(GKW), row(GVW), row(GVW), row(GKW), row(GKW)],
        out_shape=[jax.ShapeDtypeStruct((T_ALL, D), F32),
                   jax.ShapeDtypeStruct((T_ALL, GKW), F32),
                   jax.ShapeDtypeStruct((T_ALL, GKW), F32),
                   jax.ShapeDtypeStruct((T_ALL, GVW), BF16),
                   jax.ShapeDtypeStruct((T_ALL, GVW), BF16),
                   jax.ShapeDtypeStruct((T_ALL, GKW), F32),
                   jax.ShapeDtypeStruct((T_ALL, GKW), F32)],
        compiler_params=_cp(("arbitrary",)),
        name="odd_in",
    )(x1, y0, mod0, mod1, n1g, wq, wk, wv, wgo, wgd, wgkf, bgkf, wgkb, bgkb)


def _gla_dir(q, k, v, g, s_ref, tri, reverse):
    outs = []
    for hd in range(GH):
        ks = slice(hd * GDK, (hd + 1) * GDK)
        vs = slice(hd * GDV, (hd + 1) * GDV)
        cum = jnp.dot(tri, g[:, ks], precision=HI, preferred_element_type=F32)
        tot = cum[0:1, :] if reverse else cum[GCHUNK - 1:GCHUNK, :]
        qd = (q[:, ks] * jnp.exp(cum)).astype(BF16)
        kd = (k[:, ks] * jnp.exp(-cum)).astype(BF16)
        ke = (k[:, ks] * jnp.exp(tot - cum)).astype(BF16)
        sc = lax.dot_general(qd, kd, NT_DIMS, preferred_element_type=F32)
        sc = jnp.where(tri > 0, sc, 0.0).astype(BF16)
        st = s_ref[hd]
        vv = v[:, vs]
        outs.append(lax.dot_general(qd, st.astype(BF16), NT_DIMS, preferred_element_type=F32)
                    + jnp.dot(sc, vv, preferred_element_type=F32))
        s_ref[hd] = st * jnp.exp(tot) + lax.dot_general(vv, ke, TN_DIMS, preferred_element_type=F32)
    return jnp.concatenate(outs, axis=-1)


def _gla_body(qf_ref, kf_ref, vf_ref, gf_ref, qb_ref, kb_ref, vb_ref, gb_ref, of_ref, ob_ref, sf_ref, sb_ref):
    @pl.when(pl.program_id(1) == 0)
    def _():
        sf_ref[...] = jnp.zeros_like(sf_ref)
        sb_ref[...] = jnp.zeros_like(sb_ref)

    r = lax.broadcasted_iota(jnp.int32, (GCHUNK, GCHUNK), 0)
    c = lax.broadcasted_iota(jnp.int32, (GCHUNK, GCHUNK), 1)
    lower = (c <= r).astype(F32)
    upper = (c >= r).astype(F32)
    of_ref[...] = _gla_dir(qf_ref[...], kf_ref[...], vf_ref[...], gf_ref[...], sf_ref, lower, False)
    ob_ref[...] = _gla_dir(qb_ref[...], kb_ref[...], vb_ref[...], gb_ref[...], sb_ref, upper, True)


def _bwd_chunk(c):
    return jnp.where(c < NCH_CTX, NCH_CTX - 1 - c, NCH_B + NCH_CTX - 1 - c)


def gla(q, k, v, gf, gb):
    fw = lambda w: pl.BlockSpec((GCHUNK, w), lambda b, c: (b * NCH_B + c, 0))
    bw = lambda w: pl.BlockSpec((GCHUNK, w), lambda b, c: (b * NCH_B + _bwd_chunk(c), 0))
    return pl.pallas_call(
        _gla_body,
        grid=(B, NCH_B),
        in_specs=[fw(GKW), fw(GKW), fw(GVW), fw(GKW), bw(GKW), bw(GKW), bw(GVW), bw(GKW)],
        out_specs=[fw(GVW), bw(GVW)],
        out_shape=[jax.ShapeDtypeStruct((T_ALL, GVW), F32), jax.ShapeDtypeStruct((T_ALL, GVW), F32)],
        scratch_shapes=[pltpu.VMEM((GH, GDV, GDK), F32), pltpu.VMEM((GH, GDV, GDK), F32)],
        compiler_params=_cp(("arbitrary", "arbitrary")),
        name="gla",
    )(q, k, v, gf, q, k, v, gb)


def _out1_body(of_ref, ob_ref, go_ref, x_ref, mod_ref, gn_ref, wo_ref, n2_ref, rw_ref, rb_ref,
               x3_ref, h2_ref, key_ref, gate_ref, cum_ref, carry_ref):
    n = pl.program_id(0)
    m = mod_ref[0]
    o = of_ref[...] + ob_ref[...]
    gn = gn_ref[...]
    parts = [_rms(o[:, hd * GDV:(hd + 1) * GDV], gn) for hd in range(GH)]
    on = jnp.concatenate(parts, axis=-1) * _silu(go_ref[...].astype(F32))
    y = jnp.dot(on.astype(BF16), wo_ref[...], preferred_element_type=F32)
    x3 = x_ref[...] + m[2:3] * y
    x3_ref[...] = x3
    h2 = _modulate(x3, n2_ref[...], m[3:4], m[4:5])
    h2_ref[...] = h2.astype(BF16)
    _route(h2, rw_ref, rb_ref, carry_ref, n % NSUB == 0, key_ref, gate_ref, cum_ref)


def out1(of, ob, go, x2, mod, gng, wo, n2g, rw, rb):
    full = lambda w: pl.BlockSpec(w.shape, lambda n: (0,) * w.ndim)
    lat = lambda w: pl.BlockSpec((TILE, w), lambda n: (_lat_tile(n), 0))
    rspecs, rshapes = _route_out_specs(NT_LAT)
    return pl.pallas_call(
        _out1_body,
        grid=(NT_LAT,),
        in_specs=[lat(GVW), lat(GVW), lat(GVW), lat(D),
                  pl.BlockSpec((1, 6, D), lambda n: (n // LPB, 0, 0)),
                  full(gng), full(wo), full(n2g), full(rw), full(rb)],
        out_specs=[pl.BlockSpec((TILE, D), lambda n: (n, 0)),
                   pl.BlockSpec((TILE, D), lambda n: (n, 0))] + rspecs,
        out_shape=[jax.ShapeDtypeStruct((T_LAT, D), F32),
                   jax.ShapeDtypeStruct((T_LAT, D), BF16)] + rshapes,
        scratch_shapes=[pltpu.VMEM((8, LANES), F32)],
        compiler_params=_cp(("arbitrary",)),
        name="out1",
    )(of, ob, go, x2, mod, gng, wo, n2g, rw, rb)


def _final_body(x_ref, y_ref, mod_ref, g_ref, o_ref):
    m = mod_ref[0]
    o_ref[...] = _rms(x_ref[...] + m[5:6] * y_ref[...], g_ref[...])


def final(x3, y1, mod, g):
    return pl.pallas_call(
        _final_body,
        grid=(NT_LAT,),
        in_specs=[pl.BlockSpec((TILE, D), lambda n: (n, 0)),
                  pl.BlockSpec((TILE, D), lambda n: (n, 0)),
                  pl.BlockSpec((1, 6, D), lambda n: (n // LPB, 0, 0)),
                  pl.BlockSpec((1, D), lambda n: (0, 0))],
        out_specs=pl.BlockSpec((TILE, D), lambda n: (n, 0)),
        out_shape=jax.ShapeDtypeStruct((T_LAT, D), F32),
        compiler_params=_cp(("arbitrary",)),
        name="final",
    )(x3, y1, mod, g)


def _rot_cols(w):
    a, b, c, d = jnp.split(w, 4, axis=-1)
    return jnp.concatenate([-b, a, -d, c], axis=-1)


def _rope_tables():
    n = np.arange(SEQ)
    row = (n // GRID_W).astype(np.float32)
    col = (n % GRID_W).astype(np.float32)
    axis_dim = ROPE // 2
    inv = (ROPE_BASE ** (-np.arange(0, axis_dim, 2, dtype=np.float32) / axis_dim)).astype(np.float32)
    ar = row[:, None] * inv
    ac = col[:, None] * inv
    ang = np.concatenate([ar, ar, ac, ac], axis=-1).astype(np.float32)
    cos = np.concatenate([np.ones((CTX, ROPE), np.float32), np.cos(ang)], axis=0)
    sin = np.concatenate([np.zeros((CTX, ROPE), np.float32), np.sin(ang)], axis=0)
    return jnp.asarray(np.tile(cos, (1, HEADS))), jnp.asarray(np.tile(sin, (1, HEADS)))


def _dft(n, scale):
    j = np.arange(n, dtype=np.int64)
    ang = 2.0 * np.pi * ((j[:, None] * j[None, :]) % n).astype(np.float64) / n
    return (np.cos(ang) * scale), (np.sin(ang) * scale)


def _dft_tables():
    cn, sn = _dft(SEQ, SEQ ** -0.5)
    c2, s2 = _dft(CTX, CTX ** -0.5)
    cg, sg = _dft(FGD, FGD ** -0.5)
    eye = np.eye(FG)
    to = lambda a: jnp.asarray(a.astype(np.float32)).astype(BF16)
    return to(cn), to(sn), to(c2), to(s2), to(np.kron(eye, cg)), to(np.kron(eye, sg))


def _router_pad(rw, rb):
    rwp = jnp.zeros((D, LANES), F32).at[:, :NE].set(rw)
    rbp = jnp.full((1, LANES), -1e30, F32).at[0, :NE].set(rb)
    return rwp, rbp


def _expert_weights(w_gu, b_gu, w_down, b_down):
    wgu = w_gu.reshape(NE, D, DFF, 2)
    bgu = b_gu.reshape(NE, 1, DFF, 2)
    return (wgu[..., 0].astype(BF16), wgu[..., 1].astype(BF16), bgu[..., 0], bgu[..., 1],
            w_down.astype(BF16), b_down.reshape(NE, 1, D))


def kernel(x, c, ctx, c_ctx, final_norm_g, l0_mod_w, l0_mod_b, l0_norm1_g, l0_w_in, l0_q_norm_g, l0_w_q_up, l0_kv_norm_g, l0_w_kv_up, l0_w_out, l0_norm2_g, l0_router_w, l0_router_b, l0_w_gu, l0_b_gu, l0_w_down, l0_b_down, l1_mod_w, l1_mod_b, l1_norm1_g, l1_w_in, l1_w_gk_fwd, l1_b_gk_fwd, l1_w_gk_bwd, l1_b_gk_bwd, l1_gnorm_g, l1_w_out, l1_norm2_g, l1_router_w, l1_router_b, l1_w_gu, l1_b_gu, l1_w_down, l1_b_down):
    row = lambda g: g.reshape(1, -1)
    xs = jnp.concatenate([ctx, x], axis=1).reshape(T_ALL, D)
    cc = jnp.zeros((16, D), F32).at[:B].set(c).at[B].set(c_ctx)
    mod0 = adaln(cc, l0_mod_w, l0_mod_b)
    mod1 = adaln(cc, l1_mod_w, l1_mod_b)

    w_in = l0_w_in
    wqa = w_in[:, :Q_LORA].astype(BF16)
    wkva = w_in[:, Q_LORA:Q_LORA + KV_LORA].astype(BF16)
    wkr = w_in[:, Q_LORA + KV_LORA:Q_LORA + KV_LORA + ROPE]
    wkr2 = jnp.concatenate([wkr, _rot_cols(wkr)], axis=-1).astype(BF16)
    wf = w_in[:, Q_LORA + KV_LORA + ROPE:].astype(BF16)
    wq3 = l0_w_q_up.reshape(Q_LORA, HEADS, QK)
    wq_nope = wq3[:, :, :NOPE].reshape(Q_LORA, HEADS * NOPE)
    wq_rope = wq3[:, :, NOPE:]
    wqup = jnp.concatenate([wq_nope, wq_rope.reshape(Q_LORA, HEADS * ROPE),
                            _rot_cols(wq_rope).reshape(Q_LORA, HEADS * ROPE)], axis=-1).astype(BF16)
    wkv3 = l0_w_kv_up.reshape(KV_LORA, HEADS, NOPE + VDIM)
    wkvup = jnp.concatenate([wkv3[:, :, :NOPE].reshape(KV_LORA, HEADS * NOPE),
                             wkv3[:, :, NOPE:].reshape(KV_LORA, HEADS * VDIM)], axis=-1).astype(BF16)
    cos6, sin6 = _rope_tables()
    q, k, v, uf = even_in(xs, mod0, row(l0_norm1_g), wqa, wkva, wkr2, wf, row(l0_q_norm_g), wqup,
                          row(l0_kv_norm_g), wkvup, cos6, sin6)
    att = attention(q, k, v).reshape(T_ALL, HEADS * VDIM)
    fmix = fourier(uf.reshape(B, ROWS_B, FW), *_dft_tables()).reshape(T_ALL, FW)
    rw0, rb0 = _router_pad(l0_router_w, l0_router_b)
    x1, h2, key, gate, cum = out0(att, fmix, xs, mod0, l0_w_out[:HEADS * VDIM].astype(BF16),
                                  l0_w_out[HEADS * VDIM:].astype(BF16), row(l0_norm2_g), rw0, rb0)
    y0 = moe(h2, key, gate, cum, *_expert_weights(l0_w_gu, l0_b_gu, l0_w_down, l0_b_down))

    w1 = l1_w_in
    o = 0
    wq1 = w1[:, o:o + GKW].astype(BF16); o += GKW
    wk1 = w1[:, o:o + GKW].astype(BF16); o += GKW
    wv1 = w1[:, o:o + GVW].astype(BF16); o += GVW
    wgo = w1[:, o:o + GVW].astype(BF16); o += GVW
    wgd = jnp.zeros((D, LANES), F32).at[:, :2 * GRANK].set(w1[:, o:]).astype(BF16)
    wgkf = jnp.zeros((LANES, GKW), F32).at[:GRANK].set(l1_w_gk_fwd)
    wgkb = jnp.zeros((LANES, GKW), F32).at[GRANK:2 * GRANK].set(l1_w_gk_bwd)
    x2, q1, k1, v1, go, gf, gb = odd_in(x1, y0, mod0, mod1, row(l1_norm1_g), wq1, wk1, wv1, wgo, wgd,
                                        wgkf, row(l1_b_gk_fwd), wgkb, row(l1_b_gk_bwd))
    o_f, o_b = gla(q1, k1, v1, gf, gb)
    rw1, rb1 = _router_pad(l1_router_w, l1_router_b)
    x3, h2b, key1, gate1, cum1 = out1(o_f, o_b, go, x2, mod1, row(l1_gnorm_g), l1_w_out.astype(BF16),
                                      row(l1_norm2_g), rw1, rb1)
    y1 = moe(h2b, key1, gate1, cum1, *_expert_weights(l1_w_gu, l1_b_gu, l1_w_down, l1_b_down))
    out = final(x3, y1, mod1, row(final_norm_g))
    return out.reshape(B, SEQ, D)
```

```python
import functools

import jax
import jax.numpy as jnp
import numpy as np
from jax import lax
from jax.experimental import pallas as pl
from jax.experimental.pallas import tpu as pltpu

D = 1024
B = 8
SEQ = 4096
CTX = 256
GRID_W = 64
EPS = 1e-6
HEADS = 6
Q_LORA = 384
KV_LORA = 256
NOPE = 128
ROPE = 64
VDIM = 128
QK = NOPE + ROPE
ROPE_BASE = 10000.0
FG = 4
FGD = 64
FW = FG * FGD
GH = 4
GDK = 128
GDV = 256
GRANK = 16
GNORM = 16.0
GCHUNK = 64
GKW = GH * GDK
GVW = GH * GDV
NE = 32
TOPK = 4
DFF = 1024
ALPHA = 1.702
LIMIT = 7.0

LANES = 128
TILE = 256
ROWS_B = CTX + SEQ
TPB = ROWS_B // TILE
T_ALL = B * ROWS_B
NT_ALL = T_ALL // TILE
T_LAT = B * SEQ
NT_LAT = T_LAT // TILE
LPB = SEQ // TILE
CH = 16
MAXCH = (TILE * TOPK + NE * (CH - 1)) // CH
NCHT = 96
SROWS = NCHT * CH
FBLK = 256
CPB = FBLK // CH
NCH_B = ROWS_B // GCHUNK
NCH_CTX = CTX // GCHUNK
VMEM_BIG = 56 * 1024 * 1024
VMEM_MID = 40 * 1024 * 1024

F32 = jnp.float32
BF16 = jnp.bfloat16
HI = lax.Precision.HIGHEST
NT_DIMS = (((1,), (1,)), ((), ()))
TN_DIMS = (((0,), (0,)), ((), ()))


def _cp(sem, vmem=VMEM_MID):
    return pltpu.CompilerParams(dimension_semantics=sem, vmem_limit_bytes=vmem)


def _rms(x, g):
    return x * lax.rsqrt(jnp.mean(x * x, axis=-1, keepdims=True) + EPS) * g


def _modulate(x, g, sh, sc):
    return _rms(x, g) * (1.0 + sc) + sh


def _silu(x):
    return x * jax.nn.sigmoid(x)


def _mod_index(i):
    return jnp.where(i % TPB == 0, B, i // TPB)


def _lat_tile(n):
    return n + n // LPB + 1


def _adaln_body(c_ref, w_ref, b_ref, o_ref):
    s = _silu(c_ref[...])
    o_ref[...] = jnp.dot(s, w_ref[...], precision=HI, preferred_element_type=F32) + b_ref[...]


def adaln(cc, mod_w, mod_b):
    tn = 1536
    out = pl.pallas_call(
        _adaln_body,
        grid=(6 * D // tn,),
        in_specs=[pl.BlockSpec((16, D), lambda j: (0, 0)),
                  pl.BlockSpec((D, tn), lambda j: (0, j)),
                  pl.BlockSpec((1, tn), lambda j: (0, j))],
        out_specs=pl.BlockSpec((16, tn), lambda j: (0, j)),
        out_shape=jax.ShapeDtypeStruct((16, 6 * D), F32),
        compiler_params=_cp(("arbitrary",)),
        name="adaln",
    )(cc, mod_w, mod_b.reshape(1, 6 * D))
    return out.reshape(16, 6, D)


def _even_in_body(x_ref, mod_ref, n1_ref, wqa_ref, wkva_ref, wkr_ref, wf_ref, qg_ref, wqup_ref,
                  kvg_ref, wkvup_ref, cos_ref, sin_ref, q_ref, k_ref, v_ref, uf_ref):
    m = mod_ref[0]
    h = _modulate(x_ref[...], n1_ref[...], m[0:1], m[1:2]).astype(BF16)
    cq = jnp.dot(h, wqa_ref[...], preferred_element_type=F32)
    ckv = jnp.dot(h, wkva_ref[...], preferred_element_type=F32)
    kr2 = jnp.dot(h, wkr_ref[...], preferred_element_type=F32)
    uf_ref[...] = jnp.dot(h, wf_ref[...], preferred_element_type=F32).astype(BF16)
    cos6 = cos_ref[...]
    sin6 = sin_ref[...]
    scale = QK ** -0.5
    qall = jnp.dot(_rms(cq, qg_ref[...]).astype(BF16), wqup_ref[...], preferred_element_type=F32)
    qn = qall[:, :HEADS * NOPE] * scale
    qr = (qall[:, HEADS * NOPE:HEADS * QK] * cos6 + qall[:, HEADS * QK:] * sin6) * scale
    kvall = jnp.dot(_rms(ckv, kvg_ref[...]).astype(BF16), wkvup_ref[...], preferred_element_type=F32)
    kr = (kr2[:, :ROPE] * cos6[:, :ROPE] + kr2[:, ROPE:] * sin6[:, :ROPE]).astype(BF16)
    for hd in range(HEADS):
        q_ref[0, hd, :, 0:NOPE] = qn[:, hd * NOPE:(hd + 1) * NOPE].astype(BF16)
        q_ref[0, hd, :, NOPE:QK] = qr[:, hd * ROPE:(hd + 1) * ROPE].astype(BF16)
        k_ref[0, hd, :, 0:NOPE] = kvall[:, hd * NOPE:(hd + 1) * NOPE].astype(BF16)
        k_ref[0, hd, :, NOPE:QK] = kr
        v_ref[0, hd] = kvall[:, HEADS * NOPE + hd * VDIM:HEADS * NOPE + (hd + 1) * VDIM].astype(BF16)


def even_in(x, mod, n1g, wqa, wkva, wkr2, wf, qg, wqup, kvg, wkvup, cos6, sin6):
    full = lambda a: pl.BlockSpec(a.shape, lambda i: (0,) * a.ndim)
    hs = lambda w: pl.BlockSpec((1, HEADS, TILE, w), lambda i: (i // TPB, 0, i % TPB, 0))
    return pl.pallas_call(
        _even_in_body,
        grid=(NT_ALL,),
        in_specs=[pl.BlockSpec((TILE, D), lambda i: (i, 0)),
                  pl.BlockSpec((1, 6, D), lambda i: (_mod_index(i), 0, 0)),
                  full(n1g), full(wqa), full(wkva), full(wkr2), full(wf), full(qg), full(wqup),
                  full(kvg), full(wkvup),
                  pl.BlockSpec((TILE, HEADS * ROPE), lambda i: (i % TPB, 0)),
                  pl.BlockSpec((TILE, HEADS * ROPE), lambda i: (i % TPB, 0))],
        out_specs=[hs(QK), hs(QK), hs(VDIM), pl.BlockSpec((TILE, FW), lambda i: (i, 0))],
        out_shape=[jax.ShapeDtypeStruct((B, HEADS, ROWS_B, QK), BF16),
                   jax.ShapeDtypeStruct((B, HEADS, ROWS_B, QK), BF16),
                   jax.ShapeDtypeStruct((B, HEADS, ROWS_B, VDIM), BF16),
                   jax.ShapeDtypeStruct((T_ALL, FW), BF16)],
        compiler_params=_cp(("arbitrary",)),
        name="even_in",
    )(x, mod, n1g, wqa, wkva, wkr2, wf, qg, wqup, kvg, wkvup, cos6, sin6)


def _softmax_pv(s, v):
    m = jnp.max(s, axis=-1, keepdims=True)
    p = jnp.exp(s - m)
    l = jnp.sum(p, axis=-1, keepdims=True)
    return jnp.dot(p.astype(BF16), v, preferred_element_type=F32) / l


def _attn_body(q_ref, k_ref, v_ref, o_ref):
    qt = pl.program_id(2)
    q = q_ref[0, 0]

    @pl.when(qt == 0)
    def _():
        s = lax.dot_general(q, k_ref[0, 0, 0:CTX, :], NT_DIMS, preferred_element_type=F32)
        o_ref[0] = _softmax_pv(s, v_ref[0, 0, 0:CTX, :]).astype(BF16)

    @pl.when(qt > 0)
    def _():
        s = lax.dot_general(q, k_ref[0, 0], NT_DIMS, preferred_element_type=F32)
        o_ref[0] = _softmax_pv(s, v_ref[0, 0]).astype(BF16)


def attention(q, k, v):
    return pl.pallas_call(
        _attn_body,
        grid=(B, HEADS, TPB),
        in_specs=[pl.BlockSpec((1, 1, TILE, QK), lambda b, h, t: (b, h, t, 0)),
                  pl.BlockSpec((1, 1, ROWS_B, QK), lambda b, h, t: (b, h, 0, 0)),
                  pl.BlockSpec((1, 1, ROWS_B, VDIM), lambda b, h, t: (b, h, 0, 0))],
        out_specs=pl.BlockSpec((1, TILE, VDIM), lambda b, h, t: (b, t, h)),
        out_shape=jax.ShapeDtypeStruct((B, ROWS_B, HEADS * VDIM), BF16),
        compiler_params=_cp(("arbitrary", "arbitrary", "arbitrary"), VMEM_BIG),
        name="attention",
    )(q, k, v)


def _fourier_body(u_ref, cn_ref, sn_ref, c2_ref, s2_ref, cc_ref, sc_ref, o_ref):
    i = pl.program_id(0)

    def finish(a, b):
        o_ref[0] = (jnp.dot(a.astype(BF16), cc_ref[...], preferred_element_type=F32)
                    - jnp.dot(b.astype(BF16), sc_ref[...], preferred_element_type=F32)).astype(BF16)

    @pl.when(i == 0)
    def _():
        u = u_ref[0, 0:CTX, :]
        finish(jnp.dot(c2_ref[...], u, preferred_element_type=F32),
               jnp.dot(s2_ref[...], u, preferred_element_type=F32))

    @pl.when(i > 0)
    def _():
        u = u_ref[0, CTX:ROWS_B, :]
        finish(jnp.dot(cn_ref[...], u, preferred_element_type=F32),
               jnp.dot(sn_ref[...], u, preferred_element_type=F32))


def fourier(uf, cn, sn, c2, s2, ccb, scb):
    full = lambda a: pl.BlockSpec(a.shape, lambda i, b: (0,) * a.ndim)
    return pl.pallas_call(
        _fourier_body,
        grid=(TPB, B),
        in_specs=[pl.BlockSpec((1, ROWS_B, FW), lambda i, b: (b, 0, 0)),
                  pl.BlockSpec((TILE, SEQ), lambda i, b: (jnp.maximum(i - 1, 0), 0)),
                  pl.BlockSpec((TILE, SEQ), lambda i, b: (jnp.maximum(i - 1, 0), 0)),
                  full(c2), full(s2), full(ccb), full(scb)],
        out_specs=pl.BlockSpec((1, TILE, FW), lambda i, b: (b, i, 0)),
        out_shape=jax.ShapeDtypeStruct((B, ROWS_B, FW), BF16),
        compiler_params=_cp(("arbitrary", "arbitrary")),
        name="fourier",
    )(uf, cn, sn, c2, s2, ccb, scb)


def _route(h2, rw_ref, rb_ref, lrow_ref, gate_ref, cnt_ref, xs_ref):
    logits = jnp.dot(h2, rw_ref[...], precision=HI, preferred_element_type=F32) + rb_ref[...]
    lane = lax.broadcasted_iota(jnp.int32, logits.shape, 1)
    l = logits
    vals, idxs = [], []
    for _ in range(TOPK):
        mx = jnp.max(l, axis=-1, keepdims=True)
        am = jnp.min(jnp.where(l == mx, lane, LANES), axis=-1, keepdims=True)
        vals.append(mx)
        idxs.append(am)
        l = jnp.where(lane == am, -jnp.inf, l)
    ex = [jnp.exp(vv - vals[0]) for vv in vals]
    den = ex[0] + ex[1] + ex[2] + ex[3]
    onehot = jnp.zeros(logits.shape, F32)
    for am in idxs:
        onehot = onehot + (lane == am).astype(F32)
    r = lax.broadcasted_iota(jnp.int32, (TILE, TILE), 0)
    c = lax.broadcasted_iota(jnp.int32, (TILE, TILE), 1)
    strict = (c < r).astype(BF16)
    rank = jnp.dot(strict, onehot.astype(BF16), preferred_element_type=F32)
    cnt = jnp.sum(onehot, axis=0, keepdims=True)
    nch = jnp.ceil(cnt * (1.0 / CH))
    er = lax.broadcasted_iota(jnp.int32, (LANES, LANES), 0)
    ec = lax.broadcasted_iota(jnp.int32, (LANES, LANES), 1)
    before = (er < ec).astype(BF16)
    seg = jnp.dot(jnp.broadcast_to(nch, (8, LANES)).astype(BF16), before,
                  preferred_element_type=F32)[0:1, :] * CH
    dest = rank + seg
    lrow = jnp.zeros(logits.shape, F32)
    gate = jnp.zeros(logits.shape, F32)
    for kk in range(TOPK):
        pos = jnp.sum(jnp.where(lane == idxs[kk], dest, 0.0), axis=-1, keepdims=True)
        lrow = jnp.where(lane == kk, pos, lrow)
        gate = jnp.where(lane == kk, ex[kk] / den, gate)
    lrow_ref[...] = lrow.astype(jnp.int32)
    gate_ref[...] = gate
    cnt_ref[0] = jnp.broadcast_to(cnt, (8, LANES))
    lrow_t = lrow.T.astype(jnp.int32)
    srow = lax.broadcasted_iota(jnp.int32, (SROWS, TILE), 0)
    hit = lrow_t[0:1, :] == srow
    for kk in range(1, TOPK):
        hit = jnp.logical_or(hit, lrow_t[kk:kk + 1, :] == srow)
    xs_ref[...] = jnp.dot(hit.astype(BF16), h2.astype(BF16), preferred_element_type=F32).astype(BF16)


def _route_out_specs(n_tiles):
    specs = [pl.BlockSpec((TILE, LANES), lambda i: (i, 0)),
             pl.BlockSpec((TILE, LANES), lambda i: (i, 0)),
             pl.BlockSpec((1, 8, LANES), lambda i: (i, 0, 0)),
             pl.BlockSpec((SROWS, D), lambda i: (i, 0))]
    shapes = [jax.ShapeDtypeStruct((n_tiles * TILE, LANES), jnp.int32),
              jax.ShapeDtypeStruct((n_tiles * TILE, LANES), F32),
              jax.ShapeDtypeStruct((n_tiles, 8, LANES), F32),
              jax.ShapeDtypeStruct((n_tiles * SROWS, D), BF16)]
    return specs, shapes


def _out0_body(a_ref, f_ref, x_ref, mod_ref, woa_ref, wof_ref, n2_ref, rw_ref, rb_ref,
               x1_ref, lrow_ref, gate_ref, cnt_ref, xs_ref):
    m = mod_ref[0]
    y = (jnp.dot(a_ref[...], woa_ref[...], preferred_element_type=F32)
         + jnp.dot(f_ref[...], wof_ref[...], preferred_element_type=F32))
    x1 = x_ref[...] + m[2:3] * y
    x1_ref[...] = x1
    h2 = _modulate(x1, n2_ref[...], m[3:4], m[4:5])
    _route(h2, rw_ref, rb_ref, lrow_ref, gate_ref, cnt_ref, xs_ref)


def out0(a, f, x, mod, woa, wof, n2g, rw, rb):
    full = lambda w: pl.BlockSpec(w.shape, lambda i: (0,) * w.ndim)
    rspecs, rshapes = _route_out_specs(NT_ALL)
    return pl.pallas_call(
        _out0_body,
        grid=(NT_ALL,),
        in_specs=[pl.BlockSpec((TILE, HEADS * VDIM), lambda i: (i, 0)),
                  pl.BlockSpec((TILE, FW), lambda i: (i, 0)),
                  pl.BlockSpec((TILE, D), lambda i: (i, 0)),
                  pl.BlockSpec((1, 6, D), lambda i: (_mod_index(i), 0, 0)),
                  full(woa), full(wof), full(n2g), full(rw), full(rb)],
        out_specs=[pl.BlockSpec((TILE, D), lambda i: (i, 0))] + rspecs,
        out_shape=[jax.ShapeDtypeStruct((T_ALL, D), F32)] + rshapes,
        compiler_params=_cp(("arbitrary",)),
        name="out0",
    )(a, f, x, mod, woa, wof, n2g, rw, rb)


def _chunk_lists(cnt, n_tiles):
    n_blk = (n_tiles * MAXCH + CPB - 1) // CPB + NE
    cnt = cnt[:, 0, :NE].astype(jnp.int32)
    nch = (cnt + CH - 1) // CH
    lend = jnp.cumsum(nch, axis=1)
    loff = lend - nch
    tend = jnp.cumsum(nch, axis=0)
    toff = tend - nch
    tot = tend[-1]
    blk = (tot + CPB - 1) // CPB
    bend = jnp.cumsum(blk)
    estart = (bend - blk) * CPB
    nvalid = bend[-1]
    bi = jnp.arange(n_blk, dtype=jnp.int32)
    be = jnp.minimum(jnp.searchsorted(bend, bi, side="right"), NE - 1).astype(jnp.int32)
    e_last = jnp.minimum(jnp.searchsorted(bend, nvalid - 1, side="right"), NE - 1).astype(jnp.int32)
    bvalid = bi < nvalid
    be = jnp.where(bvalid, be, e_last)
    g = jnp.arange(n_blk * CPB, dtype=jnp.int32)
    ge = jnp.repeat(be, CPB)
    o = g - estart[ge]
    gvalid = jnp.logical_and(jnp.repeat(bvalid, CPB), o < tot[ge])
    tend_t = tend.T
    tile = jnp.minimum(jnp.sum(tend_t[ge] <= o[:, None], axis=1), n_tiles - 1).astype(jnp.int32)
    src = tile * NCHT + loff[tile, ge] + (o - toff[tile, ge])
    src = jnp.where(gvalid, src, 0).astype(jnp.int32)
    cslot = jnp.arange(NCHT, dtype=jnp.int32)
    ce = jnp.minimum(jnp.sum(lend[:, None, :] <= cslot[None, :, None], axis=2), NE - 1).astype(jnp.int32)
    ti = jnp.arange(n_tiles, dtype=jnp.int32)[:, None]
    dst = estart[ce] + toff[ti, ce] + (cslot[None, :] - loff[ti, ce])
    dst = jnp.where(cslot[None, :] < lend[:, -1:], dst, 0).astype(jnp.int32)
    return src, be, bvalid.astype(jnp.int32), dst.reshape(-1), n_blk


def _chunk_copy(src_hbm, chunk, buf, slot, j, sem):
    start = chunk * CH if isinstance(chunk, int) else pl.multiple_of(chunk * CH, CH)
    return pltpu.make_async_copy(src_hbm.at[pl.ds(start, CH), :],
                                 buf.at[slot, pl.ds(j * CH, CH), :], sem.at[slot])


def _gather_chunks(list_ref, step, n_steps, per_step, src_hbm, buf, sem):
    def issue(st, slot):
        for j in range(per_step):
            _chunk_copy(src_hbm, list_ref[st * per_step + j], buf, slot, j, sem).start()

    @pl.when(step == 0)
    def _():
        issue(0, 0)

    @pl.when(step + 1 < n_steps)
    def _():
        issue(step + 1, (step + 1) % 2)

    slot = step % 2
    for j in range(per_step):
        _chunk_copy(src_hbm, 0, buf, slot, j, sem).wait()
    return slot


def _ffn_body(src_ref, be_ref, bv_ref, xs_hbm, wg_ref, wl_ref, bg_ref, bl_ref, wd_ref, bd_ref,
              yb_ref, xbuf, sem):
    i = pl.program_id(0)
    slot = _gather_chunks(src_ref, i, pl.num_programs(0), CPB, xs_hbm, xbuf, sem)

    @pl.when(bv_ref[i] == 1)
    def _():
        xb = xbuf[slot]
        glu = jnp.dot(xb, wg_ref[0], preferred_element_type=F32) + bg_ref[0]
        lin = jnp.dot(xb, wl_ref[0], preferred_element_type=F32) + bl_ref[0]
        glu = jnp.minimum(glu, LIMIT)
        lin = jnp.clip(lin, -LIMIT, LIMIT)
        act = glu * jax.nn.sigmoid(ALPHA * glu) * (lin + 1.0)
        out = jnp.dot(act.astype(BF16), wd_ref[0], preferred_element_type=F32) + bd_ref[0]
        yb_ref[...] = out.astype(BF16)

    @pl.when(bv_ref[i] == 0)
    def _():
        yb_ref[...] = jnp.zeros_like(yb_ref)


def expert_ffn(xs, src, be, bv, n_blk, wg, wl, bg, bl, wd, bd):
    wspec = lambda shp: pl.BlockSpec((1,) + shp, lambda i, src, be, bv: (be[i], 0, 0))
    grid_spec = pltpu.PrefetchScalarGridSpec(
        num_scalar_prefetch=3,
        grid=(n_blk,),
        in_specs=[pl.BlockSpec(memory_space=pl.ANY),
                  wspec((D, DFF)), wspec((D, DFF)), wspec((1, DFF)), wspec((1, DFF)),
                  wspec((DFF, D)), wspec((1, D))],
        out_specs=pl.BlockSpec((FBLK, D), lambda i, *_: (i, 0)),
        scratch_shapes=[pltpu.VMEM((2, FBLK, D), BF16), pltpu.SemaphoreType.DMA((2,))],
    )
    return pl.pallas_call(
        _ffn_body,
        grid_spec=grid_spec,
        out_shape=jax.ShapeDtypeStruct((n_blk * FBLK, D), BF16),
        compiler_params=_cp(("arbitrary",)),
        name="expert_ffn",
    )(src, be, bv, xs, wg, wl, bg, bl, wd, bd)


def _combine(dst_ref, step, n_steps, yb_hbm, ybuf, sem, lrow_ref, gate_ref):
    slot = _gather_chunks(dst_ref, step, n_steps, NCHT, yb_hbm, ybuf, sem)
    lrow = lrow_ref[...]
    gate = gate_ref[...]
    scol = lax.broadcasted_iota(jnp.int32, (TILE, SROWS), 1)
    w = jnp.zeros((TILE, SROWS), F32)
    for kk in range(TOPK):
        w = w + jnp.where(lrow[:, kk:kk + 1] == scol, gate[:, kk:kk + 1], 0.0)
    return jnp.dot(w.astype(BF16), ybuf[slot], preferred_element_type=F32)


def _odd_in_body(dst_ref, x_ref, yb_hbm, lrow_ref, gate_ref, mod0_ref, mod1_ref, n1_ref, wq_ref, wk_ref, wv_ref,
                 wgo_ref, wgd_ref, wgkf_ref, bgkf_ref, wgkb_ref, bgkb_ref,
                 x2_ref, q_ref, k_ref, v_ref, go_ref, gf_ref, gb_ref, ybuf, sem):
    m0 = mod0_ref[0]
    m1 = mod1_ref[0]
    y = _combine(dst_ref, pl.program_id(0), pl.num_programs(0), yb_hbm, ybuf, sem, lrow_ref, gate_ref)
    x2 = x_ref[...] + m0[5:6] * y
    x2_ref[...] = x2
    h = _modulate(x2, n1_ref[...], m1[0:1], m1[1:2]).astype(BF16)
    q_ref[...] = jnp.dot(h, wq_ref[...], preferred_element_type=F32) * (GDK ** -0.5)
    k_ref[...] = jnp.dot(h, wk_ref[...], preferred_element_type=F32)
    v_ref[...] = jnp.dot(h, wv_ref[...], preferred_element_type=F32).astype(BF16)
    go_ref[...] = jnp.dot(h, wgo_ref[...], preferred_element_type=F32).astype(BF16)
    gd = jnp.dot(h, wgd_ref[...], preferred_element_type=F32)

    def decay(w_ref, b_ref):
        z = jnp.dot(gd, w_ref[...], precision=HI, preferred_element_type=F32) + b_ref[...]
        return (jnp.minimum(z, 0.0) - jnp.log(1.0 + jnp.exp(-jnp.abs(z)))) / GNORM

    gf_ref[...] = decay(wgkf_ref, bgkf_ref)
    gb_ref[...] = decay(wgkb_ref, bgkb_ref)


def odd_in(dst, x1, yb, lrow, gate, mod0, mod1, n1g, wq, wk, wv, wgo, wgd, wgkf, bgkf, wgkb, bgkb):
    full = lambda w: pl.BlockSpec(w.shape, lambda i, d: (0,) * w.ndim)
    row = lambda w: pl.BlockSpec((TILE, w), lambda i, d: (i, 0))
    modspec = pl.BlockSpec((1, 6, D), lambda i, d: (_mod_index(i), 0, 0))
    grid_spec = pltpu.PrefetchScalarGridSpec(
        num_scalar_prefetch=1,
        grid=(NT_ALL,),
        in_specs=[row(D), pl.BlockSpec(memory_space=pl.ANY), row(LANES), row(LANES), modspec, modspec,
                  full(n1g), full(wq), full(wk), full(wv), full(wgo),
                  full(wgd), full(wgkf), full(bgkf), full(wgkb), full(bgkb)],
        out_specs=[row(D), row(GKW), row(GKW), row(GVW), row(GVW), row(GKW), row(GKW)],
        scratch_shapes=[pltpu.VMEM((2, SROWS, D), BF16), pltpu.SemaphoreType.DMA((2,))],
    )
    return pl.pallas_call(
        _odd_in_body,
        grid_spec=grid_spec,
        out_shape=[jax.ShapeDtypeStruct((T_ALL, D), F32),
                   jax.ShapeDtypeStruct((T_ALL, GKW), F32),
                   jax.ShapeDtypeStruct((T_ALL, GKW), F32),
                   jax.ShapeDtypeStruct((T_ALL, GVW), BF16),
                   jax.ShapeDtypeStruct((T_ALL, GVW), BF16),
                   jax.ShapeDtypeStruct((T_ALL, GKW), F32),
                   jax.ShapeDtypeStruct((T_ALL, GKW), F32)],
        compiler_params=_cp(("arbitrary",), VMEM_BIG),
        name="odd_in",
    )(dst, x1, yb, lrow, gate, mod0, mod1, n1g, wq, wk, wv, wgo, wgd, wgkf, bgkf, wgkb, bgkb)


def _gla_dir(q, k, v, g, s_ref, tri, reverse):
    outs = []
    for hd in range(GH):
        ks = slice(hd * GDK, (hd + 1) * GDK)
        vs = slice(hd * GDV, (hd + 1) * GDV)
        cum = jnp.dot(tri, g[:, ks], precision=HI, preferred_element_type=F32)
        tot = cum[0:1, :] if reverse else cum[GCHUNK - 1:GCHUNK, :]
        qd = (q[:, ks] * jnp.exp(cum)).astype(BF16)
        kd = (k[:, ks] * jnp.exp(-cum)).astype(BF16)
        ke = (k[:, ks] * jnp.exp(tot - cum)).astype(BF16)
        sc = lax.dot_general(qd, kd, NT_DIMS, preferred_element_type=F32)
        sc = jnp.where(tri > 0, sc, 0.0).astype(BF16)
        st = s_ref[hd]
        vv = v[:, vs]
        outs.append(lax.dot_general(qd, st.astype(BF16), NT_DIMS, preferred_element_type=F32)
                    + jnp.dot(sc, vv, preferred_element_type=F32))
        s_ref[hd] = st * jnp.exp(tot) + lax.dot_general(vv, ke, TN_DIMS, preferred_element_type=F32)
    return jnp.concatenate(outs, axis=-1)


def _gla_body(qf_ref, kf_ref, vf_ref, gf_ref, qb_ref, kb_ref, vb_ref, gb_ref, of_ref, ob_ref, sf_ref, sb_ref):
    @pl.when(pl.program_id(1) == 0)
    def _():
        sf_ref[...] = jnp.zeros_like(sf_ref)
        sb_ref[...] = jnp.zeros_like(sb_ref)

    r = lax.broadcasted_iota(jnp.int32, (GCHUNK, GCHUNK), 0)
    c = lax.broadcasted_iota(jnp.int32, (GCHUNK, GCHUNK), 1)
    lower = (c <= r).astype(F32)
    upper = (c >= r).astype(F32)
    of_ref[...] = _gla_dir(qf_ref[...], kf_ref[...], vf_ref[...], gf_ref[...], sf_ref, lower, False)
    ob_ref[...] = _gla_dir(qb_ref[...], kb_ref[...], vb_ref[...], gb_ref[...], sb_ref, upper, True)


def _bwd_chunk(c):
    return jnp.where(c < NCH_CTX, NCH_CTX - 1 - c, NCH_B + NCH_CTX - 1 - c)


def gla(q, k, v, gf, gb):
    fw = lambda w: pl.BlockSpec((GCHUNK, w), lambda b, c: (b * NCH_B + c, 0))
    bw = lambda w: pl.BlockSpec((GCHUNK, w), lambda b, c: (b * NCH_B + _bwd_chunk(c), 0))
    return pl.pallas_call(
        _gla_body,
        grid=(B, NCH_B),
        in_specs=[fw(GKW), fw(GKW), fw(GVW), fw(GKW), bw(GKW), bw(GKW), bw(GVW), bw(GKW)],
        out_specs=[fw(GVW), bw(GVW)],
        out_shape=[jax.ShapeDtypeStruct((T_ALL, GVW), F32), jax.ShapeDtypeStruct((T_ALL, GVW), F32)],
        scratch_shapes=[pltpu.VMEM((GH, GDV, GDK), F32), pltpu.VMEM((GH, GDV, GDK), F32)],
        compiler_params=_cp(("arbitrary", "arbitrary")),
        name="gla",
    )(q, k, v, gf, q, k, v, gb)


def _out1_body(of_ref, ob_ref, go_ref, x_ref, mod_ref, gn_ref, wo_ref, n2_ref, rw_ref, rb_ref,
               x3_ref, lrow_ref, gate_ref, cnt_ref, xs_ref):
    m = mod_ref[0]
    o = of_ref[...] + ob_ref[...]
    gn = gn_ref[...]
    parts = [_rms(o[:, hd * GDV:(hd + 1) * GDV], gn) for hd in range(GH)]
    on = jnp.concatenate(parts, axis=-1) * _silu(go_ref[...].astype(F32))
    y = jnp.dot(on.astype(BF16), wo_ref[...], preferred_element_type=F32)
    x3 = x_ref[...] + m[2:3] * y
    x3_ref[...] = x3
    h2 = _modulate(x3, n2_ref[...], m[3:4], m[4:5])
    _route(h2, rw_ref, rb_ref, lrow_ref, gate_ref, cnt_ref, xs_ref)


def out1(of, ob, go, x2, mod, gng, wo, n2g, rw, rb):
    full = lambda w: pl.BlockSpec(w.shape, lambda n: (0,) * w.ndim)
    lat = lambda w: pl.BlockSpec((TILE, w), lambda n: (_lat_tile(n), 0))
    rspecs, rshapes = _route_out_specs(NT_LAT)
    return pl.pallas_call(
        _out1_body,
        grid=(NT_LAT,),
        in_specs=[lat(GVW), lat(GVW), lat(GVW), lat(D),
                  pl.BlockSpec((1, 6, D), lambda n: (n // LPB, 0, 0)),
                  full(gng), full(wo), full(n2g), full(rw), full(rb)],
        out_specs=[pl.BlockSpec((TILE, D), lambda n: (n, 0))] + rspecs,
        out_shape=[jax.ShapeDtypeStruct((T_LAT, D), F32)] + rshapes,
        compiler_params=_cp(("arbitrary",)),
        name="out1",
    )(of, ob, go, x2, mod, gng, wo, n2g, rw, rb)


def _final_body(dst_ref, x_ref, yb_hbm, lrow_ref, gate_ref, mod_ref, g_ref, o_ref, ybuf, sem):
    m = mod_ref[0]
    y = _combine(dst_ref, pl.program_id(0), pl.num_programs(0), yb_hbm, ybuf, sem, lrow_ref, gate_ref)
    o_ref[...] = _rms(x_ref[...] + m[5:6] * y, g_ref[...])


def final(dst, x3, yb, lrow, gate, mod, g):
    grid_spec = pltpu.PrefetchScalarGridSpec(
        num_scalar_prefetch=1,
        grid=(NT_LAT,),
        in_specs=[pl.BlockSpec((TILE, D), lambda n, d: (n, 0)),
                  pl.BlockSpec(memory_space=pl.ANY),
                  pl.BlockSpec((TILE, LANES), lambda n, d: (n, 0)),
                  pl.BlockSpec((TILE, LANES), lambda n, d: (n, 0)),
                  pl.BlockSpec((1, 6, D), lambda n, d: (n // LPB, 0, 0)),
                  pl.BlockSpec((1, D), lambda n, d: (0, 0))],
        out_specs=pl.BlockSpec((TILE, D), lambda n, d: (n, 0)),
        scratch_shapes=[pltpu.VMEM((2, SROWS, D), BF16), pltpu.SemaphoreType.DMA((2,))],
    )
    return pl.pallas_call(
        _final_body,
        grid_spec=grid_spec,
        out_shape=jax.ShapeDtypeStruct((T_LAT, D), F32),
        compiler_params=_cp(("arbitrary",)),
        name="final",
    )(dst, x3, yb, lrow, gate, mod, g)


def _rot_cols(w):
    a, b, c, d = jnp.split(w, 4, axis=-1)
    return jnp.concatenate([-b, a, -d, c], axis=-1)


def _rope_tables():
    n = np.arange(SEQ)
    row = (n // GRID_W).astype(np.float32)
    col = (n % GRID_W).astype(np.float32)
    axis_dim = ROPE // 2
    inv = (ROPE_BASE ** (-np.arange(0, axis_dim, 2, dtype=np.float32) / axis_dim)).astype(np.float32)
    ar = row[:, None] * inv
    ac = col[:, None] * inv
    ang = np.concatenate([ar, ar, ac, ac], axis=-1).astype(np.float32)
    cos = np.concatenate([np.ones((CTX, ROPE), np.float32), np.cos(ang)], axis=0)
    sin = np.concatenate([np.zeros((CTX, ROPE), np.float32), np.sin(ang)], axis=0)
    return jnp.asarray(np.tile(cos, (1, HEADS))), jnp.asarray(np.tile(sin, (1, HEADS)))


def _dft(n, scale):
    j = np.arange(n, dtype=np.int64)
    ang = 2.0 * np.pi * ((j[:, None] * j[None, :]) % n).astype(np.float64) / n
    return (np.cos(ang) * scale), (np.sin(ang) * scale)


def _dft_tables():
    cn, sn = _dft(SEQ, SEQ ** -0.5)
    c2, s2 = _dft(CTX, CTX ** -0.5)
    cg, sg = _dft(FGD, FGD ** -0.5)
    eye = np.eye(FG)
    to = lambda a: jnp.asarray(a.astype(np.float32)).astype(BF16)
    return to(cn), to(sn), to(c2), to(s2), to(np.kron(eye, cg)), to(np.kron(eye, sg))


def _router_pad(rw, rb):
    rwp = jnp.zeros((D, LANES), F32).at[:, :NE].set(rw)
    rbp = jnp.full((1, LANES), -1e30, F32).at[0, :NE].set(rb)
    return rwp, rbp


def _expert_weights(w_gu, b_gu, w_down, b_down):
    wgu = w_gu.reshape(NE, D, DFF, 2)
    bgu = b_gu.reshape(NE, 1, DFF, 2)
    return (wgu[..., 0].astype(BF16), wgu[..., 1].astype(BF16), bgu[..., 0], bgu[..., 1],
            w_down.astype(BF16), b_down.reshape(NE, 1, D))


def kernel(x, c, ctx, c_ctx, final_norm_g, l0_mod_w, l0_mod_b, l0_norm1_g, l0_w_in, l0_q_norm_g, l0_w_q_up, l0_kv_norm_g, l0_w_kv_up, l0_w_out, l0_norm2_g, l0_router_w, l0_router_b, l0_w_gu, l0_b_gu, l0_w_down, l0_b_down, l1_mod_w, l1_mod_b, l1_norm1_g, l1_w_in, l1_w_gk_fwd, l1_b_gk_fwd, l1_w_gk_bwd, l1_b_gk_bwd, l1_gnorm_g, l1_w_out, l1_norm2_g, l1_router_w, l1_router_b, l1_w_gu, l1_b_gu, l1_w_down, l1_b_down):
    row = lambda g: g.reshape(1, -1)
    xs = jnp.concatenate([ctx, x], axis=1).reshape(T_ALL, D)
    cc = jnp.zeros((16, D), F32).at[:B].set(c).at[B].set(c_ctx)
    mod0 = adaln(cc, l0_mod_w, l0_mod_b)
    mod1 = adaln(cc, l1_mod_w, l1_mod_b)

    w_in = l0_w_in
    wqa = w_in[:, :Q_LORA].astype(BF16)
    wkva = w_in[:, Q_LORA:Q_LORA + KV_LORA].astype(BF16)
    wkr = w_in[:, Q_LORA + KV_LORA:Q_LORA + KV_LORA + ROPE]
    wkr2 = jnp.concatenate([wkr, _rot_cols(wkr)], axis=-1).astype(BF16)
    wf = w_in[:, Q_LORA + KV_LORA + ROPE:].astype(BF16)
    wq3 = l0_w_q_up.reshape(Q_LORA, HEADS, QK)
    wq_nope = wq3[:, :, :NOPE].reshape(Q_LORA, HEADS * NOPE)
    wq_rope = wq3[:, :, NOPE:]
    wqup = jnp.concatenate([wq_nope, wq_rope.reshape(Q_LORA, HEADS * ROPE),
                            _rot_cols(wq_rope).reshape(Q_LORA, HEADS * ROPE)], axis=-1).astype(BF16)
    wkv3 = l0_w_kv_up.reshape(KV_LORA, HEADS, NOPE + VDIM)
    wkvup = jnp.concatenate([wkv3[:, :, :NOPE].reshape(KV_LORA, HEADS * NOPE),
                             wkv3[:, :, NOPE:].reshape(KV_LORA, HEADS * VDIM)], axis=-1).astype(BF16)
    cos6, sin6 = _rope_tables()
    q, k, v, uf = even_in(xs, mod0, row(l0_norm1_g), wqa, wkva, wkr2, wf, row(l0_q_norm_g), wqup,
                          row(l0_kv_norm_g), wkvup, cos6, sin6)
    att = attention(q, k, v).reshape(T_ALL, HEADS * VDIM)
    fmix = fourier(uf.reshape(B, ROWS_B, FW), *_dft_tables()).reshape(T_ALL, FW)
    rw0, rb0 = _router_pad(l0_router_w, l0_router_b)
    x1, lrow0, gate0, cnt0, xs0 = out0(att, fmix, xs, mod0, l0_w_out[:HEADS * VDIM].astype(BF16),
                                       l0_w_out[HEADS * VDIM:].astype(BF16), row(l0_norm2_g), rw0, rb0)
    src0, be0, bv0, dst0, nb0 = _chunk_lists(cnt0, NT_ALL)
    yb0 = expert_ffn(xs0, src0, be0, bv0, nb0, *_expert_weights(l0_w_gu, l0_b_gu, l0_w_down, l0_b_down))

    w1 = l1_w_in
    o = 0
    wq1 = w1[:, o:o + GKW].astype(BF16); o += GKW
    wk1 = w1[:, o:o + GKW].astype(BF16); o += GKW
    wv1 = w1[:, o:o + GVW].astype(BF16); o += GVW
    wgo = w1[:, o:o + GVW].astype(BF16); o += GVW
    wgd = jnp.zeros((D, LANES), F32).at[:, :2 * GRANK].set(w1[:, o:]).astype(BF16)
    wgkf = jnp.zeros((LANES, GKW), F32).at[:GRANK].set(l1_w_gk_fwd)
    wgkb = jnp.zeros((LANES, GKW), F32).at[GRANK:2 * GRANK].set(l1_w_gk_bwd)
    x2, q1, k1, v1, go, gf, gb = odd_in(dst0, x1, yb0, lrow0, gate0, mod0, mod1, row(l1_norm1_g), wq1, wk1, wv1,
                                        wgo, wgd, wgkf, row(l1_b_gk_fwd), wgkb, row(l1_b_gk_bwd))
    o_f, o_b = gla(q1, k1, v1, gf, gb)
    rw1, rb1 = _router_pad(l1_router_w, l1_router_b)
    x3, lrow1, gate1, cnt1, xs1 = out1(o_f, o_b, go, x2, mod1, row(l1_gnorm_g), l1_w_out.astype(BF16),
                                       row(l1_norm2_g), rw1, rb1)
    src1, be1, bv1, dst1, nb1 = _chunk_lists(cnt1, NT_LAT)
    yb1 = expert_ffn(xs1, src1, be1, bv1, nb1, *_expert_weights(l1_w_gu, l1_b_gu, l1_w_down, l1_b_down))
    out = final(dst1, x3, yb1, lrow1, gate1, mod1, row(final_norm_g))
    return out.reshape(B, SEQ, D)
```

```python
import functools

import jax
import jax.numpy as jnp
import numpy as np
from jax import lax
from jax.experimental import pallas as pl
from jax.experimental.pallas import tpu as pltpu

D = 1024
B = 8
SEQ = 4096
CTX = 256
GRID_W = 64
EPS = 1e-6
HEADS = 6
Q_LORA = 384
KV_LORA = 256
NOPE = 128
ROPE = 64
VDIM = 128
QK = NOPE + ROPE
ROPE_BASE = 10000.0
FG = 4
FGD = 64
FW = FG * FGD
GH = 4
GDK = 128
GDV = 256
GRANK = 16
GNORM = 16.0
GCHUNK = 64
GKW = GH * GDK
GVW = GH * GDV
NE = 32
TOPK = 4
DFF = 1024
ALPHA = 1.702
LIMIT = 7.0

LANES = 128
TILE = 256
ROWS_B = CTX + SEQ
TPB = ROWS_B // TILE
T_ALL = B * ROWS_B
NT_ALL = T_ALL // TILE
T_LAT = B * SEQ
NT_LAT = T_LAT // TILE
LPB = SEQ // TILE
CH = 16
MAXCH = (TILE * TOPK + NE * (CH - 1)) // CH
NCHT = 96
SROWS = NCHT * CH
FBLK = 256
CPB = FBLK // CH
NCH_B = ROWS_B // GCHUNK
NCH_CTX = CTX // GCHUNK
VMEM_BIG = 56 * 1024 * 1024
VMEM_MID = 40 * 1024 * 1024

F32 = jnp.float32
BF16 = jnp.bfloat16
HI = lax.Precision.HIGHEST
NT_DIMS = (((1,), (1,)), ((), ()))
TN_DIMS = (((0,), (0,)), ((), ()))


def _cp(sem, vmem=VMEM_MID):
    return pltpu.CompilerParams(dimension_semantics=sem, vmem_limit_bytes=vmem)


def _rms(x, g):
    return x * lax.rsqrt(jnp.mean(x * x, axis=-1, keepdims=True) + EPS) * g


def _modulate(x, g, sh, sc):
    return _rms(x, g) * (1.0 + sc) + sh


def _silu(x):
    return x * jax.nn.sigmoid(x)


def _mod_index(i):
    return jnp.where(i % TPB == 0, B, i // TPB)


def _lat_tile(n):
    return n + n // LPB + 1


def _adaln_body(c_ref, w_ref, b_ref, o_ref):
    s = _silu(c_ref[...])
    o_ref[...] = jnp.dot(s, w_ref[...], precision=HI, preferred_element_type=F32) + b_ref[...]


def adaln(cc, mod_w, mod_b):
    tn = 1536
    out = pl.pallas_call(
        _adaln_body,
        grid=(6 * D // tn,),
        in_specs=[pl.BlockSpec((16, D), lambda j: (0, 0)),
                  pl.BlockSpec((D, tn), lambda j: (0, j)),
                  pl.BlockSpec((1, tn), lambda j: (0, j))],
        out_specs=pl.BlockSpec((16, tn), lambda j: (0, j)),
        out_shape=jax.ShapeDtypeStruct((16, 6 * D), F32),
        compiler_params=_cp(("arbitrary",)),
        name="adaln",
    )(cc, mod_w, mod_b.reshape(1, 6 * D))
    return out.reshape(16, 6, D)


def _even_in_body(x_ref, mod_ref, n1_ref, wqa_ref, wkva_ref, wkr_ref, wf_ref, qg_ref, wqup_ref,
                  kvg_ref, wkvup_ref, cos_ref, sin_ref, q_ref, k_ref, v_ref, uf_ref):
    m = mod_ref[0]
    h = _modulate(x_ref[...], n1_ref[...], m[0:1], m[1:2]).astype(BF16)
    cq = jnp.dot(h, wqa_ref[...], preferred_element_type=F32)
    ckv = jnp.dot(h, wkva_ref[...], preferred_element_type=F32)
    kr2 = jnp.dot(h, wkr_ref[...], preferred_element_type=F32)
    uf_ref[...] = jnp.dot(h, wf_ref[...], preferred_element_type=F32).astype(BF16)
    cos6 = cos_ref[...]
    sin6 = sin_ref[...]
    scale = QK ** -0.5
    qall = jnp.dot(_rms(cq, qg_ref[...]).astype(BF16), wqup_ref[...], preferred_element_type=F32)
    qn = qall[:, :HEADS * NOPE] * scale
    qr = (qall[:, HEADS * NOPE:HEADS * QK] * cos6 + qall[:, HEADS * QK:] * sin6) * scale
    kvall = jnp.dot(_rms(ckv, kvg_ref[...]).astype(BF16), wkvup_ref[...], preferred_element_type=F32)
    kr = (kr2[:, :ROPE] * cos6[:, :ROPE] + kr2[:, ROPE:] * sin6[:, :ROPE]).astype(BF16)
    for hd in range(HEADS):
        q_ref[0, hd, :, 0:NOPE] = qn[:, hd * NOPE:(hd + 1) * NOPE].astype(BF16)
        q_ref[0, hd, :, NOPE:QK] = qr[:, hd * ROPE:(hd + 1) * ROPE].astype(BF16)
        k_ref[0, hd, :, 0:NOPE] = kvall[:, hd * NOPE:(hd + 1) * NOPE].astype(BF16)
        k_ref[0, hd, :, NOPE:QK] = kr
        v_ref[0, hd] = kvall[:, HEADS * NOPE + hd * VDIM:HEADS * NOPE + (hd + 1) * VDIM].astype(BF16)


def even_in(x, mod, n1g, wqa, wkva, wkr2, wf, qg, wqup, kvg, wkvup, cos6, sin6):
    full = lambda a: pl.BlockSpec(a.shape, lambda i: (0,) * a.ndim)
    hs = lambda w: pl.BlockSpec((1, HEADS, TILE, w), lambda i: (i // TPB, 0, i % TPB, 0))
    return pl.pallas_call(
        _even_in_body,
        grid=(NT_ALL,),
        in_specs=[pl.BlockSpec((TILE, D), lambda i: (i, 0)),
                  pl.BlockSpec((1, 6, D), lambda i: (_mod_index(i), 0, 0)),
                  full(n1g), full(wqa), full(wkva), full(wkr2), full(wf), full(qg), full(wqup),
                  full(kvg), full(wkvup),
                  pl.BlockSpec((TILE, HEADS * ROPE), lambda i: (i % TPB, 0)),
                  pl.BlockSpec((TILE, HEADS * ROPE), lambda i: (i % TPB, 0))],
        out_specs=[hs(QK), hs(QK), hs(VDIM), pl.BlockSpec((TILE, FW), lambda i: (i, 0))],
        out_shape=[jax.ShapeDtypeStruct((B, HEADS, ROWS_B, QK), BF16),
                   jax.ShapeDtypeStruct((B, HEADS, ROWS_B, QK), BF16),
                   jax.ShapeDtypeStruct((B, HEADS, ROWS_B, VDIM), BF16),
                   jax.ShapeDtypeStruct((T_ALL, FW), BF16)],
        compiler_params=_cp(("arbitrary",)),
        name="even_in",
    )(x, mod, n1g, wqa, wkva, wkr2, wf, qg, wqup, kvg, wkvup, cos6, sin6)


def _softmax_pv(s, v):
    m = jnp.max(s, axis=-1, keepdims=True)
    p = jnp.exp(s - m)
    l = jnp.sum(p, axis=-1, keepdims=True)
    return jnp.dot(p.astype(BF16), v, preferred_element_type=F32) / l


def _attn_body(q_ref, k_ref, v_ref, o_ref):
    qt = pl.program_id(2)
    q = q_ref[0, 0]

    @pl.when(qt == 0)
    def _():
        s = lax.dot_general(q, k_ref[0, 0, 0:CTX, :], NT_DIMS, preferred_element_type=F32)
        o_ref[0] = _softmax_pv(s, v_ref[0, 0, 0:CTX, :]).astype(BF16)

    @pl.when(qt > 0)
    def _():
        s = lax.dot_general(q, k_ref[0, 0], NT_DIMS, preferred_element_type=F32)
        o_ref[0] = _softmax_pv(s, v_ref[0, 0]).astype(BF16)


def attention(q, k, v):
    return pl.pallas_call(
        _attn_body,
        grid=(B, HEADS, TPB),
        in_specs=[pl.BlockSpec((1, 1, TILE, QK), lambda b, h, t: (b, h, t, 0)),
                  pl.BlockSpec((1, 1, ROWS_B, QK), lambda b, h, t: (b, h, 0, 0)),
                  pl.BlockSpec((1, 1, ROWS_B, VDIM), lambda b, h, t: (b, h, 0, 0))],
        out_specs=pl.BlockSpec((1, TILE, VDIM), lambda b, h, t: (b, t, h)),
        out_shape=jax.ShapeDtypeStruct((B, ROWS_B, HEADS * VDIM), BF16),
        compiler_params=_cp(("arbitrary", "arbitrary", "arbitrary"), VMEM_BIG),
        name="attention",
    )(q, k, v)


def _fourier_body(u_ref, cn_ref, sn_ref, c2_ref, s2_ref, cc_ref, sc_ref, o_ref):
    i = pl.program_id(0)

    def finish(a, b):
        o_ref[0] = (jnp.dot(a.astype(BF16), cc_ref[...], preferred_element_type=F32)
                    - jnp.dot(b.astype(BF16), sc_ref[...], preferred_element_type=F32)).astype(BF16)

    @pl.when(i == 0)
    def _():
        u = u_ref[0, 0:CTX, :]
        finish(jnp.dot(c2_ref[...], u, preferred_element_type=F32),
               jnp.dot(s2_ref[...], u, preferred_element_type=F32))

    @pl.when(i > 0)
    def _():
        u = u_ref[0, CTX:ROWS_B, :]
        finish(jnp.dot(cn_ref[...], u, preferred_element_type=F32),
               jnp.dot(sn_ref[...], u, preferred_element_type=F32))


def fourier(uf, cn, sn, c2, s2, ccb, scb):
    full = lambda a: pl.BlockSpec(a.shape, lambda i, b: (0,) * a.ndim)
    return pl.pallas_call(
        _fourier_body,
        grid=(TPB, B),
        in_specs=[pl.BlockSpec((1, ROWS_B, FW), lambda i, b: (b, 0, 0)),
                  pl.BlockSpec((TILE, SEQ), lambda i, b: (jnp.maximum(i - 1, 0), 0)),
                  pl.BlockSpec((TILE, SEQ), lambda i, b: (jnp.maximum(i - 1, 0), 0)),
                  full(c2), full(s2), full(ccb), full(scb)],
        out_specs=pl.BlockSpec((1, TILE, FW), lambda i, b: (b, i, 0)),
        out_shape=jax.ShapeDtypeStruct((B, ROWS_B, FW), BF16),
        compiler_params=_cp(("arbitrary", "arbitrary")),
        name="fourier",
    )(uf, cn, sn, c2, s2, ccb, scb)


def _route(h2, rw_ref, rb_ref, lrow_ref, gate_ref, cnt_ref, xs_ref):
    logits = jnp.dot(h2, rw_ref[...], precision=HI, preferred_element_type=F32) + rb_ref[...]
    lane = lax.broadcasted_iota(jnp.int32, logits.shape, 1)
    l = logits
    vals, idxs = [], []
    for _ in range(TOPK):
        mx = jnp.max(l, axis=-1, keepdims=True)
        am = jnp.min(jnp.where(l == mx, lane, LANES), axis=-1, keepdims=True)
        vals.append(mx)
        idxs.append(am)
        l = jnp.where(lane == am, -jnp.inf, l)
    ex = [jnp.exp(vv - vals[0]) for vv in vals]
    den = ex[0] + ex[1] + ex[2] + ex[3]
    onehot = jnp.zeros(logits.shape, F32)
    for am in idxs:
        onehot = onehot + (lane == am).astype(F32)
    r = lax.broadcasted_iota(jnp.int32, (TILE, TILE), 0)
    c = lax.broadcasted_iota(jnp.int32, (TILE, TILE), 1)
    strict = (c < r).astype(BF16)
    rank = jnp.dot(strict, onehot.astype(BF16), preferred_element_type=F32)
    cnt = jnp.sum(onehot, axis=0, keepdims=True)
    nch = jnp.ceil(cnt * (1.0 / CH))
    er = lax.broadcasted_iota(jnp.int32, (LANES, LANES), 0)
    ec = lax.broadcasted_iota(jnp.int32, (LANES, LANES), 1)
    before = (er < ec).astype(BF16)
    seg = jnp.dot(jnp.broadcast_to(nch, (8, LANES)).astype(BF16), before,
                  preferred_element_type=F32)[0:1, :] * CH
    dest = rank + seg
    lrow = jnp.zeros(logits.shape, F32)
    gate = jnp.zeros(logits.shape, F32)
    for kk in range(TOPK):
        pos = jnp.sum(jnp.where(lane == idxs[kk], dest, 0.0), axis=-1, keepdims=True)
        lrow = jnp.where(lane == kk, pos, lrow)
        gate = jnp.where(lane == kk, ex[kk] / den, gate)
    lrow_ref[...] = lrow.astype(jnp.int32)
    gate_ref[...] = gate
    cnt_ref[0] = jnp.broadcast_to(cnt, (8, LANES))
    lrow_t = lrow.T.astype(jnp.int32)
    srow = lax.broadcasted_iota(jnp.int32, (SROWS, TILE), 0)
    hit = lrow_t[0:1, :] == srow
    for kk in range(1, TOPK):
        hit = jnp.logical_or(hit, lrow_t[kk:kk + 1, :] == srow)
    xs_ref[...] = jnp.dot(hit.astype(BF16), h2.astype(BF16), preferred_element_type=F32).astype(BF16)


def _route_out_specs(n_tiles):
    specs = [pl.BlockSpec((TILE, LANES), lambda i: (i, 0)),
             pl.BlockSpec((TILE, LANES), lambda i: (i, 0)),
             pl.BlockSpec((1, 8, LANES), lambda i: (i, 0, 0)),
             pl.BlockSpec((SROWS, D), lambda i: (i, 0))]
    shapes = [jax.ShapeDtypeStruct((n_tiles * TILE, LANES), jnp.int32),
              jax.ShapeDtypeStruct((n_tiles * TILE, LANES), F32),
              jax.ShapeDtypeStruct((n_tiles, 8, LANES), F32),
              jax.ShapeDtypeStruct((n_tiles * SROWS, D), BF16)]
    return specs, shapes


def _out0_body(a_ref, f_ref, x_ref, mod_ref, woa_ref, wof_ref, n2_ref, rw_ref, rb_ref,
               x1_ref, lrow_ref, gate_ref, cnt_ref, xs_ref):
    m = mod_ref[0]
    y = (jnp.dot(a_ref[...], woa_ref[...], preferred_element_type=F32)
         + jnp.dot(f_ref[...], wof_ref[...], preferred_element_type=F32))
    x1 = x_ref[...] + m[2:3] * y
    x1_ref[...] = x1
    h2 = _modulate(x1, n2_ref[...], m[3:4], m[4:5])
    _route(h2, rw_ref, rb_ref, lrow_ref, gate_ref, cnt_ref, xs_ref)


def out0(a, f, x, mod, woa, wof, n2g, rw, rb):
    full = lambda w: pl.BlockSpec(w.shape, lambda i: (0,) * w.ndim)
    rspecs, rshapes = _route_out_specs(NT_ALL)
    return pl.pallas_call(
        _out0_body,
        grid=(NT_ALL,),
        in_specs=[pl.BlockSpec((TILE, HEADS * VDIM), lambda i: (i, 0)),
                  pl.BlockSpec((TILE, FW), lambda i: (i, 0)),
                  pl.BlockSpec((TILE, D), lambda i: (i, 0)),
                  pl.BlockSpec((1, 6, D), lambda i: (_mod_index(i), 0, 0)),
                  full(woa), full(wof), full(n2g), full(rw), full(rb)],
        out_specs=[pl.BlockSpec((TILE, D), lambda i: (i, 0))] + rspecs,
        out_shape=[jax.ShapeDtypeStruct((T_ALL, D), F32)] + rshapes,
        compiler_params=_cp(("arbitrary",)),
        name="out0",
    )(a, f, x, mod, woa, wof, n2g, rw, rb)


def _chunk_lists(cnt, n_tiles):
    n_blk = (n_tiles * MAXCH + CPB - 1) // CPB + NE
    cnt = cnt[:, 0, :NE].astype(jnp.int32)
    nch = (cnt + CH - 1) // CH
    lend = jnp.cumsum(nch, axis=1)
    loff = lend - nch
    tend = jnp.cumsum(nch, axis=0)
    toff = tend - nch
    tot = tend[-1]
    blk = (tot + CPB - 1) // CPB
    bend = jnp.cumsum(blk)
    estart = (bend - blk) * CPB
    nvalid = bend[-1]
    bi = jnp.arange(n_blk, dtype=jnp.int32)
    bvalid = bi < nvalid
    count_le = lambda ends, v: jnp.sum((ends[None, :] <= v[:, None]).astype(jnp.int32), axis=1)
    be = jnp.where(bvalid, jnp.minimum(count_le(bend, bi), NE - 1), jnp.sum((bend < nvalid).astype(jnp.int32)))
    g = jnp.arange(n_blk * CPB, dtype=jnp.int32)
    sel = (jnp.repeat(be, CPB)[:, None] == jnp.arange(NE, dtype=jnp.int32)[None, :]).astype(F32)
    pick = lambda table: jnp.dot(sel, table.astype(F32), precision=HI).astype(jnp.int32)
    o = g - pick(estart[:, None])[:, 0]
    gvalid = jnp.logical_and(jnp.repeat(bvalid, CPB), o < pick(tot[:, None])[:, 0])
    tend_g = pick(tend.T)
    tile = jnp.minimum(jnp.sum((tend_g <= o[:, None]).astype(jnp.int32), axis=1), n_tiles - 1)
    tsel = tile[:, None] == jnp.arange(n_tiles, dtype=jnp.int32)[None, :]
    shift = jnp.sum(jnp.where(tsel, pick((loff - toff).T), 0), axis=1)
    src = jnp.where(gvalid, tile * NCHT + o + shift, 0).astype(jnp.int32)
    cslot = jnp.arange(NCHT, dtype=jnp.int32)
    inside = jnp.logical_and(loff[:, None, :] <= cslot[None, :, None], cslot[None, :, None] < lend[:, None, :])
    base = estart[None, :] + toff - loff
    dst = cslot[None, :] + jnp.sum(jnp.where(inside, base[:, None, :], 0), axis=2)
    dst = jnp.where(cslot[None, :] < lend[:, -1:], dst, 0).astype(jnp.int32)
    return src, be.astype(jnp.int32), bvalid.astype(jnp.int32), dst.reshape(-1), n_blk


def _chunk_copy(src_hbm, chunk, buf, slot, j, sem):
    start = chunk * CH if isinstance(chunk, int) else pl.multiple_of(chunk * CH, CH)
    return pltpu.make_async_copy(src_hbm.at[pl.ds(start, CH), :],
                                 buf.at[slot, pl.ds(j * CH, CH), :], sem.at[slot])


def _gather_chunks(list_ref, step, n_steps, per_step, src_hbm, buf, sem):
    def issue(st, slot):
        for j in range(per_step):
            _chunk_copy(src_hbm, list_ref[st * per_step + j], buf, slot, j, sem).start()

    @pl.when(step == 0)
    def _():
        issue(0, 0)

    @pl.when(step + 1 < n_steps)
    def _():
        issue(step + 1, (step + 1) % 2)

    slot = step % 2
    for j in range(per_step):
        _chunk_copy(src_hbm, 0, buf, slot, j, sem).wait()
    return slot


def _ffn_body(src_ref, be_ref, bv_ref, xs_hbm, wg_ref, wl_ref, bg_ref, bl_ref, wd_ref, bd_ref,
              yb_ref, xbuf, sem):
    i = pl.program_id(0)
    slot = _gather_chunks(src_ref, i, pl.num_programs(0), CPB, xs_hbm, xbuf, sem)

    @pl.when(bv_ref[i] == 1)
    def _():
        xb = xbuf[slot]
        glu = jnp.dot(xb, wg_ref[0], preferred_element_type=F32) + bg_ref[0]
        lin = jnp.dot(xb, wl_ref[0], preferred_element_type=F32) + bl_ref[0]
        glu = jnp.minimum(glu, LIMIT)
        lin = jnp.clip(lin, -LIMIT, LIMIT)
        act = glu * jax.nn.sigmoid(ALPHA * glu) * (lin + 1.0)
        out = jnp.dot(act.astype(BF16), wd_ref[0], preferred_element_type=F32) + bd_ref[0]
        yb_ref[...] = out.astype(BF16)

    @pl.when(bv_ref[i] == 0)
    def _():
        yb_ref[...] = jnp.zeros_like(yb_ref)


def expert_ffn(xs, src, be, bv, n_blk, wg, wl, bg, bl, wd, bd):
    wspec = lambda shp: pl.BlockSpec((1,) + shp, lambda i, src, be, bv: (be[i], 0, 0))
    grid_spec = pltpu.PrefetchScalarGridSpec(
        num_scalar_prefetch=3,
        grid=(n_blk,),
        in_specs=[pl.BlockSpec(memory_space=pl.ANY),
                  wspec((D, DFF)), wspec((D, DFF)), wspec((1, DFF)), wspec((1, DFF)),
                  wspec((DFF, D)), wspec((1, D))],
        out_specs=pl.BlockSpec((FBLK, D), lambda i, *_: (i, 0)),
        scratch_shapes=[pltpu.VMEM((2, FBLK, D), BF16), pltpu.SemaphoreType.DMA((2,))],
    )
    return pl.pallas_call(
        _ffn_body,
        grid_spec=grid_spec,
        out_shape=jax.ShapeDtypeStruct((n_blk * FBLK, D), BF16),
        compiler_params=_cp(("arbitrary",)),
        name="expert_ffn",
    )(src, be, bv, xs, wg, wl, bg, bl, wd, bd)


def _combine(dst_ref, step, n_steps, yb_hbm, ybuf, sem, lrow_ref, gate_ref):
    slot = _gather_chunks(dst_ref, step, n_steps, NCHT, yb_hbm, ybuf, sem)
    lrow = lrow_ref[...]
    gate = gate_ref[...]
    scol = lax.broadcasted_iota(jnp.int32, (TILE, SROWS), 1)
    w = jnp.zeros((TILE, SROWS), F32)
    for kk in range(TOPK):
        w = w + jnp.where(lrow[:, kk:kk + 1] == scol, gate[:, kk:kk + 1], 0.0)
    return jnp.dot(w.astype(BF16), ybuf[slot], preferred_element_type=F32)


def _odd_in_body(dst_ref, x_ref, yb_hbm, lrow_ref, gate_ref, mod0_ref, mod1_ref, n1_ref, wq_ref, wk_ref, wv_ref,
                 wgo_ref, wgd_ref, wgkf_ref, bgkf_ref, wgkb_ref, bgkb_ref,
                 x2_ref, q_ref, k_ref, v_ref, go_ref, gf_ref, gb_ref, ybuf, sem):
    m0 = mod0_ref[0]
    m1 = mod1_ref[0]
    y = _combine(dst_ref, pl.program_id(0), pl.num_programs(0), yb_hbm, ybuf, sem, lrow_ref, gate_ref)
    x2 = x_ref[...] + m0[5:6] * y
    x2_ref[...] = x2
    h = _modulate(x2, n1_ref[...], m1[0:1], m1[1:2]).astype(BF16)
    q_ref[...] = jnp.dot(h, wq_ref[...], preferred_element_type=F32) * (GDK ** -0.5)
    k_ref[...] = jnp.dot(h, wk_ref[...], preferred_element_type=F32)
    v_ref[...] = jnp.dot(h, wv_ref[...], preferred_element_type=F32).astype(BF16)
    go_ref[...] = jnp.dot(h, wgo_ref[...], preferred_element_type=F32).astype(BF16)
    gd = jnp.dot(h, wgd_ref[...], preferred_element_type=F32)

    def decay(w_ref, b_ref):
        z = jnp.dot(gd, w_ref[...], precision=HI, preferred_element_type=F32) + b_ref[...]
        return (jnp.minimum(z, 0.0) - jnp.log(1.0 + jnp.exp(-jnp.abs(z)))) / GNORM

    gf_ref[...] = decay(wgkf_ref, bgkf_ref)
    gb_ref[...] = decay(wgkb_ref, bgkb_ref)


def odd_in(dst, x1, yb, lrow, gate, mod0, mod1, n1g, wq, wk, wv, wgo, wgd, wgkf, bgkf, wgkb, bgkb):
    full = lambda w: pl.BlockSpec(w.shape, lambda i, d: (0,) * w.ndim)
    row = lambda w: pl.BlockSpec((TILE, w), lambda i, d: (i, 0))
    modspec = pl.BlockSpec((1, 6, D), lambda i, d: (_mod_index(i), 0, 0))
    grid_spec = pltpu.PrefetchScalarGridSpec(
        num_scalar_prefetch=1,
        grid=(NT_ALL,),
        in_specs=[row(D), pl.BlockSpec(memory_space=pl.ANY), row(LANES), row(LANES), modspec, modspec,
                  full(n1g), full(wq), full(wk), full(wv), full(wgo),
                  full(wgd), full(wgkf), full(bgkf), full(wgkb), full(bgkb)],
        out_specs=[row(D), row(GKW), row(GKW), row(GVW), row(GVW), row(GKW), row(GKW)],
        scratch_shapes=[pltpu.VMEM((2, SROWS, D), BF16), pltpu.SemaphoreType.DMA((2,))],
    )
    return pl.pallas_call(
        _odd_in_body,
        grid_spec=grid_spec,
        out_shape=[jax.ShapeDtypeStruct((T_ALL, D), F32),
                   jax.ShapeDtypeStruct((T_ALL, GKW), F32),
                   jax.ShapeDtypeStruct((T_ALL, GKW), F32),
                   jax.ShapeDtypeStruct((T_ALL, GVW), BF16),
                   jax.ShapeDtypeStruct((T_ALL, GVW), BF16),
                   jax.ShapeDtypeStruct((T_ALL, GKW), F32),
                   jax.ShapeDtypeStruct((T_ALL, GKW), F32)],
        compiler_params=_cp(("arbitrary",), VMEM_BIG),
        name="odd_in",
    )(dst, x1, yb, lrow, gate, mod0, mod1, n1g, wq, wk, wv, wgo, wgd, wgkf, bgkf, wgkb, bgkb)


def _gla_body(qf_ref, kf_ref, vf_ref, gf_ref, qb_ref, kb_ref, vb_ref, gb_ref, of_ref, ob_ref, sf_ref, sb_ref):
    @pl.when(pl.program_id(1) == 0)
    def _():
        sf_ref[...] = jnp.zeros_like(sf_ref)
        sb_ref[...] = jnp.zeros_like(sb_ref)

    r = lax.broadcasted_iota(jnp.int32, (GCHUNK, GCHUNK), 0)
    c = lax.broadcasted_iota(jnp.int32, (GCHUNK, GCHUNK), 1)
    dirs = [
        (qf_ref, kf_ref, vf_ref, gf_ref, sf_ref, of_ref, c <= r, GCHUNK - 1),
        (qb_ref, kb_ref, vb_ref, gb_ref, sb_ref, ob_ref, c >= r, 0),
    ]
    chains = []
    for q_ref, k_ref, v_ref, g_ref, s_ref, o_ref, mask, last in dirs:
        cum = jnp.dot(mask.astype(F32), g_ref[...], precision=HI, preferred_element_type=F32)
        tot = cum[last:last + 1, :]
        q = q_ref[...]
        k = k_ref[...]
        qd = (q * jnp.exp(cum)).astype(BF16)
        kd = (k * jnp.exp(-cum)).astype(BF16)
        ke = (k * jnp.exp(tot - cum)).astype(BF16)
        dec = jnp.exp(tot)
        v = v_ref[...]
        for hd in range(GH):
            ks = slice(hd * GDK, (hd + 1) * GDK)
            vs = slice(hd * GDV, (hd + 1) * GDV)
            chains.append(dict(qd=qd[:, ks], kd=kd[:, ks], ke=ke[:, ks], dec=dec[:, ks], v=v[:, vs],
                               st=s_ref[hd], mask=mask))
    for ch in chains:
        sc = lax.dot_general(ch["qd"], ch["kd"], NT_DIMS, preferred_element_type=F32)
        ch["sc"] = jnp.where(ch["mask"], sc, 0.0).astype(BF16)
    for ch in chains:
        ch["out"] = (lax.dot_general(ch["qd"], ch["st"].astype(BF16), NT_DIMS, preferred_element_type=F32)
                     + jnp.dot(ch["sc"], ch["v"], preferred_element_type=F32))
    for ch in chains:
        ch["new"] = ch["st"] * ch["dec"] + lax.dot_general(ch["v"], ch["ke"], TN_DIMS, preferred_element_type=F32)
    for d, (_, _, _, _, s_ref, o_ref, _, _) in enumerate(dirs):
        o_ref[...] = jnp.concatenate([chains[d * GH + hd]["out"] for hd in range(GH)], axis=-1)
        for hd in range(GH):
            s_ref[hd] = chains[d * GH + hd]["new"]


def _bwd_chunk(c):
    return jnp.where(c < NCH_CTX, NCH_CTX - 1 - c, NCH_B + NCH_CTX - 1 - c)


def gla(q, k, v, gf, gb):
    fw = lambda w: pl.BlockSpec((GCHUNK, w), lambda b, c: (b * NCH_B + c, 0))
    bw = lambda w: pl.BlockSpec((GCHUNK, w), lambda b, c: (b * NCH_B + _bwd_chunk(c), 0))
    return pl.pallas_call(
        _gla_body,
        grid=(B, NCH_B),
        in_specs=[fw(GKW), fw(GKW), fw(GVW), fw(GKW), bw(GKW), bw(GKW), bw(GVW), bw(GKW)],
        out_specs=[fw(GVW), bw(GVW)],
        out_shape=[jax.ShapeDtypeStruct((T_ALL, GVW), F32), jax.ShapeDtypeStruct((T_ALL, GVW), F32)],
        scratch_shapes=[pltpu.VMEM((GH, GDV, GDK), F32), pltpu.VMEM((GH, GDV, GDK), F32)],
        compiler_params=_cp(("arbitrary", "arbitrary")),
        name="gla",
    )(q, k, v, gf, q, k, v, gb)


def _out1_body(of_ref, ob_ref, go_ref, x_ref, mod_ref, gn_ref, wo_ref, n2_ref, rw_ref, rb_ref,
               x3_ref, lrow_ref, gate_ref, cnt_ref, xs_ref):
    m = mod_ref[0]
    o = of_ref[...] + ob_ref[...]
    gn = gn_ref[...]
    parts = [_rms(o[:, hd * GDV:(hd + 1) * GDV], gn) for hd in range(GH)]
    on = jnp.concatenate(parts, axis=-1) * _silu(go_ref[...].astype(F32))
    y = jnp.dot(on.astype(BF16), wo_ref[...], preferred_element_type=F32)
    x3 = x_ref[...] + m[2:3] * y
    x3_ref[...] = x3
    h2 = _modulate(x3, n2_ref[...], m[3:4], m[4:5])
    _route(h2, rw_ref, rb_ref, lrow_ref, gate_ref, cnt_ref, xs_ref)


def out1(of, ob, go, x2, mod, gng, wo, n2g, rw, rb):
    full = lambda w: pl.BlockSpec(w.shape, lambda n: (0,) * w.ndim)
    lat = lambda w: pl.BlockSpec((TILE, w), lambda n: (_lat_tile(n), 0))
    rspecs, rshapes = _route_out_specs(NT_LAT)
    return pl.pallas_call(
        _out1_body,
        grid=(NT_LAT,),
        in_specs=[lat(GVW), lat(GVW), lat(GVW), lat(D),
                  pl.BlockSpec((1, 6, D), lambda n: (n // LPB, 0, 0)),
                  full(gng), full(wo), full(n2g), full(rw), full(rb)],
        out_specs=[pl.BlockSpec((TILE, D), lambda n: (n, 0))] + rspecs,
        out_shape=[jax.ShapeDtypeStruct((T_LAT, D), F32)] + rshapes,
        compiler_params=_cp(("arbitrary",)),
        name="out1",
    )(of, ob, go, x2, mod, gng, wo, n2g, rw, rb)


def _final_body(dst_ref, x_ref, yb_hbm, lrow_ref, gate_ref, mod_ref, g_ref, o_ref, ybuf, sem):
    m = mod_ref[0]
    y = _combine(dst_ref, pl.program_id(0), pl.num_programs(0), yb_hbm, ybuf, sem, lrow_ref, gate_ref)
    o_ref[...] = _rms(x_ref[...] + m[5:6] * y, g_ref[...])


def final(dst, x3, yb, lrow, gate, mod, g):
    grid_spec = pltpu.PrefetchScalarGridSpec(
        num_scalar_prefetch=1,
        grid=(NT_LAT,),
        in_specs=[pl.BlockSpec((TILE, D), lambda n, d: (n, 0)),
                  pl.BlockSpec(memory_space=pl.ANY),
                  pl.BlockSpec((TILE, LANES), lambda n, d: (n, 0)),
                  pl.BlockSpec((TILE, LANES), lambda n, d: (n, 0)),
                  pl.BlockSpec((1, 6, D), lambda n, d: (n // LPB, 0, 0)),
                  pl.BlockSpec((1, D), lambda n, d: (0, 0))],
        out_specs=pl.BlockSpec((TILE, D), lambda n, d: (n, 0)),
        scratch_shapes=[pltpu.VMEM((2, SROWS, D), BF16), pltpu.SemaphoreType.DMA((2,))],
    )
    return pl.pallas_call(
        _final_body,
        grid_spec=grid_spec,
        out_shape=jax.ShapeDtypeStruct((T_LAT, D), F32),
        compiler_params=_cp(("arbitrary",)),
        name="final",
    )(dst, x3, yb, lrow, gate, mod, g)


def _rot_cols(w):
    a, b, c, d = jnp.split(w, 4, axis=-1)
    return jnp.concatenate([-b, a, -d, c], axis=-1)


def _rope_tables():
    n = np.arange(SEQ)
    row = (n // GRID_W).astype(np.float32)
    col = (n % GRID_W).astype(np.float32)
    axis_dim = ROPE // 2
    inv = (ROPE_BASE ** (-np.arange(0, axis_dim, 2, dtype=np.float32) / axis_dim)).astype(np.float32)
    ar = row[:, None] * inv
    ac = col[:, None] * inv
    ang = np.concatenate([ar, ar, ac, ac], axis=-1).astype(np.float32)
    cos = np.concatenate([np.ones((CTX, ROPE), np.float32), np.cos(ang)], axis=0)
    sin = np.concatenate([np.zeros((CTX, ROPE), np.float32), np.sin(ang)], axis=0)
    return jnp.asarray(np.tile(cos, (1, HEADS))), jnp.asarray(np.tile(sin, (1, HEADS)))


def _dft(n, scale):
    j = np.arange(n, dtype=np.int64)
    ang = 2.0 * np.pi * ((j[:, None] * j[None, :]) % n).astype(np.float64) / n
    return (np.cos(ang) * scale), (np.sin(ang) * scale)


def _dft_tables():
    cn, sn = _dft(SEQ, SEQ ** -0.5)
    c2, s2 = _dft(CTX, CTX ** -0.5)
    cg, sg = _dft(FGD, FGD ** -0.5)
    eye = np.eye(FG)
    to = lambda a: jnp.asarray(a.astype(np.float32)).astype(BF16)
    return to(cn), to(sn), to(c2), to(s2), to(np.kron(eye, cg)), to(np.kron(eye, sg))


def _router_pad(rw, rb):
    rwp = jnp.zeros((D, LANES), F32).at[:, :NE].set(rw)
    rbp = jnp.full((1, LANES), -1e30, F32).at[0, :NE].set(rb)
    return rwp, rbp


def _expert_weights(w_gu, b_gu, w_down, b_down):
    wgu = w_gu.reshape(NE, D, DFF, 2)
    bgu = b_gu.reshape(NE, 1, DFF, 2)
    return (wgu[..., 0].astype(BF16), wgu[..., 1].astype(BF16), bgu[..., 0], bgu[..., 1],
            w_down.astype(BF16), b_down.reshape(NE, 1, D))


def kernel(x, c, ctx, c_ctx, final_norm_g, l0_mod_w, l0_mod_b, l0_norm1_g, l0_w_in, l0_q_norm_g, l0_w_q_up, l0_kv_norm_g, l0_w_kv_up, l0_w_out, l0_norm2_g, l0_router_w, l0_router_b, l0_w_gu, l0_b_gu, l0_w_down, l0_b_down, l1_mod_w, l1_mod_b, l1_norm1_g, l1_w_in, l1_w_gk_fwd, l1_b_gk_fwd, l1_w_gk_bwd, l1_b_gk_bwd, l1_gnorm_g, l1_w_out, l1_norm2_g, l1_router_w, l1_router_b, l1_w_gu, l1_b_gu, l1_w_down, l1_b_down):
    row = lambda g: g.reshape(1, -1)
    xs = jnp.concatenate([ctx, x], axis=1).reshape(T_ALL, D)
    cc = jnp.zeros((16, D), F32).at[:B].set(c).at[B].set(c_ctx)
    mod0 = adaln(cc, l0_mod_w, l0_mod_b)
    mod1 = adaln(cc, l1_mod_w, l1_mod_b)

    w_in = l0_w_in
    wqa = w_in[:, :Q_LORA].astype(BF16)
    wkva = w_in[:, Q_LORA:Q_LORA + KV_LORA].astype(BF16)
    wkr = w_in[:, Q_LORA + KV_LORA:Q_LORA + KV_LORA + ROPE]
    wkr2 = jnp.concatenate([wkr, _rot_cols(wkr)], axis=-1).astype(BF16)
    wf = w_in[:, Q_LORA + KV_LORA + ROPE:].astype(BF16)
    wq3 = l0_w_q_up.reshape(Q_LORA, HEADS, QK)
    wq_nope = wq3[:, :, :NOPE].reshape(Q_LORA, HEADS * NOPE)
    wq_rope = wq3[:, :, NOPE:]
    wqup = jnp.concatenate([wq_nope, wq_rope.reshape(Q_LORA, HEADS * ROPE),
                            _rot_cols(wq_rope).reshape(Q_LORA, HEADS * ROPE)], axis=-1).astype(BF16)
    wkv3 = l0_w_kv_up.reshape(KV_LORA, HEADS, NOPE + VDIM)
    wkvup = jnp.concatenate([wkv3[:, :, :NOPE].reshape(KV_LORA, HEADS * NOPE),
                             wkv3[:, :, NOPE:].reshape(KV_LORA, HEADS * VDIM)], axis=-1).astype(BF16)
    cos6, sin6 = _rope_tables()
    q, k, v, uf = even_in(xs, mod0, row(l0_norm1_g), wqa, wkva, wkr2, wf, row(l0_q_norm_g), wqup,
                          row(l0_kv_norm_g), wkvup, cos6, sin6)
    att = attention(q, k, v).reshape(T_ALL, HEADS * VDIM)
    fmix = fourier(uf.reshape(B, ROWS_B, FW), *_dft_tables()).reshape(T_ALL, FW)
    rw0, rb0 = _router_pad(l0_router_w, l0_router_b)
    x1, lrow0, gate0, cnt0, xs0 = out0(att, fmix, xs, mod0, l0_w_out[:HEADS * VDIM].astype(BF16),
                                       l0_w_out[HEADS * VDIM:].astype(BF16), row(l0_norm2_g), rw0, rb0)
    src0, be0, bv0, dst0, nb0 = _chunk_lists(cnt0, NT_ALL)
    yb0 = expert_ffn(xs0, src0, be0, bv0, nb0, *_expert_weights(l0_w_gu, l0_b_gu, l0_w_down, l0_b_down))

    w1 = l1_w_in
    o = 0
    wq1 = w1[:, o:o + GKW].astype(BF16); o += GKW
    wk1 = w1[:, o:o + GKW].astype(BF16); o += GKW
    wv1 = w1[:, o:o + GVW].astype(BF16); o += GVW
    wgo = w1[:, o:o + GVW].astype(BF16); o += GVW
    wgd = jnp.zeros((D, LANES), F32).at[:, :2 * GRANK].set(w1[:, o:]).astype(BF16)
    wgkf = jnp.zeros((LANES, GKW), F32).at[:GRANK].set(l1_w_gk_fwd)
    wgkb = jnp.zeros((LANES, GKW), F32).at[GRANK:2 * GRANK].set(l1_w_gk_bwd)
    x2, q1, k1, v1, go, gf, gb = odd_in(dst0, x1, yb0, lrow0, gate0, mod0, mod1, row(l1_norm1_g), wq1, wk1, wv1,
                                        wgo, wgd, wgkf, row(l1_b_gk_fwd), wgkb, row(l1_b_gk_bwd))
    o_f, o_b = gla(q1, k1, v1, gf, gb)
    rw1, rb1 = _router_pad(l1_router_w, l1_router_b)
    x3, lrow1, gate1, cnt1, xs1 = out1(o_f, o_b, go, x2, mod1, row(l1_gnorm_g), l1_w_out.astype(BF16),
                                       row(l1_norm2_g), rw1, rb1)
    src1, be1, bv1, dst1, nb1 = _chunk_lists(cnt1, NT_LAT)
    yb1 = expert_ffn(xs1, src1, be1, bv1, nb1, *_expert_weights(l1_w_gu, l1_b_gu, l1_w_down, l1_b_down))
    out = final(dst1, x3, yb1, lrow1, gate1, mod1, row(final_norm_g))
    return out.reshape(B, SEQ, D)
```

```python
import functools

import jax
import jax.numpy as jnp
import numpy as np
from jax import lax
from jax.experimental import pallas as pl
from jax.experimental.pallas import tpu as pltpu

D = 1024
B = 8
SEQ = 4096
CTX = 256
GRID_W = 64
EPS = 1e-6
HEADS = 6
Q_LORA = 384
KV_LORA = 256
NOPE = 128
ROPE = 64
VDIM = 128
QK = NOPE + ROPE
ROPE_BASE = 10000.0
FG = 4
FGD = 64
FW = FG * FGD
GH = 4
GDK = 128
GDV = 256
GRANK = 16
GNORM = 16.0
GCHUNK = 64
GKW = GH * GDK
GVW = GH * GDV
NE = 32
TOPK = 4
DFF = 1024
ALPHA = 1.702
LIMIT = 7.0

LANES = 128
TILE = 256
ROWS_B = CTX + SEQ
TPB = ROWS_B // TILE
T_ALL = B * ROWS_B
NT_ALL = T_ALL // TILE
T_LAT = B * SEQ
NT_LAT = T_LAT // TILE
LPB = SEQ // TILE
HPS = 2
CH = 16
MAXCH = (TILE * TOPK + NE * (CH - 1)) // CH
NCHT = 96
SROWS = NCHT * CH
FBLK = 256
CPB = FBLK // CH
NCH_B = ROWS_B // GCHUNK
NCH_CTX = CTX // GCHUNK
VMEM_BIG = 56 * 1024 * 1024
VMEM_MID = 40 * 1024 * 1024

F32 = jnp.float32
BF16 = jnp.bfloat16
HI = lax.Precision.HIGHEST
NT_DIMS = (((1,), (1,)), ((), ()))
TN_DIMS = (((0,), (0,)), ((), ()))


def _cp(sem, vmem=VMEM_MID):
    return pltpu.CompilerParams(dimension_semantics=sem, vmem_limit_bytes=vmem)


def _rms(x, g):
    return x * lax.rsqrt(jnp.mean(x * x, axis=-1, keepdims=True) + EPS) * g


def _modulate(x, g, sh, sc):
    return _rms(x, g) * (1.0 + sc) + sh


def _silu(x):
    return x * jax.nn.sigmoid(x)


def _mod_index(i):
    return jnp.where(i % TPB == 0, B, i // TPB)


def _lat_tile(n):
    return n + n // LPB + 1


def _adaln_body(c_ref, w_ref, b_ref, o_ref):
    s = _silu(c_ref[...])
    o_ref[...] = jnp.dot(s, w_ref[...], precision=HI, preferred_element_type=F32) + b_ref[...]


def adaln(cc, mod_w, mod_b):
    tn = 1536
    out = pl.pallas_call(
        _adaln_body,
        grid=(6 * D // tn,),
        in_specs=[pl.BlockSpec((16, D), lambda j: (0, 0)),
                  pl.BlockSpec((D, tn), lambda j: (0, j)),
                  pl.BlockSpec((1, tn), lambda j: (0, j))],
        out_specs=pl.BlockSpec((16, tn), lambda j: (0, j)),
        out_shape=jax.ShapeDtypeStruct((16, 6 * D), F32),
        compiler_params=_cp(("arbitrary",)),
        name="adaln",
    )(cc, mod_w, mod_b.reshape(1, 6 * D))
    return out.reshape(16, 6, D)


def _stream_tile(c_ref, x_ref):
    return jnp.where(pl.program_id(0) % TPB == 0, c_ref[...], x_ref[...])


def _stream_specs():
    return [pl.BlockSpec((TILE, D), lambda i: (i // TPB, 0)),
            pl.BlockSpec((TILE, D), lambda i: ((i // TPB) * LPB + jnp.maximum(i % TPB - 1, 0), 0))]


def _even_in_body(c_ref, x_ref, mod_ref, n1_ref, wqa_ref, wkva_ref, wkr_ref, wf_ref, qg_ref, wqup_ref,
                  kvg_ref, wkvup_ref, cos_ref, sin_ref, q_ref, k_ref, v_ref, uf_ref):
    m = mod_ref[0]
    h = _modulate(_stream_tile(c_ref, x_ref), n1_ref[...], m[0:1], m[1:2]).astype(BF16)
    cq = jnp.dot(h, wqa_ref[...], preferred_element_type=F32)
    ckv = jnp.dot(h, wkva_ref[...], preferred_element_type=F32)
    kr2 = jnp.dot(h, wkr_ref[...], preferred_element_type=F32)
    uf_ref[...] = jnp.dot(h, wf_ref[...], preferred_element_type=F32).astype(BF16)
    cos6 = cos_ref[...]
    sin6 = sin_ref[...]
    scale = QK ** -0.5
    qall = jnp.dot(_rms(cq, qg_ref[...]).astype(BF16), wqup_ref[...], preferred_element_type=F32)
    qn = qall[:, :HEADS * NOPE] * scale
    qr = (qall[:, HEADS * NOPE:HEADS * QK] * cos6 + qall[:, HEADS * QK:] * sin6) * scale
    kvall = jnp.dot(_rms(ckv, kvg_ref[...]).astype(BF16), wkvup_ref[...], preferred_element_type=F32)
    kr = (kr2[:, :ROPE] * cos6[:, :ROPE] + kr2[:, ROPE:] * sin6[:, :ROPE]).astype(BF16)
    for hd in range(HEADS):
        q_ref[0, hd, :, 0:NOPE] = qn[:, hd * NOPE:(hd + 1) * NOPE].astype(BF16)
        q_ref[0, hd, :, NOPE:QK] = qr[:, hd * ROPE:(hd + 1) * ROPE].astype(BF16)
        k_ref[0, hd, :, 0:NOPE] = kvall[:, hd * NOPE:(hd + 1) * NOPE].astype(BF16)
        k_ref[0, hd, :, NOPE:QK] = kr
        v_ref[0, hd] = kvall[:, HEADS * NOPE + hd * VDIM:HEADS * NOPE + (hd + 1) * VDIM].astype(BF16)


def even_in(c2, x2, mod, n1g, wqa, wkva, wkr2, wf, qg, wqup, kvg, wkvup, cos6, sin6):
    full = lambda a: pl.BlockSpec(a.shape, lambda i: (0,) * a.ndim)
    hs = lambda w: pl.BlockSpec((1, HEADS, TILE, w), lambda i: (i // TPB, 0, i % TPB, 0))
    return pl.pallas_call(
        _even_in_body,
        grid=(NT_ALL,),
        in_specs=_stream_specs() + [
                  pl.BlockSpec((1, 6, D), lambda i: (_mod_index(i), 0, 0)),
                  full(n1g), full(wqa), full(wkva), full(wkr2), full(wf), full(qg), full(wqup),
                  full(kvg), full(wkvup),
                  pl.BlockSpec((TILE, HEADS * ROPE), lambda i: (i % TPB, 0)),
                  pl.BlockSpec((TILE, HEADS * ROPE), lambda i: (i % TPB, 0))],
        out_specs=[hs(QK), hs(QK), hs(VDIM), pl.BlockSpec((TILE, FW), lambda i: (i, 0))],
        out_shape=[jax.ShapeDtypeStruct((B, HEADS, ROWS_B, QK), BF16),
                   jax.ShapeDtypeStruct((B, HEADS, ROWS_B, QK), BF16),
                   jax.ShapeDtypeStruct((B, HEADS, ROWS_B, VDIM), BF16),
                   jax.ShapeDtypeStruct((T_ALL, FW), BF16)],
        compiler_params=_cp(("arbitrary",)),
        name="even_in",
    )(c2, x2, mod, n1g, wqa, wkva, wkr2, wf, qg, wqup, kvg, wkvup, cos6, sin6)


def _attend(q_ref, k_ref, v_ref, o_ref, rows):
    ks = [k_ref[0, j, 0:rows, :] for j in range(HPS)]
    ss = [lax.dot_general(q_ref[0, j], ks[j], NT_DIMS, preferred_element_type=F32) for j in range(HPS)]
    ps, ls = [], []
    for s in ss:
        p = jnp.exp(s - jnp.max(s, axis=-1, keepdims=True))
        ls.append(jnp.sum(p, axis=-1, keepdims=True))
        ps.append(p.astype(BF16))
    os = [jnp.dot(ps[j], v_ref[0, j, 0:rows, :], preferred_element_type=F32) / ls[j] for j in range(HPS)]
    o_ref[0] = jnp.concatenate(os, axis=-1).astype(BF16)


def _attn_body(q_ref, k_ref, v_ref, o_ref):
    qt = pl.program_id(2)

    @pl.when(qt == 0)
    def _():
        _attend(q_ref, k_ref, v_ref, o_ref, CTX)

    @pl.when(qt > 0)
    def _():
        _attend(q_ref, k_ref, v_ref, o_ref, ROWS_B)


def attention(q, k, v):
    return pl.pallas_call(
        _attn_body,
        grid=(B, HEADS // HPS, TPB),
        in_specs=[pl.BlockSpec((1, HPS, TILE, QK), lambda b, h, t: (b, h, t, 0)),
                  pl.BlockSpec((1, HPS, ROWS_B, QK), lambda b, h, t: (b, h, 0, 0)),
                  pl.BlockSpec((1, HPS, ROWS_B, VDIM), lambda b, h, t: (b, h, 0, 0))],
        out_specs=pl.BlockSpec((1, TILE, HPS * VDIM), lambda b, h, t: (b, t, h)),
        out_shape=jax.ShapeDtypeStruct((B, ROWS_B, HEADS * VDIM), BF16),
        compiler_params=_cp(("arbitrary", "arbitrary", "arbitrary"), VMEM_BIG),
        name="attention",
    )(q, k, v)


def _fourier_body(u_ref, cn_ref, sn_ref, c2_ref, s2_ref, cc_ref, sc_ref, o_ref):
    i = pl.program_id(0)

    def finish(a, b):
        o_ref[0] = (jnp.dot(a.astype(BF16), cc_ref[...], preferred_element_type=F32)
                    - jnp.dot(b.astype(BF16), sc_ref[...], preferred_element_type=F32)).astype(BF16)

    @pl.when(i == 0)
    def _():
        u = u_ref[0, 0:CTX, :]
        finish(jnp.dot(c2_ref[...], u, preferred_element_type=F32),
               jnp.dot(s2_ref[...], u, preferred_element_type=F32))

    @pl.when(i > 0)
    def _():
        u = u_ref[0, CTX:ROWS_B, :]
        finish(jnp.dot(cn_ref[...], u, preferred_element_type=F32),
               jnp.dot(sn_ref[...], u, preferred_element_type=F32))


def fourier(uf, cn, sn, c2, s2, ccb, scb):
    full = lambda a: pl.BlockSpec(a.shape, lambda i, b: (0,) * a.ndim)
    return pl.pallas_call(
        _fourier_body,
        grid=(TPB, B),
        in_specs=[pl.BlockSpec((1, ROWS_B, FW), lambda i, b: (b, 0, 0)),
                  pl.BlockSpec((TILE, SEQ), lambda i, b: (jnp.maximum(i - 1, 0), 0)),
                  pl.BlockSpec((TILE, SEQ), lambda i, b: (jnp.maximum(i - 1, 0), 0)),
                  full(c2), full(s2), full(ccb), full(scb)],
        out_specs=pl.BlockSpec((1, TILE, FW), lambda i, b: (b, i, 0)),
        out_shape=jax.ShapeDtypeStruct((B, ROWS_B, FW), BF16),
        compiler_params=_cp(("arbitrary", "arbitrary")),
        name="fourier",
    )(uf, cn, sn, c2, s2, ccb, scb)


def _route(h2, rw_ref, rb_ref, lrow_ref, gate_ref, cnt_ref, xs_ref):
    h_hi = h2.astype(BF16)
    h_lo = (h2 - h_hi.astype(F32)).astype(BF16)
    rw = rw_ref[...]
    w_hi = rw.astype(BF16)
    w_lo = (rw - w_hi.astype(F32)).astype(BF16)
    both = jnp.dot(h_hi, jnp.concatenate([w_hi, w_lo], axis=-1), preferred_element_type=F32)
    logits = (both[:, :LANES] + both[:, LANES:]
              + jnp.dot(h_lo, w_hi, preferred_element_type=F32) + rb_ref[...])
    lane = lax.broadcasted_iota(jnp.int32, logits.shape, 1)
    l = logits
    vals, idxs = [], []
    for _ in range(TOPK):
        mx = jnp.max(l, axis=-1, keepdims=True)
        am = jnp.min(jnp.where(l == mx, lane, LANES), axis=-1, keepdims=True)
        vals.append(mx)
        idxs.append(am)
        l = jnp.where(lane == am, -jnp.inf, l)
    ex = [jnp.exp(vv - vals[0]) for vv in vals]
    den = ex[0] + ex[1] + ex[2] + ex[3]
    onehot = jnp.zeros(logits.shape, F32)
    for am in idxs:
        onehot = onehot + (lane == am).astype(F32)
    r = lax.broadcasted_iota(jnp.int32, (TILE, TILE), 0)
    c = lax.broadcasted_iota(jnp.int32, (TILE, TILE), 1)
    strict = (c < r).astype(BF16)
    rank = jnp.dot(strict, onehot.astype(BF16), preferred_element_type=F32)
    cnt = jnp.sum(onehot, axis=0, keepdims=True)
    nch = jnp.ceil(cnt * (1.0 / CH))
    er = lax.broadcasted_iota(jnp.int32, (LANES, LANES), 0)
    ec = lax.broadcasted_iota(jnp.int32, (LANES, LANES), 1)
    before = (er < ec).astype(BF16)
    seg = jnp.dot(jnp.broadcast_to(nch, (8, LANES)).astype(BF16), before,
                  preferred_element_type=F32)[0:1, :] * CH
    dest = rank + seg
    lrow = jnp.zeros(logits.shape, F32)
    gate = jnp.zeros(logits.shape, F32)
    for kk in range(TOPK):
        pos = jnp.sum(jnp.where(lane == idxs[kk], dest, 0.0), axis=-1, keepdims=True)
        lrow = jnp.where(lane == kk, pos, lrow)
        gate = jnp.where(lane == kk, ex[kk] / den, gate)
    lrow_ref[...] = lrow.astype(jnp.int32)
    gate_ref[...] = gate
    cnt_ref[0] = jnp.broadcast_to(cnt, (8, LANES))
    lrow_t = lrow.T.astype(jnp.int32)
    srow = lax.broadcasted_iota(jnp.int32, (SROWS, TILE), 0)
    hit = lrow_t[0:1, :] == srow
    for kk in range(1, TOPK):
        hit = jnp.logical_or(hit, lrow_t[kk:kk + 1, :] == srow)
    xs_ref[...] = jnp.dot(hit.astype(BF16), h_hi, preferred_element_type=F32).astype(BF16)


def _route_out_specs(n_tiles):
    specs = [pl.BlockSpec((TILE, LANES), lambda i: (i, 0)),
             pl.BlockSpec((TILE, LANES), lambda i: (i, 0)),
             pl.BlockSpec((1, 8, LANES), lambda i: (i, 0, 0)),
             pl.BlockSpec((SROWS, D), lambda i: (i, 0))]
    shapes = [jax.ShapeDtypeStruct((n_tiles * TILE, LANES), jnp.int32),
              jax.ShapeDtypeStruct((n_tiles * TILE, LANES), F32),
              jax.ShapeDtypeStruct((n_tiles, 8, LANES), F32),
              jax.ShapeDtypeStruct((n_tiles * SROWS, D), BF16)]
    return specs, shapes


def _out0_body(c_ref, x_ref, a_ref, f_ref, mod_ref, woa_ref, wof_ref, n2_ref, rw_ref, rb_ref,
               x1_ref, lrow_ref, gate_ref, cnt_ref, xs_ref):
    m = mod_ref[0]
    y = (jnp.dot(a_ref[...], woa_ref[...], preferred_element_type=F32)
         + jnp.dot(f_ref[...], wof_ref[...], preferred_element_type=F32))
    x1 = _stream_tile(c_ref, x_ref) + m[2:3] * y
    x1_ref[...] = x1
    h2 = _modulate(x1, n2_ref[...], m[3:4], m[4:5])
    _route(h2, rw_ref, rb_ref, lrow_ref, gate_ref, cnt_ref, xs_ref)


def out0(c2, x2, a, f, mod, woa, wof, n2g, rw, rb):
    full = lambda w: pl.BlockSpec(w.shape, lambda i: (0,) * w.ndim)
    rspecs, rshapes = _route_out_specs(NT_ALL)
    return pl.pallas_call(
        _out0_body,
        grid=(NT_ALL,),
        in_specs=_stream_specs() + [
                  pl.BlockSpec((TILE, HEADS * VDIM), lambda i: (i, 0)),
                  pl.BlockSpec((TILE, FW), lambda i: (i, 0)),
                  pl.BlockSpec((1, 6, D), lambda i: (_mod_index(i), 0, 0)),
                  full(woa), full(wof), full(n2g), full(rw), full(rb)],
        out_specs=[pl.BlockSpec((TILE, D), lambda i: (i, 0))] + rspecs,
        out_shape=[jax.ShapeDtypeStruct((T_ALL, D), F32)] + rshapes,
        compiler_params=_cp(("arbitrary",)),
        name="out0",
    )(c2, x2, a, f, mod, woa, wof, n2g, rw, rb)


def _chunk_lists(cnt, n_tiles):
    n_blk = (n_tiles * MAXCH + CPB - 1) // CPB + NE
    cnt = cnt[:, 0, :NE].astype(jnp.int32)
    nch = (cnt + CH - 1) // CH
    lend = jnp.cumsum(nch, axis=1)
    loff = lend - nch
    tend = jnp.cumsum(nch, axis=0)
    toff = tend - nch
    tot = tend[-1]
    blk = (tot + CPB - 1) // CPB
    bend = jnp.cumsum(blk)
    estart = (bend - blk) * CPB
    nvalid = bend[-1]
    bi = jnp.arange(n_blk, dtype=jnp.int32)
    bvalid = bi < nvalid
    count_le = lambda ends, v: jnp.sum((ends[None, :] <= v[:, None]).astype(jnp.int32), axis=1)
    be = jnp.where(bvalid, jnp.minimum(count_le(bend, bi), NE - 1), jnp.sum((bend < nvalid).astype(jnp.int32)))
    g = jnp.arange(n_blk * CPB, dtype=jnp.int32)
    sel = (jnp.repeat(be, CPB)[:, None] == jnp.arange(NE, dtype=jnp.int32)[None, :]).astype(F32)
    pick = lambda table: jnp.dot(sel, table.astype(F32), precision=HI).astype(jnp.int32)
    o = g - pick(estart[:, None])[:, 0]
    gvalid = jnp.logical_and(jnp.repeat(bvalid, CPB), o < pick(tot[:, None])[:, 0])
    tend_g = pick(tend.T)
    tile = jnp.minimum(jnp.sum((tend_g <= o[:, None]).astype(jnp.int32), axis=1), n_tiles - 1)
    tsel = tile[:, None] == jnp.arange(n_tiles, dtype=jnp.int32)[None, :]
    shift = jnp.sum(jnp.where(tsel, pick((loff - toff).T), 0), axis=1)
    src = jnp.where(gvalid, tile * NCHT + o + shift, 0).astype(jnp.int32)
    cslot = jnp.arange(NCHT, dtype=jnp.int32)
    inside = jnp.logical_and(loff[:, None, :] <= cslot[None, :, None], cslot[None, :, None] < lend[:, None, :])
    base = estart[None, :] + toff - loff
    dst = cslot[None, :] + jnp.sum(jnp.where(inside, base[:, None, :], 0), axis=2)
    dst = jnp.where(cslot[None, :] < lend[:, -1:], dst, 0).astype(jnp.int32)
    return src, be.astype(jnp.int32), bvalid.astype(jnp.int32), dst.reshape(-1), n_blk


def _chunk_copy(src_hbm, chunk, buf, slot, j, sem):
    start = chunk * CH if isinstance(chunk, int) else pl.multiple_of(chunk * CH, CH)
    return pltpu.make_async_copy(src_hbm.at[pl.ds(start, CH), :],
                                 buf.at[slot, pl.ds(j * CH, CH), :], sem.at[slot])


def _gather_chunks(list_ref, step, n_steps, per_step, src_hbm, buf, sem):
    def issue(st, slot):
        for j in range(per_step):
            _chunk_copy(src_hbm, list_ref[st * per_step + j], buf, slot, j, sem).start()

    @pl.when(step == 0)
    def _():
        issue(0, 0)

    @pl.when(step + 1 < n_steps)
    def _():
        issue(step + 1, (step + 1) % 2)

    slot = step % 2
    for j in range(per_step):
        _chunk_copy(src_hbm, 0, buf, slot, j, sem).wait()
    return slot


def _ffn_body(src_ref, be_ref, bv_ref, bf_ref, xs_hbm, wgu_ref, bg_ref, bl_ref, wdn_ref, bd_ref,
              yb_ref, xbuf, sem, wg_s, wl_s, wd_s):
    i = pl.program_id(0)
    slot = _gather_chunks(src_ref, i, pl.num_programs(0), CPB, xs_hbm, xbuf, sem)

    @pl.when(bf_ref[i] == 1)
    def _():
        r = lax.broadcasted_iota(jnp.int32, (2 * LANES, 2 * LANES), 0)
        c = lax.broadcasted_iota(jnp.int32, (2 * LANES, 2 * LANES), 1)
        split = (r == 2 * (c % LANES) + c // LANES).astype(BF16)
        for g in range(DFF // LANES):
            pair = wgu_ref[0, :, g * 2 * LANES:(g + 1) * 2 * LANES].astype(BF16)
            sep = jnp.dot(pair, split, preferred_element_type=F32)
            wg_s[:, g * LANES:(g + 1) * LANES] = sep[:, :LANES].astype(BF16)
            wl_s[:, g * LANES:(g + 1) * LANES] = sep[:, LANES:].astype(BF16)
        wd_s[...] = wdn_ref[0].astype(BF16)

    @pl.when(bv_ref[i] == 1)
    def _():
        xb = xbuf[slot]
        glu = jnp.dot(xb, wg_s[...], preferred_element_type=F32) + bg_ref[0]
        lin = jnp.dot(xb, wl_s[...], preferred_element_type=F32) + bl_ref[0]
        glu = jnp.minimum(glu, LIMIT)
        lin = jnp.clip(lin, -LIMIT, LIMIT)
        act = glu * jax.nn.sigmoid(ALPHA * glu) * (lin + 1.0)
        out = jnp.dot(act.astype(BF16), wd_s[...], preferred_element_type=F32) + bd_ref[0]
        yb_ref[...] = out.astype(BF16)

    @pl.when(bv_ref[i] == 0)
    def _():
        yb_ref[...] = jnp.zeros_like(yb_ref)


def expert_ffn(xs, src, be, bv, n_blk, w_gu, b_gu, w_down, b_down):
    bf = jnp.concatenate([jnp.ones((1,), jnp.int32), (be[1:] != be[:-1]).astype(jnp.int32)])
    bgu = b_gu.reshape(NE, 1, DFF, 2)
    wspec = lambda shp: pl.BlockSpec((1,) + shp, lambda i, src, be, bv, bf: (be[i], 0, 0))
    grid_spec = pltpu.PrefetchScalarGridSpec(
        num_scalar_prefetch=4,
        grid=(n_blk,),
        in_specs=[pl.BlockSpec(memory_space=pl.ANY),
                  wspec((D, 2 * DFF)), wspec((1, DFF)), wspec((1, DFF)),
                  wspec((DFF, D)), wspec((1, D))],
        out_specs=pl.BlockSpec((FBLK, D), lambda i, *_: (i, 0)),
        scratch_shapes=[pltpu.VMEM((2, FBLK, D), BF16), pltpu.SemaphoreType.DMA((2,)),
                        pltpu.VMEM((D, DFF), BF16), pltpu.VMEM((D, DFF), BF16), pltpu.VMEM((DFF, D), BF16)],
    )
    return pl.pallas_call(
        _ffn_body,
        grid_spec=grid_spec,
        out_shape=jax.ShapeDtypeStruct((n_blk * FBLK, D), BF16),
        compiler_params=_cp(("arbitrary",), VMEM_BIG),
        name="expert_ffn",
    )(src, be, bv, bf, xs, w_gu, bgu[..., 0], bgu[..., 1], w_down, b_down.reshape(NE, 1, D))


def _combine(dst_ref, step, n_steps, yb_hbm, ybuf, sem, lrow_ref, gate_ref):
    slot = _gather_chunks(dst_ref, step, n_steps, NCHT, yb_hbm, ybuf, sem)
    lrow = lrow_ref[...]
    gate = gate_ref[...]
    scol = lax.broadcasted_iota(jnp.int32, (TILE, SROWS), 1)
    w = jnp.zeros((TILE, SROWS), F32)
    for kk in range(TOPK):
        w = w + jnp.where(lrow[:, kk:kk + 1] == scol, gate[:, kk:kk + 1], 0.0)
    return jnp.dot(w.astype(BF16), ybuf[slot], preferred_element_type=F32)


def _odd_in_body(dst_ref, x_ref, yb_hbm, lrow_ref, gate_ref, mod0_ref, mod1_ref, n1_ref, wq_ref, wk_ref, wv_ref,
                 wgo_ref, wgd_ref, wgkf_ref, bgkf_ref, wgkb_ref, bgkb_ref,
                 x2_ref, q_ref, k_ref, v_ref, go_ref, gf_ref, gb_ref, ybuf, sem):
    m0 = mod0_ref[0]
    m1 = mod1_ref[0]
    y = _combine(dst_ref, pl.program_id(0), pl.num_programs(0), yb_hbm, ybuf, sem, lrow_ref, gate_ref)
    x2 = x_ref[...] + m0[5:6] * y
    x2_ref[...] = x2
    h = _modulate(x2, n1_ref[...], m1[0:1], m1[1:2]).astype(BF16)
    q_ref[...] = jnp.dot(h, wq_ref[...], preferred_element_type=F32) * (GDK ** -0.5)
    k_ref[...] = jnp.dot(h, wk_ref[...], preferred_element_type=F32)
    v_ref[...] = jnp.dot(h, wv_ref[...], preferred_element_type=F32).astype(BF16)
    go_ref[...] = jnp.dot(h, wgo_ref[...], preferred_element_type=F32).astype(BF16)
    gd = jnp.dot(h, wgd_ref[...], preferred_element_type=F32)

    def decay(w_ref, b_ref):
        z = jnp.dot(gd, w_ref[...], precision=HI, preferred_element_type=F32) + b_ref[...]
        return (jnp.minimum(z, 0.0) - jnp.log(1.0 + jnp.exp(-jnp.abs(z)))) / GNORM

    gf_ref[...] = decay(wgkf_ref, bgkf_ref)
    gb_ref[...] = decay(wgkb_ref, bgkb_ref)


def odd_in(dst, x1, yb, lrow, gate, mod0, mod1, n1g, wq, wk, wv, wgo, wgd, wgkf, bgkf, wgkb, bgkb):
    full = lambda w: pl.BlockSpec(w.shape, lambda i, d: (0,) * w.ndim)
    row = lambda w: pl.BlockSpec((TILE, w), lambda i, d: (i, 0))
    modspec = pl.BlockSpec((1, 6, D), lambda i, d: (_mod_index(i), 0, 0))
    grid_spec = pltpu.PrefetchScalarGridSpec(
        num_scalar_prefetch=1,
        grid=(NT_ALL,),
        in_specs=[row(D), pl.BlockSpec(memory_space=pl.ANY), row(LANES), row(LANES), modspec, modspec,
                  full(n1g), full(wq), full(wk), full(wv), full(wgo),
                  full(wgd), full(wgkf), full(bgkf), full(wgkb), full(bgkb)],
        out_specs=[row(D), row(GKW), row(GKW), row(GVW), row(GVW), row(GKW), row(GKW)],
        scratch_shapes=[pltpu.VMEM((2, SROWS, D), BF16), pltpu.SemaphoreType.DMA((2,))],
    )
    return pl.pallas_call(
        _odd_in_body,
        grid_spec=grid_spec,
        out_shape=[jax.ShapeDtypeStruct((T_ALL, D), F32),
                   jax.ShapeDtypeStruct((T_ALL, GKW), F32),
                   jax.ShapeDtypeStruct((T_ALL, GKW), F32),
                   jax.ShapeDtypeStruct((T_ALL, GVW), BF16),
                   jax.ShapeDtypeStruct((T_ALL, GVW), BF16),
                   jax.ShapeDtypeStruct((T_ALL, GKW), F32),
                   jax.ShapeDtypeStruct((T_ALL, GKW), F32)],
        compiler_params=_cp(("arbitrary",), VMEM_BIG),
        name="odd_in",
    )(dst, x1, yb, lrow, gate, mod0, mod1, n1g, wq, wk, wv, wgo, wgd, wgkf, bgkf, wgkb, bgkb)


def _gla_body(qf_ref, kf_ref, vf_ref, gf_ref, qb_ref, kb_ref, vb_ref, gb_ref, of_ref, ob_ref, sf_ref, sb_ref):
    @pl.when(pl.program_id(1) == 0)
    def _():
        sf_ref[...] = jnp.zeros_like(sf_ref)
        sb_ref[...] = jnp.zeros_like(sb_ref)

    r = lax.broadcasted_iota(jnp.int32, (GCHUNK, GCHUNK), 0)
    c = lax.broadcasted_iota(jnp.int32, (GCHUNK, GCHUNK), 1)
    dirs = [
        (qf_ref, kf_ref, vf_ref, gf_ref, sf_ref, of_ref, c <= r, GCHUNK - 1),
        (qb_ref, kb_ref, vb_ref, gb_ref, sb_ref, ob_ref, c >= r, 0),
    ]
    chains = []
    for q_ref, k_ref, v_ref, g_ref, s_ref, o_ref, mask, last in dirs:
        cum = jnp.dot(mask.astype(F32), g_ref[...], precision=HI, preferred_element_type=F32)
        tot = cum[last:last + 1, :]
        q = q_ref[...]
        k = k_ref[...]
        qd = (q * jnp.exp(cum)).astype(BF16)
        kd = (k * jnp.exp(-cum)).astype(BF16)
        ke = (k * jnp.exp(tot - cum)).astype(BF16)
        dec = jnp.exp(tot)
        v = v_ref[...]
        for hd in range(GH):
            ks = slice(hd * GDK, (hd + 1) * GDK)
            vs = slice(hd * GDV, (hd + 1) * GDV)
            chains.append(dict(qd=qd[:, ks], kd=kd[:, ks], ke=ke[:, ks], dec=dec[:, ks], v=v[:, vs],
                               st=s_ref[hd], mask=mask))
    for ch in chains:
        sc = lax.dot_general(ch["qd"], ch["kd"], NT_DIMS, preferred_element_type=F32)
        ch["sc"] = jnp.where(ch["mask"], sc, 0.0).astype(BF16)
    for ch in chains:
        ch["out"] = (lax.dot_general(ch["qd"], ch["st"].astype(BF16), NT_DIMS, preferred_element_type=F32)
                     + jnp.dot(ch["sc"], ch["v"], preferred_element_type=F32))
    for ch in chains:
        ch["new"] = ch["st"] * ch["dec"] + lax.dot_general(ch["v"], ch["ke"], TN_DIMS, preferred_element_type=F32)
    for d, (_, _, _, _, s_ref, o_ref, _, _) in enumerate(dirs):
        o_ref[...] = jnp.concatenate([chains[d * GH + hd]["out"] for hd in range(GH)], axis=-1)
        for hd in range(GH):
            s_ref[hd] = chains[d * GH + hd]["new"]


def _bwd_chunk(c):
    return jnp.where(c < NCH_CTX, NCH_CTX - 1 - c, NCH_B + NCH_CTX - 1 - c)


def gla(q, k, v, gf, gb):
    fw = lambda w: pl.BlockSpec((GCHUNK, w), lambda b, c: (b * NCH_B + c, 0))
    bw = lambda w: pl.BlockSpec((GCHUNK, w), lambda b, c: (b * NCH_B + _bwd_chunk(c), 0))
    return pl.pallas_call(
        _gla_body,
        grid=(B, NCH_B),
        in_specs=[fw(GKW), fw(GKW), fw(GVW), fw(GKW), bw(GKW), bw(GKW), bw(GVW), bw(GKW)],
        out_specs=[fw(GVW), bw(GVW)],
        out_shape=[jax.ShapeDtypeStruct((T_ALL, GVW), F32), jax.ShapeDtypeStruct((T_ALL, GVW), F32)],
        scratch_shapes=[pltpu.VMEM((GH, GDV, GDK), F32), pltpu.VMEM((GH, GDV, GDK), F32)],
        compiler_params=_cp(("arbitrary", "arbitrary")),
        name="gla",
    )(q, k, v, gf, q, k, v, gb)


def _out1_body(of_ref, ob_ref, go_ref, x_ref, mod_ref, gn_ref, wo_ref, n2_ref, rw_ref, rb_ref,
               x3_ref, lrow_ref, gate_ref, cnt_ref, xs_ref):
    m = mod_ref[0]
    o = of_ref[...] + ob_ref[...]
    gn = gn_ref[...]
    parts = [_rms(o[:, hd * GDV:(hd + 1) * GDV], gn) for hd in range(GH)]
    on = jnp.concatenate(parts, axis=-1) * _silu(go_ref[...].astype(F32))
    y = jnp.dot(on.astype(BF16), wo_ref[...], preferred_element_type=F32)
    x3 = x_ref[...] + m[2:3] * y
    x3_ref[...] = x3
    h2 = _modulate(x3, n2_ref[...], m[3:4], m[4:5])
    _route(h2, rw_ref, rb_ref, lrow_ref, gate_ref, cnt_ref, xs_ref)


def out1(of, ob, go, x2, mod, gng, wo, n2g, rw, rb):
    full = lambda w: pl.BlockSpec(w.shape, lambda n: (0,) * w.ndim)
    lat = lambda w: pl.BlockSpec((TILE, w), lambda n: (_lat_tile(n), 0))
    rspecs, rshapes = _route_out_specs(NT_LAT)
    return pl.pallas_call(
        _out1_body,
        grid=(NT_LAT,),
        in_specs=[lat(GVW), lat(GVW), lat(GVW), lat(D),
                  pl.BlockSpec((1, 6, D), lambda n: (n // LPB, 0, 0)),
                  full(gng), full(wo), full(n2g), full(rw), full(rb)],
        out_specs=[pl.BlockSpec((TILE, D), lambda n: (n, 0))] + rspecs,
        out_shape=[jax.ShapeDtypeStruct((T_LAT, D), F32)] + rshapes,
        compiler_params=_cp(("arbitrary",)),
        name="out1",
    )(of, ob, go, x2, mod, gng, wo, n2g, rw, rb)


def _final_body(dst_ref, x_ref, yb_hbm, lrow_ref, gate_ref, mod_ref, g_ref, o_ref, ybuf, sem):
    m = mod_ref[0]
    y = _combine(dst_ref, pl.program_id(0), pl.num_programs(0), yb_hbm, ybuf, sem, lrow_ref, gate_ref)
    o_ref[...] = _rms(x_ref[...] + m[5:6] * y, g_ref[...])


def final(dst, x3, yb, lrow, gate, mod, g):
    grid_spec = pltpu.PrefetchScalarGridSpec(
        num_scalar_prefetch=1,
        grid=(NT_LAT,),
        in_specs=[pl.BlockSpec((TILE, D), lambda n, d: (n, 0)),
                  pl.BlockSpec(memory_space=pl.ANY),
                  pl.BlockSpec((TILE, LANES), lambda n, d: (n, 0)),
                  pl.BlockSpec((TILE, LANES), lambda n, d: (n, 0)),
                  pl.BlockSpec((1, 6, D), lambda n, d: (n // LPB, 0, 0)),
                  pl.BlockSpec((1, D), lambda n, d: (0, 0))],
        out_specs=pl.BlockSpec((TILE, D), lambda n, d: (n, 0)),
        scratch_shapes=[pltpu.VMEM((2, SROWS, D), BF16), pltpu.SemaphoreType.DMA((2,))],
    )
    return pl.pallas_call(
        _final_body,
        grid_spec=grid_spec,
        out_shape=jax.ShapeDtypeStruct((T_LAT, D), F32),
        compiler_params=_cp(("arbitrary",)),
        name="final",
    )(dst, x3, yb, lrow, gate, mod, g)


def _rot_cols(w):
    a, b, c, d = jnp.split(w, 4, axis=-1)
    return jnp.concatenate([-b, a, -d, c], axis=-1)


def _rope_tables():
    n = np.arange(SEQ)
    row = (n // GRID_W).astype(np.float32)
    col = (n % GRID_W).astype(np.float32)
    axis_dim = ROPE // 2
    inv = (ROPE_BASE ** (-np.arange(0, axis_dim, 2, dtype=np.float32) / axis_dim)).astype(np.float32)
    ar = row[:, None] * inv
    ac = col[:, None] * inv
    ang = np.concatenate([ar, ar, ac, ac], axis=-1).astype(np.float32)
    cos = np.concatenate([np.ones((CTX, ROPE), np.float32), np.cos(ang)], axis=0)
    sin = np.concatenate([np.zeros((CTX, ROPE), np.float32), np.sin(ang)], axis=0)
    return jnp.asarray(np.tile(cos, (1, HEADS))), jnp.asarray(np.tile(sin, (1, HEADS)))


def _dft(n, scale):
    j = np.arange(n, dtype=np.int64)
    ang = 2.0 * np.pi * ((j[:, None] * j[None, :]) % n).astype(np.float64) / n
    return (np.cos(ang) * scale), (np.sin(ang) * scale)


def _dft_tables():
    cn, sn = _dft(SEQ, SEQ ** -0.5)
    c2, s2 = _dft(CTX, CTX ** -0.5)
    cg, sg = _dft(FGD, FGD ** -0.5)
    eye = np.eye(FG)
    to = lambda a: jnp.asarray(a.astype(np.float32)).astype(BF16)
    return to(cn), to(sn), to(c2), to(s2), to(np.kron(eye, cg)), to(np.kron(eye, sg))


def _router_pad(rw, rb):
    rwp = jnp.zeros((D, LANES), F32).at[:, :NE].set(rw)
    rbp = jnp.full((1, LANES), -1e30, F32).at[0, :NE].set(rb)
    return rwp, rbp


def kernel(x, c, ctx, c_ctx, final_norm_g, l0_mod_w, l0_mod_b, l0_norm1_g, l0_w_in, l0_q_norm_g, l0_w_q_up, l0_kv_norm_g, l0_w_kv_up, l0_w_out, l0_norm2_g, l0_router_w, l0_router_b, l0_w_gu, l0_b_gu, l0_w_down, l0_b_down, l1_mod_w, l1_mod_b, l1_norm1_g, l1_w_in, l1_w_gk_fwd, l1_b_gk_fwd, l1_w_gk_bwd, l1_b_gk_bwd, l1_gnorm_g, l1_w_out, l1_norm2_g, l1_router_w, l1_router_b, l1_w_gu, l1_b_gu, l1_w_down, l1_b_down):
    row = lambda g: g.reshape(1, -1)
    c2 = ctx.reshape(B * CTX, D)
    x2 = x.reshape(T_LAT, D)
    cc = jnp.zeros((16, D), F32).at[:B].set(c).at[B].set(c_ctx)
    mod0 = adaln(cc, l0_mod_w, l0_mod_b)
    mod1 = adaln(cc, l1_mod_w, l1_mod_b)

    w_in = l0_w_in
    wqa = w_in[:, :Q_LORA].astype(BF16)
    wkva = w_in[:, Q_LORA:Q_LORA + KV_LORA].astype(BF16)
    wkr = w_in[:, Q_LORA + KV_LORA:Q_LORA + KV_LORA + ROPE]
    wkr2 = jnp.concatenate([wkr, _rot_cols(wkr)], axis=-1).astype(BF16)
    wf = w_in[:, Q_LORA + KV_LORA + ROPE:].astype(BF16)
    wq3 = l0_w_q_up.reshape(Q_LORA, HEADS, QK)
    wq_nope = wq3[:, :, :NOPE].reshape(Q_LORA, HEADS * NOPE)
    wq_rope = wq3[:, :, NOPE:]
    wqup = jnp.concatenate([wq_nope, wq_rope.reshape(Q_LORA, HEADS * ROPE),
                            _rot_cols(wq_rope).reshape(Q_LORA, HEADS * ROPE)], axis=-1).astype(BF16)
    wkv3 = l0_w_kv_up.reshape(KV_LORA, HEADS, NOPE + VDIM)
    wkvup = jnp.concatenate([wkv3[:, :, :NOPE].reshape(KV_LORA, HEADS * NOPE),
                             wkv3[:, :, NOPE:].reshape(KV_LORA, HEADS * VDIM)], axis=-1).astype(BF16)
    cos6, sin6 = _rope_tables()
    q, k, v, uf = even_in(c2, x2, mod0, row(l0_norm1_g), wqa, wkva, wkr2, wf, row(l0_q_norm_g), wqup,
                          row(l0_kv_norm_g), wkvup, cos6, sin6)
    att = attention(q, k, v).reshape(T_ALL, HEADS * VDIM)
    fmix = fourier(uf.reshape(B, ROWS_B, FW), *_dft_tables()).reshape(T_ALL, FW)
    rw0, rb0 = _router_pad(l0_router_w, l0_router_b)
    x1, lrow0, gate0, cnt0, xs0 = out0(c2, x2, att, fmix, mod0, l0_w_out[:HEADS * VDIM].astype(BF16),
                                       l0_w_out[HEADS * VDIM:].astype(BF16), row(l0_norm2_g), rw0, rb0)
    src0, be0, bv0, dst0, nb0 = _chunk_lists(cnt0, NT_ALL)
    yb0 = expert_ffn(xs0, src0, be0, bv0, nb0, l0_w_gu, l0_b_gu, l0_w_down, l0_b_down)

    w1 = l1_w_in
    o = 0
    wq1 = w1[:, o:o + GKW].astype(BF16); o += GKW
    wk1 = w1[:, o:o + GKW].astype(BF16); o += GKW
    wv1 = w1[:, o:o + GVW].astype(BF16); o += GVW
    wgo = w1[:, o:o + GVW].astype(BF16); o += GVW
    wgd = jnp.zeros((D, LANES), F32).at[:, :2 * GRANK].set(w1[:, o:]).astype(BF16)
    wgkf = jnp.zeros((LANES, GKW), F32).at[:GRANK].set(l1_w_gk_fwd)
    wgkb = jnp.zeros((LANES, GKW), F32).at[GRANK:2 * GRANK].set(l1_w_gk_bwd)
    x2, q1, k1, v1, go, gf, gb = odd_in(dst0, x1, yb0, lrow0, gate0, mod0, mod1, row(l1_norm1_g), wq1, wk1, wv1,
                                        wgo, wgd, wgkf, row(l1_b_gk_fwd), wgkb, row(l1_b_gk_bwd))
    o_f, o_b = gla(q1, k1, v1, gf, gb)
    rw1, rb1 = _router_pad(l1_router_w, l1_router_b)
    x3, lrow1, gate1, cnt1, xs1 = out1(o_f, o_b, go, x2, mod1, row(l1_gnorm_g), l1_w_out.astype(BF16),
                                       row(l1_norm2_g), rw1, rb1)
    src1, be1, bv1, dst1, nb1 = _chunk_lists(cnt1, NT_LAT)
    yb1 = expert_ffn(xs1, src1, be1, bv1, nb1, l1_w_gu, l1_b_gu, l1_w_down, l1_b_down)
    out = final(dst1, x3, yb1, lrow1, gate1, mod1, row(final_norm_g))
    return out.reshape(B, SEQ, D)
```

```python
import functools

import jax
import jax.numpy as jnp
import numpy as np
from jax import lax
from jax.experimental import pallas as pl
from jax.experimental.pallas import tpu as pltpu

D = 1024
B = 8
SEQ = 4096
CTX = 256
GRID_W = 64
EPS = 1e-6
HEADS = 6
Q_LORA = 384
KV_LORA = 256
NOPE = 128
ROPE = 64
VDIM = 128
QK = NOPE + ROPE
ROPE_BASE = 10000.0
FG = 4
FGD = 64
FW = FG * FGD
GH = 4
GDK = 128
GDV = 256
GRANK = 16
GNORM = 16.0
GCHUNK = 64
GKW = GH * GDK
GVW = GH * GDV
NE = 32
TOPK = 4
DFF = 1024
ALPHA = 1.702
LIMIT = 7.0

LANES = 128
TILE = 256
ROWS_B = CTX + SEQ
TPB = ROWS_B // TILE
T_ALL = B * ROWS_B
NT_ALL = T_ALL // TILE
T_LAT = B * SEQ
NT_LAT = T_LAT // TILE
LPB = SEQ // TILE
HPS = 2
CH = 16
MAXCH = (TILE * TOPK + NE * (CH - 1)) // CH
NCHT = 96
SROWS = NCHT * CH
FBLK = 512
CPB = FBLK // CH
GBS = 4
NCH_B = ROWS_B // GCHUNK
NCH_CTX = CTX // GCHUNK
VMEM_BIG = 56 * 1024 * 1024
VMEM_MID = 40 * 1024 * 1024

F32 = jnp.float32
BF16 = jnp.bfloat16
HI = lax.Precision.HIGHEST
NT_DIMS = (((1,), (1,)), ((), ()))
TN_DIMS = (((0,), (0,)), ((), ()))


def _cp(sem, vmem=VMEM_MID):
    return pltpu.CompilerParams(dimension_semantics=sem, vmem_limit_bytes=vmem)


def _rms(x, g):
    return x * lax.rsqrt(jnp.mean(x * x, axis=-1, keepdims=True) + EPS) * g


def _modulate(x, g, sh, sc):
    return _rms(x, g) * (1.0 + sc) + sh


def _silu(x):
    return x * jax.nn.sigmoid(x)


def _mod_index(i):
    return jnp.where(i % TPB == 0, B, i // TPB)


def _lat_tile(n):
    return n + n // LPB + 1


def _adaln_body(c_ref, w_ref, b_ref, o_ref):
    s = _silu(c_ref[...])
    o_ref[...] = jnp.dot(s, w_ref[...], precision=HI, preferred_element_type=F32) + b_ref[...]


def adaln(cc, mod_w, mod_b):
    tn = 1536
    out = pl.pallas_call(
        _adaln_body,
        grid=(6 * D // tn,),
        in_specs=[pl.BlockSpec((16, D), lambda j: (0, 0)),
                  pl.BlockSpec((D, tn), lambda j: (0, j)),
                  pl.BlockSpec((1, tn), lambda j: (0, j))],
        out_specs=pl.BlockSpec((16, tn), lambda j: (0, j)),
        out_shape=jax.ShapeDtypeStruct((16, 6 * D), F32),
        compiler_params=_cp(("arbitrary",)),
        name="adaln",
    )(cc, mod_w, mod_b.reshape(1, 6 * D))
    return out.reshape(16, 6, D)


def _stream_tile(c_ref, x_ref):
    return jnp.where(pl.program_id(0) % TPB == 0, c_ref[...], x_ref[...])


def _stream_specs():
    return [pl.BlockSpec((TILE, D), lambda i: (i // TPB, 0)),
            pl.BlockSpec((TILE, D), lambda i: ((i // TPB) * LPB + jnp.maximum(i % TPB - 1, 0), 0))]


def _even_in_body(c_ref, x_ref, mod_ref, n1_ref, wqa_ref, wkva_ref, wkr_ref, wf_ref, qg_ref, wqup_ref,
                  kvg_ref, wkvup_ref, cos_ref, sin_ref, q_ref, k_ref, v_ref, uf_ref):
    m = mod_ref[0]
    h = _modulate(_stream_tile(c_ref, x_ref), n1_ref[...], m[0:1], m[1:2]).astype(BF16)
    cq = jnp.dot(h, wqa_ref[...], preferred_element_type=F32)
    ckv = jnp.dot(h, wkva_ref[...], preferred_element_type=F32)
    kr2 = jnp.dot(h, wkr_ref[...], preferred_element_type=F32)
    uf_ref[...] = jnp.dot(h, wf_ref[...], preferred_element_type=F32).astype(BF16)
    cos6 = cos_ref[...]
    sin6 = sin_ref[...]
    scale = QK ** -0.5
    qall = jnp.dot(_rms(cq, qg_ref[...]).astype(BF16), wqup_ref[...], preferred_element_type=F32)
    qn = qall[:, :HEADS * NOPE] * scale
    qr = (qall[:, HEADS * NOPE:HEADS * QK] * cos6 + qall[:, HEADS * QK:] * sin6) * scale
    kvall = jnp.dot(_rms(ckv, kvg_ref[...]).astype(BF16), wkvup_ref[...], preferred_element_type=F32)
    kr = (kr2[:, :ROPE] * cos6[:, :ROPE] + kr2[:, ROPE:] * sin6[:, :ROPE]).astype(BF16)
    for hd in range(HEADS):
        q_ref[0, hd, :, 0:NOPE] = qn[:, hd * NOPE:(hd + 1) * NOPE].astype(BF16)
        q_ref[0, hd, :, NOPE:QK] = qr[:, hd * ROPE:(hd + 1) * ROPE].astype(BF16)
        k_ref[0, hd, :, 0:NOPE] = kvall[:, hd * NOPE:(hd + 1) * NOPE].astype(BF16)
        k_ref[0, hd, :, NOPE:QK] = kr
        v_ref[0, hd] = kvall[:, HEADS * NOPE + hd * VDIM:HEADS * NOPE + (hd + 1) * VDIM].astype(BF16)


def even_in(c2, x2, mod, n1g, wqa, wkva, wkr2, wf, qg, wqup, kvg, wkvup, cos6, sin6):
    full = lambda a: pl.BlockSpec(a.shape, lambda i: (0,) * a.ndim)
    hs = lambda w: pl.BlockSpec((1, HEADS, TILE, w), lambda i: (i // TPB, 0, i % TPB, 0))
    return pl.pallas_call(
        _even_in_body,
        grid=(NT_ALL,),
        in_specs=_stream_specs() + [
                  pl.BlockSpec((1, 6, D), lambda i: (_mod_index(i), 0, 0)),
                  full(n1g), full(wqa), full(wkva), full(wkr2), full(wf), full(qg), full(wqup),
                  full(kvg), full(wkvup),
                  pl.BlockSpec((TILE, HEADS * ROPE), lambda i: (i % TPB, 0)),
                  pl.BlockSpec((TILE, HEADS * ROPE), lambda i: (i % TPB, 0))],
        out_specs=[hs(QK), hs(QK), hs(VDIM), pl.BlockSpec((TILE, FW), lambda i: (i, 0))],
        out_shape=[jax.ShapeDtypeStruct((B, HEADS, ROWS_B, QK), BF16),
                   jax.ShapeDtypeStruct((B, HEADS, ROWS_B, QK), BF16),
                   jax.ShapeDtypeStruct((B, HEADS, ROWS_B, VDIM), BF16),
                   jax.ShapeDtypeStruct((T_ALL, FW), BF16)],
        compiler_params=_cp(("arbitrary",)),
        name="even_in",
    )(c2, x2, mod, n1g, wqa, wkva, wkr2, wf, qg, wqup, kvg, wkvup, cos6, sin6)


def _attend(q_ref, k_ref, v_ref, o_ref, rows):
    ks = [k_ref[0, j, 0:rows, :] for j in range(HPS)]
    ss = [lax.dot_general(q_ref[0, j], ks[j], NT_DIMS, preferred_element_type=F32) for j in range(HPS)]
    ps, ls = [], []
    for s in ss:
        p = jnp.exp(s - jnp.max(s, axis=-1, keepdims=True))
        ls.append(jnp.sum(p, axis=-1, keepdims=True))
        ps.append(p.astype(BF16))
    os = [jnp.dot(ps[j], v_ref[0, j, 0:rows, :], preferred_element_type=F32) / ls[j] for j in range(HPS)]
    o_ref[0] = jnp.concatenate(os, axis=-1).astype(BF16)


def _attn_body(q_ref, k_ref, v_ref, o_ref):
    qt = pl.program_id(2)

    @pl.when(qt == 0)
    def _():
        _attend(q_ref, k_ref, v_ref, o_ref, CTX)

    @pl.when(qt > 0)
    def _():
        _attend(q_ref, k_ref, v_ref, o_ref, ROWS_B)


def attention(q, k, v):
    return pl.pallas_call(
        _attn_body,
        grid=(B, HEADS // HPS, TPB),
        in_specs=[pl.BlockSpec((1, HPS, TILE, QK), lambda b, h, t: (b, h, t, 0)),
                  pl.BlockSpec((1, HPS, ROWS_B, QK), lambda b, h, t: (b, h, 0, 0)),
                  pl.BlockSpec((1, HPS, ROWS_B, VDIM), lambda b, h, t: (b, h, 0, 0))],
        out_specs=pl.BlockSpec((1, TILE, HPS * VDIM), lambda b, h, t: (b, t, h)),
        out_shape=jax.ShapeDtypeStruct((B, ROWS_B, HEADS * VDIM), BF16),
        compiler_params=_cp(("arbitrary", "arbitrary", "arbitrary"), VMEM_BIG),
        name="attention",
    )(q, k, v)


def _fourier_body(u_ref, cn_ref, sn_ref, c2_ref, s2_ref, cc_ref, sc_ref, o_ref):
    i = pl.program_id(0)

    def finish(a, b):
        o_ref[0] = (jnp.dot(a.astype(BF16), cc_ref[...], preferred_element_type=F32)
                    - jnp.dot(b.astype(BF16), sc_ref[...], preferred_element_type=F32)).astype(BF16)

    @pl.when(i == 0)
    def _():
        u = u_ref[0, 0:CTX, :]
        finish(jnp.dot(c2_ref[...], u, preferred_element_type=F32),
               jnp.dot(s2_ref[...], u, preferred_element_type=F32))

    @pl.when(i > 0)
    def _():
        u = u_ref[0, CTX:ROWS_B, :]
        finish(jnp.dot(cn_ref[...], u, preferred_element_type=F32),
               jnp.dot(sn_ref[...], u, preferred_element_type=F32))


def fourier(uf, cn, sn, c2, s2, ccb, scb):
    full = lambda a: pl.BlockSpec(a.shape, lambda i, b: (0,) * a.ndim)
    return pl.pallas_call(
        _fourier_body,
        grid=(TPB, B),
        in_specs=[pl.BlockSpec((1, ROWS_B, FW), lambda i, b: (b, 0, 0)),
                  pl.BlockSpec((TILE, SEQ), lambda i, b: (jnp.maximum(i - 1, 0), 0)),
                  pl.BlockSpec((TILE, SEQ), lambda i, b: (jnp.maximum(i - 1, 0), 0)),
                  full(c2), full(s2), full(ccb), full(scb)],
        out_specs=pl.BlockSpec((1, TILE, FW), lambda i, b: (b, i, 0)),
        out_shape=jax.ShapeDtypeStruct((B, ROWS_B, FW), BF16),
        compiler_params=_cp(("arbitrary", "arbitrary")),
        name="fourier",
    )(uf, cn, sn, c2, s2, ccb, scb)


def _route(h2, rw_ref, rb_ref, lrow_ref, gate_ref, cnt_ref, xs_ref):
    h_hi = h2.astype(BF16)
    h_lo = (h2 - h_hi.astype(F32)).astype(BF16)
    rw = rw_ref[...]
    w_hi = rw.astype(BF16)
    w_lo = (rw - w_hi.astype(F32)).astype(BF16)
    both = jnp.dot(h_hi, jnp.concatenate([w_hi, w_lo], axis=-1), preferred_element_type=F32)
    logits = (both[:, :LANES] + both[:, LANES:]
              + jnp.dot(h_lo, w_hi, preferred_element_type=F32) + rb_ref[...])
    lane = lax.broadcasted_iota(jnp.int32, logits.shape, 1)
    l = logits
    vals, idxs = [], []
    for _ in range(TOPK):
        mx = jnp.max(l, axis=-1, keepdims=True)
        am = jnp.min(jnp.where(l == mx, lane, LANES), axis=-1, keepdims=True)
        vals.append(mx)
        idxs.append(am)
        l = jnp.where(lane == am, -jnp.inf, l)
    ex = [jnp.exp(vv - vals[0]) for vv in vals]
    den = ex[0] + ex[1] + ex[2] + ex[3]
    onehot = jnp.zeros(logits.shape, F32)
    for am in idxs:
        onehot = onehot + (lane == am).astype(F32)
    r = lax.broadcasted_iota(jnp.int32, (TILE, TILE), 0)
    c = lax.broadcasted_iota(jnp.int32, (TILE, TILE), 1)
    strict = (c < r).astype(BF16)
    rank = jnp.dot(strict, onehot.astype(BF16), preferred_element_type=F32)
    cnt = jnp.sum(onehot, axis=0, keepdims=True)
    nch = jnp.ceil(cnt * (1.0 / CH))
    er = lax.broadcasted_iota(jnp.int32, (LANES, LANES), 0)
    ec = lax.broadcasted_iota(jnp.int32, (LANES, LANES), 1)
    before = (er < ec).astype(BF16)
    seg = jnp.dot(jnp.broadcast_to(nch, (8, LANES)).astype(BF16), before,
                  preferred_element_type=F32)[0:1, :] * CH
    dest = rank + seg
    lrow = jnp.zeros(logits.shape, F32)
    gate = jnp.zeros(logits.shape, F32)
    for kk in range(TOPK):
        pos = jnp.sum(jnp.where(lane == idxs[kk], dest, 0.0), axis=-1, keepdims=True)
        lrow = jnp.where(lane == kk, pos, lrow)
        gate = jnp.where(lane == kk, ex[kk] / den, gate)
    lrow_ref[...] = lrow.astype(jnp.int32)
    gate_ref[...] = gate
    cnt_ref[0] = jnp.broadcast_to(cnt, (8, LANES))
    lrow_t = lrow.T.astype(jnp.int32)
    srow = lax.broadcasted_iota(jnp.int32, (SROWS, TILE), 0)
    hit = lrow_t[0:1, :] == srow
    for kk in range(1, TOPK):
        hit = jnp.logical_or(hit, lrow_t[kk:kk + 1, :] == srow)
    xs_ref[...] = jnp.dot(hit.astype(BF16), h_hi, preferred_element_type=F32).astype(BF16)


def _route_out_specs(n_tiles):
    specs = [pl.BlockSpec((TILE, LANES), lambda i: (i, 0)),
             pl.BlockSpec((TILE, LANES), lambda i: (i, 0)),
             pl.BlockSpec((1, 8, LANES), lambda i: (i, 0, 0)),
             pl.BlockSpec((SROWS, D), lambda i: (i, 0))]
    shapes = [jax.ShapeDtypeStruct((n_tiles * TILE, LANES), jnp.int32),
              jax.ShapeDtypeStruct((n_tiles * TILE, LANES), F32),
              jax.ShapeDtypeStruct((n_tiles, 8, LANES), F32),
              jax.ShapeDtypeStruct((n_tiles * SROWS, D), BF16)]
    return specs, shapes


def _out0_body(c_ref, x_ref, a_ref, f_ref, mod_ref, woa_ref, wof_ref, n2_ref, rw_ref, rb_ref,
               x1_ref, lrow_ref, gate_ref, cnt_ref, xs_ref):
    m = mod_ref[0]
    y = (jnp.dot(a_ref[...], woa_ref[...], preferred_element_type=F32)
         + jnp.dot(f_ref[...], wof_ref[...], preferred_element_type=F32))
    x1 = _stream_tile(c_ref, x_ref) + m[2:3] * y
    x1_ref[...] = x1
    h2 = _modulate(x1, n2_ref[...], m[3:4], m[4:5])
    _route(h2, rw_ref, rb_ref, lrow_ref, gate_ref, cnt_ref, xs_ref)


def out0(c2, x2, a, f, mod, woa, wof, n2g, rw, rb):
    full = lambda w: pl.BlockSpec(w.shape, lambda i: (0,) * w.ndim)
    rspecs, rshapes = _route_out_specs(NT_ALL)
    return pl.pallas_call(
        _out0_body,
        grid=(NT_ALL,),
        in_specs=_stream_specs() + [
                  pl.BlockSpec((TILE, HEADS * VDIM), lambda i: (i, 0)),
                  pl.BlockSpec((TILE, FW), lambda i: (i, 0)),
                  pl.BlockSpec((1, 6, D), lambda i: (_mod_index(i), 0, 0)),
                  full(woa), full(wof), full(n2g), full(rw), full(rb)],
        out_specs=[pl.BlockSpec((TILE, D), lambda i: (i, 0))] + rspecs,
        out_shape=[jax.ShapeDtypeStruct((T_ALL, D), F32)] + rshapes,
        compiler_params=_cp(("arbitrary",)),
        name="out0",
    )(c2, x2, a, f, mod, woa, wof, n2g, rw, rb)


def _chunk_lists(cnt, n_tiles):
    n_blk = (n_tiles * MAXCH + CPB - 1) // CPB + NE
    cnt = cnt[:, 0, :NE].astype(jnp.int32)
    nch = (cnt + CH - 1) // CH
    lend = jnp.cumsum(nch, axis=1)
    loff = lend - nch
    tend = jnp.cumsum(nch, axis=0)
    toff = tend - nch
    tot = tend[-1]
    blk = (tot + CPB - 1) // CPB
    bend = jnp.cumsum(blk)
    estart = (bend - blk) * CPB
    nvalid = bend[-1]
    bi = jnp.arange(n_blk, dtype=jnp.int32)
    bvalid = bi < nvalid
    count_le = lambda ends, v: jnp.sum((ends[None, :] <= v[:, None]).astype(jnp.int32), axis=1)
    be = jnp.where(bvalid, jnp.minimum(count_le(bend, bi), NE - 1), jnp.sum((bend < nvalid).astype(jnp.int32)))
    g = jnp.arange(n_blk * CPB, dtype=jnp.int32)
    sel = (jnp.repeat(be, CPB)[:, None] == jnp.arange(NE, dtype=jnp.int32)[None, :]).astype(F32)
    pick = lambda table: jnp.dot(sel, table.astype(F32), precision=HI).astype(jnp.int32)
    o = g - pick(estart[:, None])[:, 0]
    gvalid = jnp.logical_and(jnp.repeat(bvalid, CPB), o < pick(tot[:, None])[:, 0])
    tend_g = pick(tend.T)
    tile = jnp.minimum(jnp.sum((tend_g <= o[:, None]).astype(jnp.int32), axis=1), n_tiles - 1)
    tsel = tile[:, None] == jnp.arange(n_tiles, dtype=jnp.int32)[None, :]
    shift = jnp.sum(jnp.where(tsel, pick((loff - toff).T), 0), axis=1)
    src = jnp.where(gvalid, tile * NCHT + o + shift, 0).astype(jnp.int32)
    cslot = jnp.arange(NCHT, dtype=jnp.int32)
    inside = jnp.logical_and(loff[:, None, :] <= cslot[None, :, None], cslot[None, :, None] < lend[:, None, :])
    base = estart[None, :] + toff - loff
    dst = cslot[None, :] + jnp.sum(jnp.where(inside, base[:, None, :], 0), axis=2)
    dst = jnp.where(cslot[None, :] < lend[:, -1:], dst, 0).astype(jnp.int32)
    src = jnp.concatenate([src, jnp.zeros((CPB,), jnp.int32)])
    dst = jnp.concatenate([dst.reshape(-1), jnp.zeros((NCHT,), jnp.int32)])
    return src, be.astype(jnp.int32), bvalid.astype(jnp.int32), dst, n_blk


def _chunk_copy(src_hbm, chunk, buf, slot, j, sem):
    start = chunk * CH if isinstance(chunk, int) else pl.multiple_of(chunk * CH, CH)
    return pltpu.make_async_copy(src_hbm.at[pl.ds(start, CH), :],
                                 buf.at[slot, pl.ds(j * CH, CH), :], sem.at[slot])


class _ChunkGather:
    def __init__(self, list_ref, per_step, src_hbm, buf, sem):
        self.list_ref, self.per_step, self.src_hbm, self.buf, self.sem = list_ref, per_step, src_hbm, buf, sem
        self.step = pl.program_id(0)
        self.n_steps = pl.num_programs(0)

    def _issue(self, st, slot):
        for j in range(self.per_step):
            _chunk_copy(self.src_hbm, self.list_ref[st * self.per_step + j], self.buf, slot, j, self.sem).start()

    def _wait(self, slot):
        for j in range(self.per_step):
            _chunk_copy(self.src_hbm, 0, self.buf, slot, j, self.sem).wait()

    def arrive(self):
        @pl.when(self.step == 0)
        def _():
            self._issue(0, 0)

        slot = self.step % 2
        self._wait(slot)
        return slot

    def fetch_next(self):
        self._issue(self.step + 1, (self.step + 1) % 2)

    def drain(self):
        @pl.when(self.step == self.n_steps - 1)
        def _():
            self._wait((self.step + 1) % 2)


def _ffn_body(src_ref, be_ref, bv_ref, bf_ref, xs_hbm, wgu_ref, bg_ref, bl_ref, wdn_ref, bd_ref,
              yb_ref, xbuf, sem, wg_s, wl_s, wd_s):
    i = pl.program_id(0)
    rows = _ChunkGather(src_ref, CPB, xs_hbm, xbuf, sem)
    slot = rows.arrive()

    @pl.when(bf_ref[i] == 1)
    def _():
        r = lax.broadcasted_iota(jnp.int32, (2 * LANES, 2 * LANES), 0)
        c = lax.broadcasted_iota(jnp.int32, (2 * LANES, 2 * LANES), 1)
        split = (r == 2 * (c % LANES) + c // LANES).astype(BF16)
        for g in range(DFF // LANES):
            pair = wgu_ref[0, :, g * 2 * LANES:(g + 1) * 2 * LANES].astype(BF16)
            sep = jnp.dot(pair, split, preferred_element_type=F32)
            wg_s[:, g * LANES:(g + 1) * LANES] = sep[:, :LANES].astype(BF16)
            wl_s[:, g * LANES:(g + 1) * LANES] = sep[:, LANES:].astype(BF16)
        wd_s[...] = wdn_ref[0].astype(BF16)

    @pl.when(bv_ref[i] == 1)
    def _():
        rows.fetch_next()
        xb = xbuf[slot]
        glu = jnp.dot(xb, wg_s[...], preferred_element_type=F32) + bg_ref[0]
        lin = jnp.dot(xb, wl_s[...], preferred_element_type=F32) + bl_ref[0]
        glu = jnp.minimum(glu, LIMIT)
        lin = jnp.clip(lin, -LIMIT, LIMIT)
        act = glu * jax.nn.sigmoid(ALPHA * glu) * (lin + 1.0)
        out = jnp.dot(act.astype(BF16), wd_s[...], preferred_element_type=F32) + bd_ref[0]
        yb_ref[...] = out.astype(BF16)

    @pl.when(bv_ref[i] == 0)
    def _():
        rows.fetch_next()
        yb_ref[...] = jnp.zeros_like(yb_ref)

    rows.drain()


def expert_ffn(xs, src, be, bv, n_blk, w_gu, b_gu, w_down, b_down):
    bf = jnp.concatenate([jnp.ones((1,), jnp.int32), (be[1:] != be[:-1]).astype(jnp.int32)])
    bgu = b_gu.reshape(NE, 1, DFF, 2)
    wspec = lambda shp: pl.BlockSpec((1,) + shp, lambda i, src, be, bv, bf: (be[i], 0, 0))
    grid_spec = pltpu.PrefetchScalarGridSpec(
        num_scalar_prefetch=4,
        grid=(n_blk,),
        in_specs=[pl.BlockSpec(memory_space=pl.ANY),
                  wspec((D, 2 * DFF)), wspec((1, DFF)), wspec((1, DFF)),
                  wspec((DFF, D)), wspec((1, D))],
        out_specs=pl.BlockSpec((FBLK, D), lambda i, *_: (i, 0)),
        scratch_shapes=[pltpu.VMEM((2, FBLK, D), BF16), pltpu.SemaphoreType.DMA((2,)),
                        pltpu.VMEM((D, DFF), BF16), pltpu.VMEM((D, DFF), BF16), pltpu.VMEM((DFF, D), BF16)],
    )
    return pl.pallas_call(
        _ffn_body,
        grid_spec=grid_spec,
        out_shape=jax.ShapeDtypeStruct((n_blk * FBLK, D), BF16),
        compiler_params=_cp(("arbitrary",), VMEM_BIG),
        name="expert_ffn",
    )(src, be, bv, bf, xs, w_gu, bgu[..., 0], bgu[..., 1], w_down, b_down.reshape(NE, 1, D))


def _combine(dst_ref, yb_hbm, ybuf, sem, lrow_ref, gate_ref):
    rows = _ChunkGather(dst_ref, NCHT, yb_hbm, ybuf, sem)
    slot = rows.arrive()
    rows.fetch_next()
    lrow = lrow_ref[...]
    gate = gate_ref[...]
    scol = lax.broadcasted_iota(jnp.int32, (TILE, SROWS), 1)
    w = jnp.zeros((TILE, SROWS), F32)
    for kk in range(TOPK):
        w = w + jnp.where(lrow[:, kk:kk + 1] == scol, gate[:, kk:kk + 1], 0.0)
    return jnp.dot(w.astype(BF16), ybuf[slot], preferred_element_type=F32), rows


def _odd_in_body(dst_ref, x_ref, yb_hbm, lrow_ref, gate_ref, mod0_ref, mod1_ref, n1_ref, wq_ref, wk_ref, wv_ref,
                 wgo_ref, wgd_ref, wgkf_ref, bgkf_ref, wgkb_ref, bgkb_ref,
                 x2_ref, q_ref, k_ref, v_ref, go_ref, gf_ref, gb_ref, ybuf, sem):
    m0 = mod0_ref[0]
    m1 = mod1_ref[0]
    y, rows = _combine(dst_ref, yb_hbm, ybuf, sem, lrow_ref, gate_ref)
    x2 = x_ref[...] + m0[5:6] * y
    x2_ref[...] = x2
    h = _modulate(x2, n1_ref[...], m1[0:1], m1[1:2]).astype(BF16)
    q_ref[...] = jnp.dot(h, wq_ref[...], preferred_element_type=F32) * (GDK ** -0.5)
    k_ref[...] = jnp.dot(h, wk_ref[...], preferred_element_type=F32)
    v_ref[...] = jnp.dot(h, wv_ref[...], preferred_element_type=F32).astype(BF16)
    go_ref[...] = jnp.dot(h, wgo_ref[...], preferred_element_type=F32).astype(BF16)
    gd = jnp.dot(h, wgd_ref[...], preferred_element_type=F32)

    gd_hi = gd.astype(BF16)
    gd_lo = (gd - gd_hi.astype(F32)).astype(BF16)

    def decay(w_ref, b_ref):
        w = w_ref[...]
        w_hi = w.astype(BF16)
        w_lo = (w - w_hi.astype(F32)).astype(BF16)
        z = (jnp.dot(gd_hi, w_hi, preferred_element_type=F32) + jnp.dot(gd_hi, w_lo, preferred_element_type=F32)
             + jnp.dot(gd_lo, w_hi, preferred_element_type=F32) + b_ref[...])
        return (jnp.minimum(z, 0.0) - jnp.log(1.0 + jnp.exp(-jnp.abs(z)))) / GNORM

    gf_ref[...] = decay(wgkf_ref, bgkf_ref)
    gb_ref[...] = decay(wgkb_ref, bgkb_ref)
    rows.drain()


def odd_in(dst, x1, yb, lrow, gate, mod0, mod1, n1g, wq, wk, wv, wgo, wgd, wgkf, bgkf, wgkb, bgkb):
    full = lambda w: pl.BlockSpec(w.shape, lambda i, d: (0,) * w.ndim)
    row = lambda w: pl.BlockSpec((TILE, w), lambda i, d: (i, 0))
    modspec = pl.BlockSpec((1, 6, D), lambda i, d: (_mod_index(i), 0, 0))
    grid_spec = pltpu.PrefetchScalarGridSpec(
        num_scalar_prefetch=1,
        grid=(NT_ALL,),
        in_specs=[row(D), pl.BlockSpec(memory_space=pl.ANY), row(LANES), row(LANES), modspec, modspec,
                  full(n1g), full(wq), full(wk), full(wv), full(wgo),
                  full(wgd), full(wgkf), full(bgkf), full(wgkb), full(bgkb)],
        out_specs=[row(D), row(GKW), row(GKW), row(GVW), row(GVW), row(GKW), row(GKW)],
        scratch_shapes=[pltpu.VMEM((2, SROWS, D), BF16), pltpu.SemaphoreType.DMA((2,))],
    )
    return pl.pallas_call(
        _odd_in_body,
        grid_spec=grid_spec,
        out_shape=[jax.ShapeDtypeStruct((T_ALL, D), F32),
                   jax.ShapeDtypeStruct((T_ALL, GKW), F32),
                   jax.ShapeDtypeStruct((T_ALL, GKW), F32),
                   jax.ShapeDtypeStruct((T_ALL, GVW), BF16),
                   jax.ShapeDtypeStruct((T_ALL, GVW), BF16),
                   jax.ShapeDtypeStruct((T_ALL, GKW), F32),
                   jax.ShapeDtypeStruct((T_ALL, GKW), F32)],
        compiler_params=_cp(("arbitrary",), VMEM_BIG),
        name="odd_in",
    )(dst, x1, yb, lrow, gate, mod0, mod1, n1g, wq, wk, wv, wgo, wgd, wgkf, bgkf, wgkb, bgkb)


def _gla_body(qf_ref, kf_ref, vf_ref, gf_ref, qb_ref, kb_ref, vb_ref, gb_ref, of_ref, ob_ref, sf_ref, sb_ref):
    @pl.when(pl.program_id(1) == 0)
    def _():
        sf_ref[...] = jnp.zeros_like(sf_ref)
        sb_ref[...] = jnp.zeros_like(sb_ref)

    r = lax.broadcasted_iota(jnp.int32, (GCHUNK, GCHUNK), 0)
    c = lax.broadcasted_iota(jnp.int32, (GCHUNK, GCHUNK), 1)
    dirs = []
    for s in range(GBS):
        at = lambda ref: ref.at[0, s]
        dirs.append((at(qf_ref), at(kf_ref), at(vf_ref), at(gf_ref), sf_ref.at[s], at(of_ref), c <= r, GCHUNK - 1))
        dirs.append((at(qb_ref), at(kb_ref), at(vb_ref), at(gb_ref), sb_ref.at[s], at(ob_ref), c >= r, 0))
    chains = []
    for q_ref, k_ref, v_ref, g_ref, s_ref, o_ref, mask, last in dirs:
        g = g_ref[...]
        g_hi = g.astype(BF16)
        g_lo = (g - g_hi.astype(F32)).astype(BF16)
        tri = mask.astype(BF16)
        cum = (jnp.dot(tri, g_hi, preferred_element_type=F32) + jnp.dot(tri, g_lo, preferred_element_type=F32))
        tot = cum[last:last + 1, :]
        q = q_ref[...]
        k = k_ref[...]
        qd = (q * jnp.exp(cum)).astype(BF16)
        kd = (k * jnp.exp(-cum)).astype(BF16)
        ke = (k * jnp.exp(tot - cum)).astype(BF16)
        dec = jnp.exp(tot)
        v = v_ref[...]
        for hd in range(GH):
            ks = slice(hd * GDK, (hd + 1) * GDK)
            vs = slice(hd * GDV, (hd + 1) * GDV)
            chains.append(dict(qd=qd[:, ks], kd=kd[:, ks], ke=ke[:, ks], dec=dec[:, ks], v=v[:, vs],
                               st=s_ref[hd], mask=mask))
    for ch in chains:
        sc = lax.dot_general(ch["qd"], ch["kd"], NT_DIMS, preferred_element_type=F32)
        ch["sc"] = jnp.where(ch["mask"], sc, 0.0).astype(BF16)
    for ch in chains:
        ch["out"] = (lax.dot_general(ch["qd"], ch["st"].astype(BF16), NT_DIMS, preferred_element_type=F32)
                     + jnp.dot(ch["sc"], ch["v"], preferred_element_type=F32))
    for ch in chains:
        ch["new"] = ch["st"] * ch["dec"] + lax.dot_general(ch["v"], ch["ke"], TN_DIMS, preferred_element_type=F32)
    for d, (_, _, _, _, s_ref, o_ref, _, _) in enumerate(dirs):
        o_ref[...] = jnp.concatenate([chains[d * GH + hd]["out"] for hd in range(GH)], axis=-1)
        for hd in range(GH):
            s_ref[hd] = chains[d * GH + hd]["new"]


def _bwd_chunk(c):
    return jnp.where(c < NCH_CTX, NCH_CTX - 1 - c, NCH_B + NCH_CTX - 1 - c)


def gla(q, k, v, gf, gb):
    fw = lambda w: pl.BlockSpec((1, GBS, GCHUNK, w), lambda b, c: (b, 0, c, 0))
    bw = lambda w: pl.BlockSpec((1, GBS, GCHUNK, w), lambda b, c: (b, 0, _bwd_chunk(c), 0))
    grouped = lambda a: a.reshape(B // GBS, GBS, ROWS_B, a.shape[-1])
    q, k, v, gf, gb = (grouped(a) for a in (q, k, v, gf, gb))
    oshape = jax.ShapeDtypeStruct((B // GBS, GBS, ROWS_B, GVW), F32)
    o_f, o_b = pl.pallas_call(
        _gla_body,
        grid=(B // GBS, NCH_B),
        in_specs=[fw(GKW), fw(GKW), fw(GVW), fw(GKW), bw(GKW), bw(GKW), bw(GVW), bw(GKW)],
        out_specs=[fw(GVW), bw(GVW)],
        out_shape=[oshape, oshape],
        scratch_shapes=[pltpu.VMEM((GBS, GH, GDV, GDK), F32), pltpu.VMEM((GBS, GH, GDV, GDK), F32)],
        compiler_params=_cp(("arbitrary", "arbitrary")),
        name="gla",
    )(q, k, v, gf, q, k, v, gb)
    return o_f.reshape(T_ALL, GVW), o_b.reshape(T_ALL, GVW)


def _out1_body(of_ref, ob_ref, go_ref, x_ref, mod_ref, gn_ref, wo_ref, n2_ref, rw_ref, rb_ref,
               x3_ref, lrow_ref, gate_ref, cnt_ref, xs_ref):
    m = mod_ref[0]
    o = of_ref[...] + ob_ref[...]
    gn = gn_ref[...]
    parts = [_rms(o[:, hd * GDV:(hd + 1) * GDV], gn) for hd in range(GH)]
    on = jnp.concatenate(parts, axis=-1) * _silu(go_ref[...].astype(F32))
    y = jnp.dot(on.astype(BF16), wo_ref[...], preferred_element_type=F32)
    x3 = x_ref[...] + m[2:3] * y
    x3_ref[...] = x3
    h2 = _modulate(x3, n2_ref[...], m[3:4], m[4:5])
    _route(h2, rw_ref, rb_ref, lrow_ref, gate_ref, cnt_ref, xs_ref)


def out1(of, ob, go, x2, mod, gng, wo, n2g, rw, rb):
    full = lambda w: pl.BlockSpec(w.shape, lambda n: (0,) * w.ndim)
    lat = lambda w: pl.BlockSpec((TILE, w), lambda n: (_lat_tile(n), 0))
    rspecs, rshapes = _route_out_specs(NT_LAT)
    return pl.pallas_call(
        _out1_body,
        grid=(NT_LAT,),
        in_specs=[lat(GVW), lat(GVW), lat(GVW), lat(D),
                  pl.BlockSpec((1, 6, D), lambda n: (n // LPB, 0, 0)),
                  full(gng), full(wo), full(n2g), full(rw), full(rb)],
        out_specs=[pl.BlockSpec((TILE, D), lambda n: (n, 0))] + rspecs,
        out_shape=[jax.ShapeDtypeStruct((T_LAT, D), F32)] + rshapes,
        compiler_params=_cp(("arbitrary",)),
        name="out1",
    )(of, ob, go, x2, mod, gng, wo, n2g, rw, rb)


def _final_body(dst_ref, x_ref, yb_hbm, lrow_ref, gate_ref, mod_ref, g_ref, o_ref, ybuf, sem):
    m = mod_ref[0]
    y, rows = _combine(dst_ref, yb_hbm, ybuf, sem, lrow_ref, gate_ref)
    o_ref[...] = _rms(x_ref[...] + m[5:6] * y, g_ref[...])
    rows.drain()


def final(dst, x3, yb, lrow, gate, mod, g):
    grid_spec = pltpu.PrefetchScalarGridSpec(
        num_scalar_prefetch=1,
        grid=(NT_LAT,),
        in_specs=[pl.BlockSpec((TILE, D), lambda n, d: (n, 0)),
                  pl.BlockSpec(memory_space=pl.ANY),
                  pl.BlockSpec((TILE, LANES), lambda n, d: (n, 0)),
                  pl.BlockSpec((TILE, LANES), lambda n, d: (n, 0)),
                  pl.BlockSpec((1, 6, D), lambda n, d: (n // LPB, 0, 0)),
                  pl.BlockSpec((1, D), lambda n, d: (0, 0))],
        out_specs=pl.BlockSpec((TILE, D), lambda n, d: (n, 0)),
        scratch_shapes=[pltpu.VMEM((2, SROWS, D), BF16), pltpu.SemaphoreType.DMA((2,))],
    )
    return pl.pallas_call(
        _final_body,
        grid_spec=grid_spec,
        out_shape=jax.ShapeDtypeStruct((T_LAT, D), F32),
        compiler_params=_cp(("arbitrary",)),
        name="final",
    )(dst, x3, yb, lrow, gate, mod, g)


def _rot_cols(w):
    a, b, c, d = jnp.split(w, 4, axis=-1)
    return jnp.concatenate([-b, a, -d, c], axis=-1)


def _rope_tables():
    n = np.arange(SEQ)
    row = (n // GRID_W).astype(np.float32)
    col = (n % GRID_W).astype(np.float32)
    axis_dim = ROPE // 2
    inv = (ROPE_BASE ** (-np.arange(0, axis_dim, 2, dtype=np.float32) / axis_dim)).astype(np.float32)
    ar = row[:, None] * inv
    ac = col[:, None] * inv
    ang = np.concatenate([ar, ar, ac, ac], axis=-1).astype(np.float32)
    cos = np.concatenate([np.ones((CTX, ROPE), np.float32), np.cos(ang)], axis=0)
    sin = np.concatenate([np.zeros((CTX, ROPE), np.float32), np.sin(ang)], axis=0)
    return jnp.asarray(np.tile(cos, (1, HEADS))), jnp.asarray(np.tile(sin, (1, HEADS)))


def _dft(n, scale):
    j = np.arange(n, dtype=np.int64)
    ang = 2.0 * np.pi * ((j[:, None] * j[None, :]) % n).astype(np.float64) / n
    return (np.cos(ang) * scale), (np.sin(ang) * scale)


def _dft_tables():
    cn, sn = _dft(SEQ, SEQ ** -0.5)
    c2, s2 = _dft(CTX, CTX ** -0.5)
    cg, sg = _dft(FGD, FGD ** -0.5)
    eye = np.eye(FG)
    to = lambda a: jnp.asarray(a.astype(np.float32)).astype(BF16)
    return to(cn), to(sn), to(c2), to(s2), to(np.kron(eye, cg)), to(np.kron(eye, sg))


def _router_pad(rw, rb):
    rwp = jnp.zeros((D, LANES), F32).at[:, :NE].set(rw)
    rbp = jnp.full((1, LANES), -1e30, F32).at[0, :NE].set(rb)
    return rwp, rbp


def kernel(x, c, ctx, c_ctx, final_norm_g, l0_mod_w, l0_mod_b, l0_norm1_g, l0_w_in, l0_q_norm_g, l0_w_q_up, l0_kv_norm_g, l0_w_kv_up, l0_w_out, l0_norm2_g, l0_router_w, l0_router_b, l0_w_gu, l0_b_gu, l0_w_down, l0_b_down, l1_mod_w, l1_mod_b, l1_norm1_g, l1_w_in, l1_w_gk_fwd, l1_b_gk_fwd, l1_w_gk_bwd, l1_b_gk_bwd, l1_gnorm_g, l1_w_out, l1_norm2_g, l1_router_w, l1_router_b, l1_w_gu, l1_b_gu, l1_w_down, l1_b_down):
    row = lambda g: g.reshape(1, -1)
    c2 = ctx.reshape(B * CTX, D)
    x2 = x.reshape(T_LAT, D)
    cc = jnp.zeros((16, D), F32).at[:B].set(c).at[B].set(c_ctx)
    mod0 = adaln(cc, l0_mod_w, l0_mod_b)
    mod1 = adaln(cc, l1_mod_w, l1_mod_b)

    w_in = l0_w_in
    wqa = w_in[:, :Q_LORA].astype(BF16)
    wkva = w_in[:, Q_LORA:Q_LORA + KV_LORA].astype(BF16)
    wkr = w_in[:, Q_LORA + KV_LORA:Q_LORA + KV_LORA + ROPE]
    wkr2 = jnp.concatenate([wkr, _rot_cols(wkr)], axis=-1).astype(BF16)
    wf = w_in[:, Q_LORA + KV_LORA + ROPE:].astype(BF16)
    wq3 = l0_w_q_up.reshape(Q_LORA, HEADS, QK)
    wq_nope = wq3[:, :, :NOPE].reshape(Q_LORA, HEADS * NOPE)
    wq_rope = wq3[:, :, NOPE:]
    wqup = jnp.concatenate([wq_nope, wq_rope.reshape(Q_LORA, HEADS * ROPE),
                            _rot_cols(wq_rope).reshape(Q_LORA, HEADS * ROPE)], axis=-1).astype(BF16)
    wkv3 = l0_w_kv_up.reshape(KV_LORA, HEADS, NOPE + VDIM)
    wkvup = jnp.concatenate([wkv3[:, :, :NOPE].reshape(KV_LORA, HEADS * NOPE),
                             wkv3[:, :, NOPE:].reshape(KV_LORA, HEADS * VDIM)], axis=-1).astype(BF16)
    cos6, sin6 = _rope_tables()
    q, k, v, uf = even_in(c2, x2, mod0, row(l0_norm1_g), wqa, wkva, wkr2, wf, row(l0_q_norm_g), wqup,
                          row(l0_kv_norm_g), wkvup, cos6, sin6)
    att = attention(q, k, v).reshape(T_ALL, HEADS * VDIM)
    fmix = fourier(uf.reshape(B, ROWS_B, FW), *_dft_tables()).reshape(T_ALL, FW)
    rw0, rb0 = _router_pad(l0_router_w, l0_router_b)
    x1, lrow0, gate0, cnt0, xs0 = out0(c2, x2, att, fmix, mod0, l0_w_out[:HEADS * VDIM].astype(BF16),
                                       l0_w_out[HEADS * VDIM:].astype(BF16), row(l0_norm2_g), rw0, rb0)
    src0, be0, bv0, dst0, nb0 = _chunk_lists(cnt0, NT_ALL)
    yb0 = expert_ffn(xs0, src0, be0, bv0, nb0, l0_w_gu, l0_b_gu, l0_w_down, l0_b_down)

    w1 = l1_w_in
    o = 0
    wq1 = w1[:, o:o + GKW].astype(BF16); o += GKW
    wk1 = w1[:, o:o + GKW].astype(BF16); o += GKW
    wv1 = w1[:, o:o + GVW].astype(BF16); o += GVW
    wgo = w1[:, o:o + GVW].astype(BF16); o += GVW
    wgd = jnp.zeros((D, LANES), F32).at[:, :2 * GRANK].set(w1[:, o:]).astype(BF16)
    wgkf = jnp.zeros((LANES, GKW), F32).at[:GRANK].set(l1_w_gk_fwd)
    wgkb = jnp.zeros((LANES, GKW), F32).at[GRANK:2 * GRANK].set(l1_w_gk_bwd)
    x2, q1, k1, v1, go, gf, gb = odd_in(dst0, x1, yb0, lrow0, gate0, mod0, mod1, row(l1_norm1_g), wq1, wk1, wv1,
                                        wgo, wgd, wgkf, row(l1_b_gk_fwd), wgkb, row(l1_b_gk_bwd))
    o_f, o_b = gla(q1, k1, v1, gf, gb)
    rw1, rb1 = _router_pad(l1_router_w, l1_router_b)
    x3, lrow1, gate1, cnt1, xs1 = out1(o_f, o_b, go, x2, mod1, row(l1_gnorm_g), l1_w_out.astype(BF16),
                                       row(l1_norm2_g), rw1, rb1)
    src1, be1, bv1, dst1, nb1 = _chunk_lists(cnt1, NT_LAT)
    yb1 = expert_ffn(xs1, src1, be1, bv1, nb1, l1_w_gu, l1_b_gu, l1_w_down, l1_b_down)
    out = final(dst1, x3, yb1, lrow1, gate1, mod1, row(final_norm_g))
    return out.reshape(B, SEQ, D)
```

```python
import functools

import jax
import jax.numpy as jnp
import numpy as np
from jax import lax
from jax.experimental import pallas as pl
from jax.experimental.pallas import tpu as pltpu

D = 1024
B = 8
SEQ = 4096
CTX = 256
GRID_W = 64
EPS = 1e-6
HEADS = 6
Q_LORA = 384
KV_LORA = 256
NOPE = 128
ROPE = 64
VDIM = 128
QK = NOPE + ROPE
ROPE_BASE = 10000.0
FG = 4
FGD = 64
FW = FG * FGD
GH = 4
GDK = 128
GDV = 256
GRANK = 16
GNORM = 16.0
GCHUNK = 64
GKW = GH * GDK
GVW = GH * GDV
NE = 32
TOPK = 4
DFF = 1024
ALPHA = 1.702
LIMIT = 7.0

LANES = 128
TILE = 256
ROWS_B = CTX + SEQ
TPB = ROWS_B // TILE
T_ALL = B * ROWS_B
NT_ALL = T_ALL // TILE
T_LAT = B * SEQ
NT_LAT = T_LAT // TILE
LPB = SEQ // TILE
HPS = 3
CH = 16
MAXCH = (TILE * TOPK + NE * (CH - 1)) // CH
NCHT = 96
SROWS = NCHT * CH
FBLK = 512
CPB = FBLK // CH
GBS = 4
NCH_B = ROWS_B // GCHUNK
NCH_CTX = CTX // GCHUNK
VMEM_BIG = 56 * 1024 * 1024
VMEM_MID = 40 * 1024 * 1024

F32 = jnp.float32
BF16 = jnp.bfloat16
HI = lax.Precision.HIGHEST
NT_DIMS = (((1,), (1,)), ((), ()))
TN_DIMS = (((0,), (0,)), ((), ()))


def _cp(sem, vmem=VMEM_MID):
    return pltpu.CompilerParams(dimension_semantics=sem, vmem_limit_bytes=vmem)


def _rms(x, g):
    return x * lax.rsqrt(jnp.mean(x * x, axis=-1, keepdims=True) + EPS) * g


def _modulate(x, g, sh, sc):
    return _rms(x, g) * (1.0 + sc) + sh


def _silu(x):
    return x * jax.nn.sigmoid(x)


def _mod_index(i):
    return jnp.where(i % TPB == 0, B, i // TPB)


def _lat_tile(n):
    return n + n // LPB + 1


def _adaln_body(c_ref, w_ref, b_ref, o_ref):
    s = _silu(c_ref[...])
    o_ref[...] = jnp.dot(s, w_ref[...], precision=HI, preferred_element_type=F32) + b_ref[...]


def adaln(cc, mod_w, mod_b):
    tn = 1536
    out = pl.pallas_call(
        _adaln_body,
        grid=(6 * D // tn,),
        in_specs=[pl.BlockSpec((16, D), lambda j: (0, 0)),
                  pl.BlockSpec((D, tn), lambda j: (0, j)),
                  pl.BlockSpec((1, tn), lambda j: (0, j))],
        out_specs=pl.BlockSpec((16, tn), lambda j: (0, j)),
        out_shape=jax.ShapeDtypeStruct((16, 6 * D), F32),
        compiler_params=_cp(("arbitrary",)),
        name="adaln",
    )(cc, mod_w, mod_b.reshape(1, 6 * D))
    return out.reshape(16, 6, D)


def _stream_tile(c_ref, x_ref):
    return jnp.where(pl.program_id(0) % TPB == 0, c_ref[...], x_ref[...])


def _stream_specs():
    return [pl.BlockSpec((TILE, D), lambda i: (i // TPB, 0)),
            pl.BlockSpec((TILE, D), lambda i: ((i // TPB) * LPB + jnp.maximum(i % TPB - 1, 0), 0))]


def _even_in_body(c_ref, x_ref, mod_ref, n1_ref, wqa_ref, wkva_ref, wkr_ref, wf_ref, qg_ref, wqup_ref,
                  kvg_ref, wkvup_ref, cos_ref, sin_ref, q_ref, k_ref, v_ref, uf_ref):
    m = mod_ref[0]
    h = _modulate(_stream_tile(c_ref, x_ref), n1_ref[...], m[0:1], m[1:2]).astype(BF16)
    cq = jnp.dot(h, wqa_ref[...], preferred_element_type=F32)
    ckv = jnp.dot(h, wkva_ref[...], preferred_element_type=F32)
    kr2 = jnp.dot(h, wkr_ref[...], preferred_element_type=F32)
    uf_ref[...] = jnp.dot(h, wf_ref[...], preferred_element_type=F32).astype(BF16)
    cos6 = cos_ref[...]
    sin6 = sin_ref[...]
    scale = QK ** -0.5
    qall = jnp.dot(_rms(cq, qg_ref[...]).astype(BF16), wqup_ref[...], preferred_element_type=F32)
    qn = qall[:, :HEADS * NOPE] * scale
    qr = (qall[:, HEADS * NOPE:HEADS * QK] * cos6 + qall[:, HEADS * QK:] * sin6) * scale
    kvall = jnp.dot(_rms(ckv, kvg_ref[...]).astype(BF16), wkvup_ref[...], preferred_element_type=F32)
    kr = (kr2[:, :ROPE] * cos6[:, :ROPE] + kr2[:, ROPE:] * sin6[:, :ROPE]).astype(BF16)
    for hd in range(HEADS):
        q_ref[0, hd, :, 0:NOPE] = qn[:, hd * NOPE:(hd + 1) * NOPE].astype(BF16)
        q_ref[0, hd, :, NOPE:QK] = qr[:, hd * ROPE:(hd + 1) * ROPE].astype(BF16)
        k_ref[0, hd, :, 0:NOPE] = kvall[:, hd * NOPE:(hd + 1) * NOPE].astype(BF16)
        k_ref[0, hd, :, NOPE:QK] = kr
        v_ref[0, hd] = kvall[:, HEADS * NOPE + hd * VDIM:HEADS * NOPE + (hd + 1) * VDIM].astype(BF16)


def even_in(c2, x2, mod, n1g, wqa, wkva, wkr2, wf, qg, wqup, kvg, wkvup, cos6, sin6):
    full = lambda a: pl.BlockSpec(a.shape, lambda i: (0,) * a.ndim)
    hs = lambda w: pl.BlockSpec((1, HEADS, TILE, w), lambda i: (i // TPB, 0, i % TPB, 0))
    return pl.pallas_call(
        _even_in_body,
        grid=(NT_ALL,),
        in_specs=_stream_specs() + [
                  pl.BlockSpec((1, 6, D), lambda i: (_mod_index(i), 0, 0)),
                  full(n1g), full(wqa), full(wkva), full(wkr2), full(wf), full(qg), full(wqup),
                  full(kvg), full(wkvup),
                  pl.BlockSpec((TILE, HEADS * ROPE), lambda i: (i % TPB, 0)),
                  pl.BlockSpec((TILE, HEADS * ROPE), lambda i: (i % TPB, 0))],
        out_specs=[hs(QK), hs(QK), hs(VDIM), pl.BlockSpec((TILE, FW), lambda i: (i, 0))],
        out_shape=[jax.ShapeDtypeStruct((B, HEADS, ROWS_B, QK), BF16),
                   jax.ShapeDtypeStruct((B, HEADS, ROWS_B, QK), BF16),
                   jax.ShapeDtypeStruct((B, HEADS, ROWS_B, VDIM), BF16),
                   jax.ShapeDtypeStruct((T_ALL, FW), BF16)],
        compiler_params=_cp(("arbitrary",)),
        name="even_in",
    )(c2, x2, mod, n1g, wqa, wkva, wkr2, wf, qg, wqup, kvg, wkvup, cos6, sin6)


def _attend(q_ref, k_ref, v_ref, o_ref, rows):
    ks = [k_ref[0, j, 0:rows, :] for j in range(HPS)]
    ss = [lax.dot_general(q_ref[0, j], ks[j], NT_DIMS, preferred_element_type=F32) for j in range(HPS)]
    ps, ls = [], []
    for s in ss:
        p = jnp.exp(s - jnp.max(s, axis=-1, keepdims=True))
        ls.append(jnp.sum(p, axis=-1, keepdims=True))
        ps.append(p.astype(BF16))
    os = [jnp.dot(ps[j], v_ref[0, j, 0:rows, :], preferred_element_type=F32) / ls[j] for j in range(HPS)]
    o_ref[0] = jnp.concatenate(os, axis=-1).astype(BF16)


def _attn_body(q_ref, k_ref, v_ref, o_ref):
    qt = pl.program_id(2)

    @pl.when(qt == 0)
    def _():
        _attend(q_ref, k_ref, v_ref, o_ref, CTX)

    @pl.when(qt > 0)
    def _():
        _attend(q_ref, k_ref, v_ref, o_ref, ROWS_B)


def attention(q, k, v):
    return pl.pallas_call(
        _attn_body,
        grid=(B, HEADS // HPS, TPB),
        in_specs=[pl.BlockSpec((1, HPS, TILE, QK), lambda b, h, t: (b, h, t, 0)),
                  pl.BlockSpec((1, HPS, ROWS_B, QK), lambda b, h, t: (b, h, 0, 0)),
                  pl.BlockSpec((1, HPS, ROWS_B, VDIM), lambda b, h, t: (b, h, 0, 0))],
        out_specs=pl.BlockSpec((1, TILE, HPS * VDIM), lambda b, h, t: (b, t, h)),
        out_shape=jax.ShapeDtypeStruct((B, ROWS_B, HEADS * VDIM), BF16),
        compiler_params=_cp(("arbitrary", "arbitrary", "arbitrary"), VMEM_BIG),
        name="attention",
    )(q, k, v)


def _fourier_body(u_ref, cn_ref, sn_ref, c2_ref, s2_ref, cc_ref, sc_ref, o_ref):
    i = pl.program_id(0)

    def finish(a, b):
        o_ref[0] = (jnp.dot(a.astype(BF16), cc_ref[...], preferred_element_type=F32)
                    - jnp.dot(b.astype(BF16), sc_ref[...], preferred_element_type=F32)).astype(BF16)

    @pl.when(i == 0)
    def _():
        u = u_ref[0, 0:CTX, :]
        finish(jnp.dot(c2_ref[...], u, preferred_element_type=F32),
               jnp.dot(s2_ref[...], u, preferred_element_type=F32))

    @pl.when(i > 0)
    def _():
        u = u_ref[0, CTX:ROWS_B, :]
        finish(jnp.dot(cn_ref[...], u, preferred_element_type=F32),
               jnp.dot(sn_ref[...], u, preferred_element_type=F32))


def fourier(uf, cn, sn, c2, s2, ccb, scb):
    full = lambda a: pl.BlockSpec(a.shape, lambda i, b: (0,) * a.ndim)
    return pl.pallas_call(
        _fourier_body,
        grid=(TPB, B),
        in_specs=[pl.BlockSpec((1, ROWS_B, FW), lambda i, b: (b, 0, 0)),
                  pl.BlockSpec((TILE, SEQ), lambda i, b: (jnp.maximum(i - 1, 0), 0)),
                  pl.BlockSpec((TILE, SEQ), lambda i, b: (jnp.maximum(i - 1, 0), 0)),
                  full(c2), full(s2), full(ccb), full(scb)],
        out_specs=pl.BlockSpec((1, TILE, FW), lambda i, b: (b, i, 0)),
        out_shape=jax.ShapeDtypeStruct((B, ROWS_B, FW), BF16),
        compiler_params=_cp(("arbitrary", "arbitrary")),
        name="fourier",
    )(uf, cn, sn, c2, s2, ccb, scb)


def _route(h2, rw_ref, rb_ref, lrow_ref, gate_ref, cnt_ref, xs_ref):
    h_hi = h2.astype(BF16)
    h_lo = (h2 - h_hi.astype(F32)).astype(BF16)
    rw = rw_ref[...]
    w_hi = rw.astype(BF16)
    w_lo = (rw - w_hi.astype(F32)).astype(BF16)
    both = lax.dot_general(jnp.concatenate([w_hi, w_lo], axis=0), h_hi, NT_DIMS, preferred_element_type=F32)
    l = (both[:LANES] + both[LANES:]
         + lax.dot_general(w_hi, h_lo, NT_DIMS, preferred_element_type=F32) + rb_ref[...])
    sub = lax.broadcasted_iota(jnp.int32, l.shape, 0)
    vals, idxs = [], []
    for _ in range(TOPK):
        mx = jnp.max(l, axis=0, keepdims=True)
        am = jnp.min(jnp.where(l == mx, sub, LANES), axis=0, keepdims=True)
        vals.append(mx)
        idxs.append(am)
        l = jnp.where(sub == am, -jnp.inf, l)
    ex = [jnp.exp(vv - vals[0]) for vv in vals]
    den = ex[0] + ex[1] + ex[2] + ex[3]
    onehot = jnp.zeros(l.shape, F32)
    for am in idxs:
        onehot = onehot + (sub == am).astype(F32)
    r = lax.broadcasted_iota(jnp.int32, (TILE, TILE), 0)
    c = lax.broadcasted_iota(jnp.int32, (TILE, TILE), 1)
    rank = jnp.dot(onehot.astype(BF16), (r < c).astype(BF16), preferred_element_type=F32)
    cnt = jnp.sum(onehot, axis=1, keepdims=True)
    nch = jnp.ceil(cnt * (1.0 / CH))
    er = lax.broadcasted_iota(jnp.int32, (LANES, LANES), 0)
    ec = lax.broadcasted_iota(jnp.int32, (LANES, LANES), 1)
    seg = jnp.dot((ec < er).astype(BF16), jnp.broadcast_to(nch, (LANES, LANES)).astype(BF16),
                  preferred_element_type=F32)[:, 0:1] * CH
    dest = rank + seg
    lrow_t = jnp.zeros(l.shape, F32)
    gate_t = jnp.zeros(l.shape, F32)
    rows = []
    for kk in range(TOPK):
        pos = jnp.sum(jnp.where(sub == idxs[kk], dest, 0.0), axis=0, keepdims=True)
        rows.append(pos.astype(jnp.int32))
        lrow_t = jnp.where(sub == kk, pos, lrow_t)
        gate_t = jnp.where(sub == kk, ex[kk] / den, gate_t)
    lrow_ref[...] = lrow_t.T.astype(jnp.int32)
    gate_ref[...] = gate_t.T
    cnt_ref[0] = jnp.broadcast_to(cnt, (LANES, LANES)).T[0:8, :]
    srow = lax.broadcasted_iota(jnp.int32, (SROWS, TILE), 0)
    hit = rows[0] == srow
    for kk in range(1, TOPK):
        hit = jnp.logical_or(hit, rows[kk] == srow)
    xs_ref[...] = jnp.dot(hit.astype(BF16), h_hi, preferred_element_type=F32).astype(BF16)


def _route_out_specs(n_tiles):
    specs = [pl.BlockSpec((TILE, LANES), lambda i: (i, 0)),
             pl.BlockSpec((TILE, LANES), lambda i: (i, 0)),
             pl.BlockSpec((1, 8, LANES), lambda i: (i, 0, 0)),
             pl.BlockSpec((SROWS, D), lambda i: (i, 0))]
    shapes = [jax.ShapeDtypeStruct((n_tiles * TILE, LANES), jnp.int32),
              jax.ShapeDtypeStruct((n_tiles * TILE, LANES), F32),
              jax.ShapeDtypeStruct((n_tiles, 8, LANES), F32),
              jax.ShapeDtypeStruct((n_tiles * SROWS, D), BF16)]
    return specs, shapes


def _out0_body(c_ref, x_ref, a_ref, f_ref, mod_ref, woa_ref, wof_ref, n2_ref, rw_ref, rb_ref,
               x1_ref, lrow_ref, gate_ref, cnt_ref, xs_ref):
    m = mod_ref[0]
    y = (jnp.dot(a_ref[...], woa_ref[...], preferred_element_type=F32)
         + jnp.dot(f_ref[...], wof_ref[...], preferred_element_type=F32))
    x1 = _stream_tile(c_ref, x_ref) + m[2:3] * y
    x1_ref[...] = x1
    h2 = _modulate(x1, n2_ref[...], m[3:4], m[4:5])
    _route(h2, rw_ref, rb_ref, lrow_ref, gate_ref, cnt_ref, xs_ref)


def out0(c2, x2, a, f, mod, woa, wof, n2g, rw, rb):
    full = lambda w: pl.BlockSpec(w.shape, lambda i: (0,) * w.ndim)
    rspecs, rshapes = _route_out_specs(NT_ALL)
    return pl.pallas_call(
        _out0_body,
        grid=(NT_ALL,),
        in_specs=_stream_specs() + [
                  pl.BlockSpec((TILE, HEADS * VDIM), lambda i: (i, 0)),
                  pl.BlockSpec((TILE, FW), lambda i: (i, 0)),
                  pl.BlockSpec((1, 6, D), lambda i: (_mod_index(i), 0, 0)),
                  full(woa), full(wof), full(n2g), full(rw), full(rb)],
        out_specs=[pl.BlockSpec((TILE, D), lambda i: (i, 0))] + rspecs,
        out_shape=[jax.ShapeDtypeStruct((T_ALL, D), F32)] + rshapes,
        compiler_params=_cp(("arbitrary",)),
        name="out0",
    )(c2, x2, a, f, mod, woa, wof, n2g, rw, rb)


def _chunk_lists(cnt, n_tiles):
    n_blk = (n_tiles * MAXCH + CPB - 1) // CPB + NE
    cnt = cnt[:, 0, :NE].astype(jnp.int32)
    nch = (cnt + CH - 1) // CH
    lend = jnp.cumsum(nch, axis=1)
    loff = lend - nch
    tend = jnp.cumsum(nch, axis=0)
    toff = tend - nch
    tot = tend[-1]
    blk = (tot + CPB - 1) // CPB
    bend = jnp.cumsum(blk)
    estart = (bend - blk) * CPB
    nvalid = bend[-1]
    bi = jnp.arange(n_blk, dtype=jnp.int32)
    bvalid = bi < nvalid
    count_le = lambda ends, v: jnp.sum((ends[None, :] <= v[:, None]).astype(jnp.int32), axis=1)
    be = jnp.where(bvalid, jnp.minimum(count_le(bend, bi), NE - 1), jnp.sum((bend < nvalid).astype(jnp.int32)))
    g = jnp.arange(n_blk * CPB, dtype=jnp.int32)
    sel = (jnp.repeat(be, CPB)[:, None] == jnp.arange(NE, dtype=jnp.int32)[None, :]).astype(F32)
    pick = lambda table: jnp.dot(sel, table.astype(F32), precision=HI).astype(jnp.int32)
    o = g - pick(estart[:, None])[:, 0]
    gvalid = jnp.logical_and(jnp.repeat(bvalid, CPB), o < pick(tot[:, None])[:, 0])
    tend_g = pick(tend.T)
    tile = jnp.minimum(jnp.sum((tend_g <= o[:, None]).astype(jnp.int32), axis=1), n_tiles - 1)
    tsel = tile[:, None] == jnp.arange(n_tiles, dtype=jnp.int32)[None, :]
    shift = jnp.sum(jnp.where(tsel, pick((loff - toff).T), 0), axis=1)
    src = jnp.where(gvalid, tile * NCHT + o + shift, 0).astype(jnp.int32)
    cslot = jnp.arange(NCHT, dtype=jnp.int32)
    inside = jnp.logical_and(loff[:, None, :] <= cslot[None, :, None], cslot[None, :, None] < lend[:, None, :])
    base = estart[None, :] + toff - loff
    dst = cslot[None, :] + jnp.sum(jnp.where(inside, base[:, None, :], 0), axis=2)
    dst = jnp.where(cslot[None, :] < lend[:, -1:], dst, 0).astype(jnp.int32)
    src = jnp.concatenate([src, jnp.zeros((CPB,), jnp.int32)])
    dst = jnp.concatenate([dst.reshape(-1), jnp.zeros((NCHT,), jnp.int32)])
    return src, be.astype(jnp.int32), bvalid.astype(jnp.int32), dst, n_blk


def _chunk_copy(src_hbm, chunk, buf, slot, j, sem):
    start = chunk * CH if isinstance(chunk, int) else pl.multiple_of(chunk * CH, CH)
    return pltpu.make_async_copy(src_hbm.at[pl.ds(start, CH), :],
                                 buf.at[slot, pl.ds(j * CH, CH), :], sem.at[slot])


class _ChunkGather:
    def __init__(self, list_ref, per_step, src_hbm, buf, sem):
        self.list_ref, self.per_step, self.src_hbm, self.buf, self.sem = list_ref, per_step, src_hbm, buf, sem
        self.step = pl.program_id(0)
        self.n_steps = pl.num_programs(0)

    def _issue(self, st, slot):
        for j in range(self.per_step):
            _chunk_copy(self.src_hbm, self.list_ref[st * self.per_step + j], self.buf, slot, j, self.sem).start()

    def _wait(self, slot):
        for j in range(self.per_step):
            _chunk_copy(self.src_hbm, 0, self.buf, slot, j, self.sem).wait()

    def arrive(self):
        @pl.when(self.step == 0)
        def _():
            self._issue(0, 0)

        slot = self.step % 2
        self._wait(slot)
        return slot

    def fetch_next(self):
        self._issue(self.step + 1, (self.step + 1) % 2)

    def drain(self):
        @pl.when(self.step == self.n_steps - 1)
        def _():
            self._wait((self.step + 1) % 2)


def _ffn_body(src_ref, be_ref, bv_ref, bf_ref, xs_hbm, wgu_ref, bg_ref, bl_ref, wdn_ref, bd_ref,
              yb_ref, xbuf, sem, wg_s, wl_s, wd_s):
    i = pl.program_id(0)
    rows = _ChunkGather(src_ref, CPB, xs_hbm, xbuf, sem)
    slot = rows.arrive()

    @pl.when(bf_ref[i] == 1)
    def _():
        r = lax.broadcasted_iota(jnp.int32, (2 * LANES, 2 * LANES), 0)
        c = lax.broadcasted_iota(jnp.int32, (2 * LANES, 2 * LANES), 1)
        split = (r == 2 * (c % LANES) + c // LANES).astype(BF16)
        for g in range(DFF // LANES):
            pair = wgu_ref[0, :, g * 2 * LANES:(g + 1) * 2 * LANES].astype(BF16)
            sep = jnp.dot(pair, split, preferred_element_type=F32)
            wg_s[:, g * LANES:(g + 1) * LANES] = sep[:, :LANES].astype(BF16)
            wl_s[:, g * LANES:(g + 1) * LANES] = sep[:, LANES:].astype(BF16)
        wd_s[...] = wdn_ref[0].astype(BF16)

    @pl.when(bv_ref[i] == 1)
    def _():
        rows.fetch_next()
        xb = xbuf[slot]
        glu = jnp.dot(xb, wg_s[...], preferred_element_type=F32) + bg_ref[0]
        lin = jnp.dot(xb, wl_s[...], preferred_element_type=F32) + bl_ref[0]
        glu = jnp.minimum(glu, LIMIT)
        lin = jnp.clip(lin, -LIMIT, LIMIT)
        act = glu * jax.nn.sigmoid(ALPHA * glu) * (lin + 1.0)
        out = jnp.dot(act.astype(BF16), wd_s[...], preferred_element_type=F32) + bd_ref[0]
        yb_ref[...] = out.astype(BF16)

    @pl.when(bv_ref[i] == 0)
    def _():
        rows.fetch_next()
        yb_ref[...] = jnp.zeros_like(yb_ref)

    rows.drain()


def expert_ffn(xs, src, be, bv, n_blk, w_gu, b_gu, w_down, b_down):
    bf = jnp.concatenate([jnp.ones((1,), jnp.int32), (be[1:] != be[:-1]).astype(jnp.int32)])
    bgu = b_gu.reshape(NE, 1, DFF, 2)
    wspec = lambda shp: pl.BlockSpec((1,) + shp, lambda i, src, be, bv, bf: (be[i], 0, 0))
    grid_spec = pltpu.PrefetchScalarGridSpec(
        num_scalar_prefetch=4,
        grid=(n_blk,),
        in_specs=[pl.BlockSpec(memory_space=pl.ANY),
                  wspec((D, 2 * DFF)), wspec((1, DFF)), wspec((1, DFF)),
                  wspec((DFF, D)), wspec((1, D))],
        out_specs=pl.BlockSpec((FBLK, D), lambda i, *_: (i, 0)),
        scratch_shapes=[pltpu.VMEM((2, FBLK, D), BF16), pltpu.SemaphoreType.DMA((2,)),
                        pltpu.VMEM((D, DFF), BF16), pltpu.VMEM((D, DFF), BF16), pltpu.VMEM((DFF, D), BF16)],
    )
    return pl.pallas_call(
        _ffn_body,
        grid_spec=grid_spec,
        out_shape=jax.ShapeDtypeStruct((n_blk * FBLK, D), BF16),
        compiler_params=_cp(("arbitrary",), VMEM_BIG),
        name="expert_ffn",
    )(src, be, bv, bf, xs, w_gu, bgu[..., 0], bgu[..., 1], w_down, b_down.reshape(NE, 1, D))


def _combine(dst_ref, yb_hbm, ybuf, sem, lrow_ref, gate_ref):
    rows = _ChunkGather(dst_ref, NCHT, yb_hbm, ybuf, sem)
    slot = rows.arrive()
    rows.fetch_next()
    lrow = lrow_ref[...]
    gate = gate_ref[...]
    scol = lax.broadcasted_iota(jnp.int32, (TILE, SROWS), 1)
    w = jnp.zeros((TILE, SROWS), F32)
    for kk in range(TOPK):
        w = w + jnp.where(lrow[:, kk:kk + 1] == scol, gate[:, kk:kk + 1], 0.0)
    return jnp.dot(w.astype(BF16), ybuf[slot], preferred_element_type=F32), rows


def _odd_in_body(dst_ref, x_ref, yb_hbm, lrow_ref, gate_ref, mod0_ref, mod1_ref, n1_ref, wq_ref, wk_ref, wv_ref,
                 wgo_ref, wgd_ref, wgkf_ref, bgkf_ref, wgkb_ref, bgkb_ref,
                 x2_ref, q_ref, k_ref, v_ref, go_ref, gf_ref, gb_ref, ybuf, sem):
    m0 = mod0_ref[0]
    m1 = mod1_ref[0]
    y, rows = _combine(dst_ref, yb_hbm, ybuf, sem, lrow_ref, gate_ref)
    x2 = x_ref[...] + m0[5:6] * y
    x2_ref[...] = x2
    h = _modulate(x2, n1_ref[...], m1[0:1], m1[1:2]).astype(BF16)
    q_ref[...] = jnp.dot(h, wq_ref[...], preferred_element_type=F32) * (GDK ** -0.5)
    k_ref[...] = jnp.dot(h, wk_ref[...], preferred_element_type=F32)
    v_ref[...] = jnp.dot(h, wv_ref[...], preferred_element_type=F32).astype(BF16)
    go_ref[...] = jnp.dot(h, wgo_ref[...], preferred_element_type=F32).astype(BF16)
    gd = jnp.dot(h, wgd_ref[...], preferred_element_type=F32)

    gd_hi = gd.astype(BF16)
    gd_lo = (gd - gd_hi.astype(F32)).astype(BF16)

    def decay(w_ref, b_ref):
        w = w_ref[...]
        w_hi = w.astype(BF16)
        w_lo = (w - w_hi.astype(F32)).astype(BF16)
        z = (jnp.dot(gd_hi, w_hi, preferred_element_type=F32) + jnp.dot(gd_hi, w_lo, preferred_element_type=F32)
             + jnp.dot(gd_lo, w_hi, preferred_element_type=F32) + b_ref[...])
        return (jnp.minimum(z, 0.0) - jnp.log(1.0 + jnp.exp(-jnp.abs(z)))) / GNORM

    gf_ref[...] = decay(wgkf_ref, bgkf_ref)
    gb_ref[...] = decay(wgkb_ref, bgkb_ref)
    rows.drain()


def odd_in(dst, x1, yb, lrow, gate, mod0, mod1, n1g, wq, wk, wv, wgo, wgd, wgkf, bgkf, wgkb, bgkb):
    full = lambda w: pl.BlockSpec(w.shape, lambda i, d: (0,) * w.ndim)
    row = lambda w: pl.BlockSpec((TILE, w), lambda i, d: (i, 0))
    modspec = pl.BlockSpec((1, 6, D), lambda i, d: (_mod_index(i), 0, 0))
    grid_spec = pltpu.PrefetchScalarGridSpec(
        num_scalar_prefetch=1,
        grid=(NT_ALL,),
        in_specs=[row(D), pl.BlockSpec(memory_space=pl.ANY), row(LANES), row(LANES), modspec, modspec,
                  full(n1g), full(wq), full(wk), full(wv), full(wgo),
                  full(wgd), full(wgkf), full(bgkf), full(wgkb), full(bgkb)],
        out_specs=[row(D), row(GKW), row(GKW), row(GVW), row(GVW), row(GKW), row(GKW)],
        scratch_shapes=[pltpu.VMEM((2, SROWS, D), BF16), pltpu.SemaphoreType.DMA((2,))],
    )
    return pl.pallas_call(
        _odd_in_body,
        grid_spec=grid_spec,
        out_shape=[jax.ShapeDtypeStruct((T_ALL, D), F32),
                   jax.ShapeDtypeStruct((T_ALL, GKW), F32),
                   jax.ShapeDtypeStruct((T_ALL, GKW), F32),
                   jax.ShapeDtypeStruct((T_ALL, GVW), BF16),
                   jax.ShapeDtypeStruct((T_ALL, GVW), BF16),
                   jax.ShapeDtypeStruct((T_ALL, GKW), F32),
                   jax.ShapeDtypeStruct((T_ALL, GKW), F32)],
        compiler_params=_cp(("arbitrary",), VMEM_BIG),
        name="odd_in",
    )(dst, x1, yb, lrow, gate, mod0, mod1, n1g, wq, wk, wv, wgo, wgd, wgkf, bgkf, wgkb, bgkb)


def _gla_body(qf_ref, kf_ref, vf_ref, gf_ref, qb_ref, kb_ref, vb_ref, gb_ref, of_ref, ob_ref, sf_ref, sb_ref):
    @pl.when(pl.program_id(1) == 0)
    def _():
        sf_ref[...] = jnp.zeros_like(sf_ref)
        sb_ref[...] = jnp.zeros_like(sb_ref)

    r = lax.broadcasted_iota(jnp.int32, (GCHUNK, GCHUNK), 0)
    c = lax.broadcasted_iota(jnp.int32, (GCHUNK, GCHUNK), 1)
    dirs = []
    for s in range(GBS):
        at = lambda ref: ref.at[0, s]
        dirs.append((at(qf_ref), at(kf_ref), at(vf_ref), at(gf_ref), sf_ref.at[s], at(of_ref), c <= r, GCHUNK - 1))
        dirs.append((at(qb_ref), at(kb_ref), at(vb_ref), at(gb_ref), sb_ref.at[s], at(ob_ref), c >= r, 0))
    chains = []
    for q_ref, k_ref, v_ref, g_ref, s_ref, o_ref, mask, last in dirs:
        g = g_ref[...]
        g_hi = g.astype(BF16)
        g_lo = (g - g_hi.astype(F32)).astype(BF16)
        tri = mask.astype(BF16)
        cum = (jnp.dot(tri, g_hi, preferred_element_type=F32) + jnp.dot(tri, g_lo, preferred_element_type=F32))
        tot = cum[last:last + 1, :]
        q = q_ref[...]
        k = k_ref[...]
        qd = (q * jnp.exp(cum)).astype(BF16)
        kd = (k * jnp.exp(-cum)).astype(BF16)
        ke = (k * jnp.exp(tot - cum)).astype(BF16)
        dec = jnp.exp(tot)
        v = v_ref[...]
        for hd in range(GH):
            ks = slice(hd * GDK, (hd + 1) * GDK)
            vs = slice(hd * GDV, (hd + 1) * GDV)
            chains.append(dict(qd=qd[:, ks], kd=kd[:, ks], ke=ke[:, ks], dec=dec[:, ks], v=v[:, vs],
                               st=s_ref[hd], mask=mask))
    for ch in chains:
        sc = lax.dot_general(ch["qd"], ch["kd"], NT_DIMS, preferred_element_type=F32)
        ch["sc"] = jnp.where(ch["mask"], sc, 0.0).astype(BF16)
    for ch in chains:
        ch["out"] = (lax.dot_general(ch["qd"], ch["st"].astype(BF16), NT_DIMS, preferred_element_type=F32)
                     + jnp.dot(ch["sc"], ch["v"], preferred_element_type=F32))
    for ch in chains:
        ch["new"] = ch["st"] * ch["dec"] + lax.dot_general(ch["v"], ch["ke"], TN_DIMS, preferred_element_type=F32)
    for d, (_, _, _, _, s_ref, o_ref, _, _) in enumerate(dirs):
        o_ref[...] = jnp.concatenate([chains[d * GH + hd]["out"] for hd in range(GH)], axis=-1)
        for hd in range(GH):
            s_ref[hd] = chains[d * GH + hd]["new"]


def _bwd_chunk(c):
    return jnp.where(c < NCH_CTX, NCH_CTX - 1 - c, NCH_B + NCH_CTX - 1 - c)


def gla(q, k, v, gf, gb):
    fw = lambda w: pl.BlockSpec((1, GBS, GCHUNK, w), lambda b, c: (b, 0, c, 0))
    bw = lambda w: pl.BlockSpec((1, GBS, GCHUNK, w), lambda b, c: (b, 0, _bwd_chunk(c), 0))
    grouped = lambda a: a.reshape(B // GBS, GBS, ROWS_B, a.shape[-1])
    q, k, v, gf, gb = (grouped(a) for a in (q, k, v, gf, gb))
    oshape = jax.ShapeDtypeStruct((B // GBS, GBS, ROWS_B, GVW), F32)
    o_f, o_b = pl.pallas_call(
        _gla_body,
        grid=(B // GBS, NCH_B),
        in_specs=[fw(GKW), fw(GKW), fw(GVW), fw(GKW), bw(GKW), bw(GKW), bw(GVW), bw(GKW)],
        out_specs=[fw(GVW), bw(GVW)],
        out_shape=[oshape, oshape],
        scratch_shapes=[pltpu.VMEM((GBS, GH, GDV, GDK), F32), pltpu.VMEM((GBS, GH, GDV, GDK), F32)],
        compiler_params=_cp(("arbitrary", "arbitrary")),
        name="gla",
    )(q, k, v, gf, q, k, v, gb)
    return o_f.reshape(T_ALL, GVW), o_b.reshape(T_ALL, GVW)


def _out1_body(of_ref, ob_ref, go_ref, x_ref, mod_ref, gn_ref, wo_ref, n2_ref, rw_ref, rb_ref,
               x3_ref, lrow_ref, gate_ref, cnt_ref, xs_ref):
    m = mod_ref[0]
    o = of_ref[...] + ob_ref[...]
    gn = gn_ref[...]
    parts = [_rms(o[:, hd * GDV:(hd + 1) * GDV], gn) for hd in range(GH)]
    on = jnp.concatenate(parts, axis=-1) * _silu(go_ref[...].astype(F32))
    y = jnp.dot(on.astype(BF16), wo_ref[...], preferred_element_type=F32)
    x3 = x_ref[...] + m[2:3] * y
    x3_ref[...] = x3
    h2 = _modulate(x3, n2_ref[...], m[3:4], m[4:5])
    _route(h2, rw_ref, rb_ref, lrow_ref, gate_ref, cnt_ref, xs_ref)


def out1(of, ob, go, x2, mod, gng, wo, n2g, rw, rb):
    full = lambda w: pl.BlockSpec(w.shape, lambda n: (0,) * w.ndim)
    lat = lambda w: pl.BlockSpec((TILE, w), lambda n: (_lat_tile(n), 0))
    rspecs, rshapes = _route_out_specs(NT_LAT)
    return pl.pallas_call(
        _out1_body,
        grid=(NT_LAT,),
        in_specs=[lat(GVW), lat(GVW), lat(GVW), lat(D),
                  pl.BlockSpec((1, 6, D), lambda n: (n // LPB, 0, 0)),
                  full(gng), full(wo), full(n2g), full(rw), full(rb)],
        out_specs=[pl.BlockSpec((TILE, D), lambda n: (n, 0))] + rspecs,
        out_shape=[jax.ShapeDtypeStruct((T_LAT, D), F32)] + rshapes,
        compiler_params=_cp(("arbitrary",)),
        name="out1",
    )(of, ob, go, x2, mod, gng, wo, n2g, rw, rb)


def _final_body(dst_ref, x_ref, yb_hbm, lrow_ref, gate_ref, mod_ref, g_ref, o_ref, ybuf, sem):
    m = mod_ref[0]
    y, rows = _combine(dst_ref, yb_hbm, ybuf, sem, lrow_ref, gate_ref)
    o_ref[...] = _rms(x_ref[...] + m[5:6] * y, g_ref[...])
    rows.drain()


def final(dst, x3, yb, lrow, gate, mod, g):
    grid_spec = pltpu.PrefetchScalarGridSpec(
        num_scalar_prefetch=1,
        grid=(NT_LAT,),
        in_specs=[pl.BlockSpec((TILE, D), lambda n, d: (n, 0)),
                  pl.BlockSpec(memory_space=pl.ANY),
                  pl.BlockSpec((TILE, LANES), lambda n, d: (n, 0)),
                  pl.BlockSpec((TILE, LANES), lambda n, d: (n, 0)),
                  pl.BlockSpec((1, 6, D), lambda n, d: (n // LPB, 0, 0)),
                  pl.BlockSpec((1, D), lambda n, d: (0, 0))],
        out_specs=pl.BlockSpec((TILE, D), lambda n, d: (n, 0)),
        scratch_shapes=[pltpu.VMEM((2, SROWS, D), BF16), pltpu.SemaphoreType.DMA((2,))],
    )
    return pl.pallas_call(
        _final_body,
        grid_spec=grid_spec,
        out_shape=jax.ShapeDtypeStruct((T_LAT, D), F32),
        compiler_params=_cp(("arbitrary",)),
        name="final",
    )(dst, x3, yb, lrow, gate, mod, g)


def _rot_cols(w):
    a, b, c, d = jnp.split(w, 4, axis=-1)
    return jnp.concatenate([-b, a, -d, c], axis=-1)


def _rope_tables():
    n = np.arange(SEQ)
    row = (n // GRID_W).astype(np.float32)
    col = (n % GRID_W).astype(np.float32)
    axis_dim = ROPE // 2
    inv = (ROPE_BASE ** (-np.arange(0, axis_dim, 2, dtype=np.float32) / axis_dim)).astype(np.float32)
    ar = row[:, None] * inv
    ac = col[:, None] * inv
    ang = np.concatenate([ar, ar, ac, ac], axis=-1).astype(np.float32)
    cos = np.concatenate([np.ones((CTX, ROPE), np.float32), np.cos(ang)], axis=0)
    sin = np.concatenate([np.zeros((CTX, ROPE), np.float32), np.sin(ang)], axis=0)
    return jnp.asarray(np.tile(cos, (1, HEADS))), jnp.asarray(np.tile(sin, (1, HEADS)))


def _dft(n, scale):
    j = np.arange(n, dtype=np.int64)
    ang = 2.0 * np.pi * ((j[:, None] * j[None, :]) % n).astype(np.float64) / n
    return (np.cos(ang) * scale), (np.sin(ang) * scale)


def _dft_tables():
    cn, sn = _dft(SEQ, SEQ ** -0.5)
    c2, s2 = _dft(CTX, CTX ** -0.5)
    cg, sg = _dft(FGD, FGD ** -0.5)
    eye = np.eye(FG)
    to = lambda a: jnp.asarray(a.astype(np.float32)).astype(BF16)
    return to(cn), to(sn), to(c2), to(s2), to(np.kron(eye, cg)), to(np.kron(eye, sg))


def _router_pad(rw, rb):
    rwp = jnp.zeros((LANES, D), F32).at[:NE].set(rw.T)
    rbp = jnp.full((LANES, 1), -1e30, F32).at[:NE, 0].set(rb)
    return rwp, rbp


def kernel(x, c, ctx, c_ctx, final_norm_g, l0_mod_w, l0_mod_b, l0_norm1_g, l0_w_in, l0_q_norm_g, l0_w_q_up, l0_kv_norm_g, l0_w_kv_up, l0_w_out, l0_norm2_g, l0_router_w, l0_router_b, l0_w_gu, l0_b_gu, l0_w_down, l0_b_down, l1_mod_w, l1_mod_b, l1_norm1_g, l1_w_in, l1_w_gk_fwd, l1_b_gk_fwd, l1_w_gk_bwd, l1_b_gk_bwd, l1_gnorm_g, l1_w_out, l1_norm2_g, l1_router_w, l1_router_b, l1_w_gu, l1_b_gu, l1_w_down, l1_b_down):
    row = lambda g: g.reshape(1, -1)
    c2 = ctx.reshape(B * CTX, D)
    x2 = x.reshape(T_LAT, D)
    cc = jnp.zeros((16, D), F32).at[:B].set(c).at[B].set(c_ctx)
    mod0 = adaln(cc, l0_mod_w, l0_mod_b)
    mod1 = adaln(cc, l1_mod_w, l1_mod_b)

    w_in = l0_w_in
    wqa = w_in[:, :Q_LORA].astype(BF16)
    wkva = w_in[:, Q_LORA:Q_LORA + KV_LORA].astype(BF16)
    wkr = w_in[:, Q_LORA + KV_LORA:Q_LORA + KV_LORA + ROPE]
    wkr2 = jnp.concatenate([wkr, _rot_cols(wkr)], axis=-1).astype(BF16)
    wf = w_in[:, Q_LORA + KV_LORA + ROPE:].astype(BF16)
    wq3 = l0_w_q_up.reshape(Q_LORA, HEADS, QK)
    wq_nope = wq3[:, :, :NOPE].reshape(Q_LORA, HEADS * NOPE)
    wq_rope = wq3[:, :, NOPE:]
    wqup = jnp.concatenate([wq_nope, wq_rope.reshape(Q_LORA, HEADS * ROPE),
                            _rot_cols(wq_rope).reshape(Q_LORA, HEADS * ROPE)], axis=-1).astype(BF16)
    wkv3 = l0_w_kv_up.reshape(KV_LORA, HEADS, NOPE + VDIM)
    wkvup = jnp.concatenate([wkv3[:, :, :NOPE].reshape(KV_LORA, HEADS * NOPE),
                             wkv3[:, :, NOPE:].reshape(KV_LORA, HEADS * VDIM)], axis=-1).astype(BF16)
    cos6, sin6 = _rope_tables()
    q, k, v, uf = even_in(c2, x2, mod0, row(l0_norm1_g), wqa, wkva, wkr2, wf, row(l0_q_norm_g), wqup,
                          row(l0_kv_norm_g), wkvup, cos6, sin6)
    att = attention(q, k, v).reshape(T_ALL, HEADS * VDIM)
    fmix = fourier(uf.reshape(B, ROWS_B, FW), *_dft_tables()).reshape(T_ALL, FW)
    rw0, rb0 = _router_pad(l0_router_w, l0_router_b)
    x1, lrow0, gate0, cnt0, xs0 = out0(c2, x2, att, fmix, mod0, l0_w_out[:HEADS * VDIM].astype(BF16),
                                       l0_w_out[HEADS * VDIM:].astype(BF16), row(l0_norm2_g), rw0, rb0)
    src0, be0, bv0, dst0, nb0 = _chunk_lists(cnt0, NT_ALL)
    yb0 = expert_ffn(xs0, src0, be0, bv0, nb0, l0_w_gu, l0_b_gu, l0_w_down, l0_b_down)

    w1 = l1_w_in
    o = 0
    wq1 = w1[:, o:o + GKW].astype(BF16); o += GKW
    wk1 = w1[:, o:o + GKW].astype(BF16); o += GKW
    wv1 = w1[:, o:o + GVW].astype(BF16); o += GVW
    wgo = w1[:, o:o + GVW].astype(BF16); o += GVW
    wgd = jnp.zeros((D, LANES), F32).at[:, :2 * GRANK].set(w1[:, o:]).astype(BF16)
    wgkf = jnp.zeros((LANES, GKW), F32).at[:GRANK].set(l1_w_gk_fwd)
    wgkb = jnp.zeros((LANES, GKW), F32).at[GRANK:2 * GRANK].set(l1_w_gk_bwd)
    x2, q1, k1, v1, go, gf, gb = odd_in(dst0, x1, yb0, lrow0, gate0, mod0, mod1, row(l1_norm1_g), wq1, wk1, wv1,
                                        wgo, wgd, wgkf, row(l1_b_gk_fwd), wgkb, row(l1_b_gk_bwd))
    o_f, o_b = gla(q1, k1, v1, gf, gb)
    rw1, rb1 = _router_pad(l1_router_w, l1_router_b)
    x3, lrow1, gate1, cnt1, xs1 = out1(o_f, o_b, go, x2, mod1, row(l1_gnorm_g), l1_w_out.astype(BF16),
                                       row(l1_norm2_g), rw1, rb1)
    src1, be1, bv1, dst1, nb1 = _chunk_lists(cnt1, NT_LAT)
    yb1 = expert_ffn(xs1, src1, be1, bv1, nb1, l1_w_gu, l1_b_gu, l1_w_down, l1_b_down)
    out = final(dst1, x3, yb1, lrow1, gate1, mod1, row(final_norm_g))
    return out.reshape(B, SEQ, D)
```

```python
import functools

import jax
import jax.numpy as jnp
import numpy as np
from jax import lax
from jax.experimental import pallas as pl
from jax.experimental.pallas import tpu as pltpu

D = 1024
B = 8
SEQ = 4096
CTX = 256
GRID_W = 64
EPS = 1e-6
HEADS = 6
Q_LORA = 384
KV_LORA = 256
NOPE = 128
ROPE = 64
VDIM = 128
QK = NOPE + ROPE
ROPE_BASE = 10000.0
FG = 4
FGD = 64
FW = FG * FGD
GH = 4
GDK = 128
GDV = 256
GRANK = 16
GNORM = 16.0
GCHUNK = 64
GKW = GH * GDK
GVW = GH * GDV
NE = 32
TOPK = 4
DFF = 1024
ALPHA = 1.702
LIMIT = 7.0

LANES = 128
TILE = 256
HALF = TILE // 2
HALVES = (slice(0, HALF), slice(HALF, TILE))
ROWS_B = CTX + SEQ
TPB = ROWS_B // TILE
T_ALL = B * ROWS_B
NT_ALL = T_ALL // TILE
T_LAT = B * SEQ
NT_LAT = T_LAT // TILE
LPB = SEQ // TILE
HPS = 3
CH = 16
MAXCH = (TILE * TOPK + NE * (CH - 1)) // CH
NCHT = 96
SROWS = NCHT * CH
FBLK = 512
CPB = FBLK // CH
GBS = 4
NCH_B = ROWS_B // GCHUNK
NCH_CTX = CTX // GCHUNK
VMEM_BIG = 56 * 1024 * 1024
VMEM_MID = 40 * 1024 * 1024

F32 = jnp.float32
BF16 = jnp.bfloat16
HI = lax.Precision.HIGHEST
NT_DIMS = (((1,), (1,)), ((), ()))
TN_DIMS = (((0,), (0,)), ((), ()))


def _cp(sem, vmem=VMEM_MID):
    return pltpu.CompilerParams(dimension_semantics=sem, vmem_limit_bytes=vmem)


def _rms(x, g):
    return x * lax.rsqrt(jnp.mean(x * x, axis=-1, keepdims=True) + EPS) * g


def _modulate(x, g, sh, sc):
    return _rms(x, g) * (1.0 + sc) + sh


def _silu(x):
    return x * jax.nn.sigmoid(x)


def _mod_index(i):
    return jnp.where(i % TPB == 0, B, i // TPB)


def _lat_tile(n):
    return n + n // LPB + 1


def _adaln_body(c_ref, w_ref, b_ref, o_ref):
    s = _silu(c_ref[...])
    o_ref[...] = jnp.dot(s, w_ref[...], precision=HI, preferred_element_type=F32) + b_ref[...]


def adaln(cc, mod_w, mod_b):
    tn = 1536
    out = pl.pallas_call(
        _adaln_body,
        grid=(6 * D // tn,),
        in_specs=[pl.BlockSpec((16, D), lambda j: (0, 0)),
                  pl.BlockSpec((D, tn), lambda j: (0, j)),
                  pl.BlockSpec((1, tn), lambda j: (0, j))],
        out_specs=pl.BlockSpec((16, tn), lambda j: (0, j)),
        out_shape=jax.ShapeDtypeStruct((16, 6 * D), F32),
        compiler_params=_cp(("arbitrary",)),
        name="adaln",
    )(cc, mod_w, mod_b.reshape(1, 6 * D))
    return out.reshape(16, 6, D)


def _stream_tile(c_ref, x_ref):
    return jnp.where(pl.program_id(0) % TPB == 0, c_ref[...], x_ref[...])


def _stream_specs():
    return [pl.BlockSpec((TILE, D), lambda i: (i // TPB, 0)),
            pl.BlockSpec((TILE, D), lambda i: ((i // TPB) * LPB + jnp.maximum(i % TPB - 1, 0), 0))]


def _even_in_body(c_ref, x_ref, mod_ref, n1_ref, wqa_ref, wkva_ref, wkr_ref, wf_ref, qg_ref, wqup_ref,
                  kvg_ref, wkvup_ref, cos_ref, sin_ref, q_ref, k_ref, v_ref, uf_ref):
    m = mod_ref[0]
    h = _modulate(_stream_tile(c_ref, x_ref), n1_ref[...], m[0:1], m[1:2]).astype(BF16)
    cq = jnp.dot(h, wqa_ref[...], preferred_element_type=F32)
    ckv = jnp.dot(h, wkva_ref[...], preferred_element_type=F32)
    kr2 = jnp.dot(h, wkr_ref[...], preferred_element_type=F32)
    uf_ref[...] = jnp.dot(h, wf_ref[...], preferred_element_type=F32).astype(BF16)
    cos6 = cos_ref[...]
    sin6 = sin_ref[...]
    scale = QK ** -0.5
    qall = jnp.dot(_rms(cq, qg_ref[...]).astype(BF16), wqup_ref[...], preferred_element_type=F32)
    qn = qall[:, :HEADS * NOPE] * scale
    qr = (qall[:, HEADS * NOPE:HEADS * QK] * cos6 + qall[:, HEADS * QK:] * sin6) * scale
    kvall = jnp.dot(_rms(ckv, kvg_ref[...]).astype(BF16), wkvup_ref[...], preferred_element_type=F32)
    kr = (kr2[:, :ROPE] * cos6[:, :ROPE] + kr2[:, ROPE:] * sin6[:, :ROPE]).astype(BF16)
    ones_col = (lax.broadcasted_iota(jnp.int32, (TILE, VDIM), 1) == 0).astype(BF16)
    for hd in range(HEADS):
        q_ref[0, hd, :, 0:NOPE] = qn[:, hd * NOPE:(hd + 1) * NOPE].astype(BF16)
        q_ref[0, hd, :, NOPE:QK] = qr[:, hd * ROPE:(hd + 1) * ROPE].astype(BF16)
        k_ref[0, hd, :, 0:NOPE] = kvall[:, hd * NOPE:(hd + 1) * NOPE].astype(BF16)
        k_ref[0, hd, :, NOPE:QK] = kr
        v_ref[0, hd, :, 0:VDIM] = kvall[:, HEADS * NOPE + hd * VDIM:HEADS * NOPE + (hd + 1) * VDIM].astype(BF16)
        v_ref[0, hd, :, VDIM:2 * VDIM] = ones_col


def even_in(c2, x2, mod, n1g, wqa, wkva, wkr2, wf, qg, wqup, kvg, wkvup, cos6, sin6):
    full = lambda a: pl.BlockSpec(a.shape, lambda i: (0,) * a.ndim)
    hs = lambda w: pl.BlockSpec((1, HEADS, TILE, w), lambda i: (i // TPB, 0, i % TPB, 0))
    return pl.pallas_call(
        _even_in_body,
        grid=(NT_ALL,),
        in_specs=_stream_specs() + [
                  pl.BlockSpec((1, 6, D), lambda i: (_mod_index(i), 0, 0)),
                  full(n1g), full(wqa), full(wkva), full(wkr2), full(wf), full(qg), full(wqup),
                  full(kvg), full(wkvup),
                  pl.BlockSpec((TILE, HEADS * ROPE), lambda i: (i % TPB, 0)),
                  pl.BlockSpec((TILE, HEADS * ROPE), lambda i: (i % TPB, 0))],
        out_specs=[hs(QK), hs(QK), hs(2 * VDIM), pl.BlockSpec((TILE, FW), lambda i: (i, 0))],
        out_shape=[jax.ShapeDtypeStruct((B, HEADS, ROWS_B, QK), BF16),
                   jax.ShapeDtypeStruct((B, HEADS, ROWS_B, QK), BF16),
                   jax.ShapeDtypeStruct((B, HEADS, ROWS_B, 2 * VDIM), BF16),
                   jax.ShapeDtypeStruct((T_ALL, FW), BF16)],
        compiler_params=_cp(("arbitrary",)),
        name="even_in",
    )(c2, x2, mod, n1g, wqa, wkva, wkr2, wf, qg, wqup, kvg, wkvup, cos6, sin6)


def _attend(q_ref, k_ref, v_ref, o_ref, rows):
    ks = [k_ref[0, j, 0:rows, :] for j in range(HPS)]
    ss = [lax.dot_general(q_ref[0, j], ks[j], NT_DIMS, preferred_element_type=F32) for j in range(HPS)]
    ps = [jnp.exp(s - jnp.max(s, axis=-1, keepdims=True)).astype(BF16) for s in ss]
    ov = [jnp.dot(ps[j], v_ref[0, j, 0:rows, :], preferred_element_type=F32) for j in range(HPS)]
    o_ref[0] = jnp.concatenate([o[:, :VDIM] / o[:, VDIM:VDIM + 1] for o in ov], axis=-1).astype(BF16)


def _attn_body(q_ref, k_ref, v_ref, o_ref):
    qt = pl.program_id(2)

    @pl.when(qt == 0)
    def _():
        _attend(q_ref, k_ref, v_ref, o_ref, CTX)

    @pl.when(qt > 0)
    def _():
        _attend(q_ref, k_ref, v_ref, o_ref, ROWS_B)


def attention(q, k, v):
    return pl.pallas_call(
        _attn_body,
        grid=(B, HEADS // HPS, TPB),
        in_specs=[pl.BlockSpec((1, HPS, TILE, QK), lambda b, h, t: (b, h, t, 0)),
                  pl.BlockSpec((1, HPS, ROWS_B, QK), lambda b, h, t: (b, h, 0, 0)),
                  pl.BlockSpec((1, HPS, ROWS_B, 2 * VDIM), lambda b, h, t: (b, h, 0, 0))],
        out_specs=pl.BlockSpec((1, TILE, HPS * VDIM), lambda b, h, t: (b, t, h)),
        out_shape=jax.ShapeDtypeStruct((B, ROWS_B, HEADS * VDIM), BF16),
        compiler_params=_cp(("arbitrary", "arbitrary", "arbitrary"), VMEM_BIG),
        name="attention",
    )(q, k, v)


def _fourier_body(u_ref, cn_ref, sn_ref, c2_ref, s2_ref, cc_ref, sc_ref, o_ref):
    i = pl.program_id(0)

    def finish(a, b):
        o_ref[0] = (jnp.dot(a.astype(BF16), cc_ref[...], preferred_element_type=F32)
                    - jnp.dot(b.astype(BF16), sc_ref[...], preferred_element_type=F32)).astype(BF16)

    @pl.when(i == 0)
    def _():
        u = u_ref[0, 0:CTX, :]
        finish(jnp.dot(c2_ref[...], u, preferred_element_type=F32),
               jnp.dot(s2_ref[...], u, preferred_element_type=F32))

    @pl.when(i > 0)
    def _():
        u = u_ref[0, CTX:ROWS_B, :]
        finish(jnp.dot(cn_ref[...], u, preferred_element_type=F32),
               jnp.dot(sn_ref[...], u, preferred_element_type=F32))


def fourier(uf, cn, sn, c2, s2, ccb, scb):
    full = lambda a: pl.BlockSpec(a.shape, lambda i, b: (0,) * a.ndim)
    return pl.pallas_call(
        _fourier_body,
        grid=(TPB, B),
        in_specs=[pl.BlockSpec((1, ROWS_B, FW), lambda i, b: (b, 0, 0)),
                  pl.BlockSpec((TILE, SEQ), lambda i, b: (jnp.maximum(i - 1, 0), 0)),
                  pl.BlockSpec((TILE, SEQ), lambda i, b: (jnp.maximum(i - 1, 0), 0)),
                  full(c2), full(s2), full(ccb), full(scb)],
        out_specs=pl.BlockSpec((1, TILE, FW), lambda i, b: (b, i, 0)),
        out_shape=jax.ShapeDtypeStruct((B, ROWS_B, FW), BF16),
        compiler_params=_cp(("arbitrary", "arbitrary")),
        name="fourier",
    )(uf, cn, sn, c2, s2, ccb, scb)


def _route(h2, rw_ref, rb_ref, lrow_ref, gate_ref, cnt_ref, xs_ref):
    h_hi = h2.astype(BF16)
    h_lo = (h2 - h_hi.astype(F32)).astype(BF16)
    rw = rw_ref[...]
    w_hi = rw.astype(BF16)
    w_lo = (rw - w_hi.astype(F32)).astype(BF16)
    both = lax.dot_general(jnp.concatenate([w_hi, w_lo], axis=0), h_hi, NT_DIMS, preferred_element_type=F32)
    l = (both[:LANES] + both[LANES:]
         + lax.dot_general(w_hi, h_lo, NT_DIMS, preferred_element_type=F32) + rb_ref[...])
    sub = lax.broadcasted_iota(jnp.int32, l.shape, 0)
    vals, idxs = [], []
    for _ in range(TOPK):
        mx = jnp.max(l, axis=0, keepdims=True)
        am = jnp.min(jnp.where(l == mx, sub, LANES), axis=0, keepdims=True)
        vals.append(mx)
        idxs.append(am)
        l = jnp.where(sub == am, -jnp.inf, l)
    ex = [jnp.exp(vv - vals[0]) for vv in vals]
    den = ex[0] + ex[1] + ex[2] + ex[3]
    onehot = jnp.zeros(l.shape, F32)
    for am in idxs:
        onehot = onehot + (sub == am).astype(F32)
    r = lax.broadcasted_iota(jnp.int32, (TILE, TILE), 0)
    c = lax.broadcasted_iota(jnp.int32, (TILE, TILE), 1)
    rank = jnp.dot(onehot.astype(BF16), (r < c).astype(BF16), preferred_element_type=F32)
    cnt = jnp.sum(onehot, axis=1, keepdims=True)
    nch = jnp.ceil(cnt * (1.0 / CH))
    er = lax.broadcasted_iota(jnp.int32, (LANES, LANES), 0)
    ec = lax.broadcasted_iota(jnp.int32, (LANES, LANES), 1)
    seg = jnp.dot((ec < er).astype(BF16), jnp.broadcast_to(nch, (LANES, LANES)).astype(BF16),
                  preferred_element_type=F32)[:, 0:1] * CH
    dest = rank + seg
    lrow_t = jnp.zeros(l.shape, F32)
    gate_t = jnp.zeros(l.shape, F32)
    rows = []
    for kk in range(TOPK):
        pos = jnp.sum(jnp.where(sub == idxs[kk], dest, 0.0), axis=0, keepdims=True)
        rows.append(pos.astype(jnp.int32))
        lrow_t = jnp.where(sub == kk, pos, lrow_t)
        gate_t = jnp.where(sub == kk, ex[kk] / den, gate_t)
    lrow_ref[...] = lrow_t.T.astype(jnp.int32)
    gate_ref[...] = gate_t.T
    cnt_ref[0] = jnp.broadcast_to(cnt, (LANES, LANES)).T[0:8, :]
    srow = lax.broadcasted_iota(jnp.int32, (SROWS, TILE), 0)
    hit = rows[0] == srow
    for kk in range(1, TOPK):
        hit = jnp.logical_or(hit, rows[kk] == srow)
    xs_ref[...] = jnp.dot(hit.astype(BF16), h_hi, preferred_element_type=F32).astype(BF16)


def _route_out_specs(n_tiles):
    specs = [pl.BlockSpec((TILE, LANES), lambda i: (i, 0)),
             pl.BlockSpec((TILE, LANES), lambda i: (i, 0)),
             pl.BlockSpec((1, 8, LANES), lambda i: (i, 0, 0)),
             pl.BlockSpec((SROWS, D), lambda i: (i, 0))]
    shapes = [jax.ShapeDtypeStruct((n_tiles * TILE, LANES), jnp.int32),
              jax.ShapeDtypeStruct((n_tiles * TILE, LANES), F32),
              jax.ShapeDtypeStruct((n_tiles, 8, LANES), F32),
              jax.ShapeDtypeStruct((n_tiles * SROWS, D), BF16)]
    return specs, shapes


def _out0_body(c_ref, x_ref, a_ref, f_ref, mod_ref, woa_ref, wof_ref, n2_ref, rw_ref, rb_ref,
               x1_ref, lrow_ref, gate_ref, cnt_ref, xs_ref):
    m = mod_ref[0]
    y = (jnp.dot(a_ref[...], woa_ref[...], preferred_element_type=F32)
         + jnp.dot(f_ref[...], wof_ref[...], preferred_element_type=F32))
    x1 = _stream_tile(c_ref, x_ref) + m[2:3] * y
    x1_ref[...] = x1
    h2 = _modulate(x1, n2_ref[...], m[3:4], m[4:5])
    _route(h2, rw_ref, rb_ref, lrow_ref, gate_ref, cnt_ref, xs_ref)


def out0(c2, x2, a, f, mod, woa, wof, n2g, rw, rb):
    full = lambda w: pl.BlockSpec(w.shape, lambda i: (0,) * w.ndim)
    rspecs, rshapes = _route_out_specs(NT_ALL)
    return pl.pallas_call(
        _out0_body,
        grid=(NT_ALL,),
        in_specs=_stream_specs() + [
                  pl.BlockSpec((TILE, HEADS * VDIM), lambda i: (i, 0)),
                  pl.BlockSpec((TILE, FW), lambda i: (i, 0)),
                  pl.BlockSpec((1, 6, D), lambda i: (_mod_index(i), 0, 0)),
                  full(woa), full(wof), full(n2g), full(rw), full(rb)],
        out_specs=[pl.BlockSpec((TILE, D), lambda i: (i, 0))] + rspecs,
        out_shape=[jax.ShapeDtypeStruct((T_ALL, D), F32)] + rshapes,
        compiler_params=_cp(("arbitrary",)),
        name="out0",
    )(c2, x2, a, f, mod, woa, wof, n2g, rw, rb)


def _chunk_lists(cnt, n_tiles):
    n_blk = (n_tiles * MAXCH + CPB - 1) // CPB + NE
    cnt = cnt[:, 0, :NE].astype(jnp.int32)
    nch = (cnt + CH - 1) // CH
    lend = jnp.cumsum(nch, axis=1)
    loff = lend - nch
    tend = jnp.cumsum(nch, axis=0)
    toff = tend - nch
    tot = tend[-1]
    blk = (tot + CPB - 1) // CPB
    bend = jnp.cumsum(blk)
    estart = (bend - blk) * CPB
    nvalid = bend[-1]
    bi = jnp.arange(n_blk, dtype=jnp.int32)
    bvalid = bi < nvalid
    count_le = lambda ends, v: jnp.sum((ends[None, :] <= v[:, None]).astype(jnp.int32), axis=1)
    be = jnp.where(bvalid, jnp.minimum(count_le(bend, bi), NE - 1), jnp.sum((bend < nvalid).astype(jnp.int32)))
    g = jnp.arange(n_blk * CPB, dtype=jnp.int32)
    sel = (jnp.repeat(be, CPB)[:, None] == jnp.arange(NE, dtype=jnp.int32)[None, :]).astype(F32)
    pick = lambda table: jnp.dot(sel, table.astype(F32), precision=HI).astype(jnp.int32)
    o = g - pick(estart[:, None])[:, 0]
    gvalid = jnp.logical_and(jnp.repeat(bvalid, CPB), o < pick(tot[:, None])[:, 0])
    tend_g = pick(tend.T)
    tile = jnp.minimum(jnp.sum((tend_g <= o[:, None]).astype(jnp.int32), axis=1), n_tiles - 1)
    tsel = tile[:, None] == jnp.arange(n_tiles, dtype=jnp.int32)[None, :]
    shift = jnp.sum(jnp.where(tsel, pick((loff - toff).T), 0), axis=1)
    src = jnp.where(gvalid, tile * NCHT + o + shift, 0).astype(jnp.int32)
    cslot = jnp.arange(NCHT, dtype=jnp.int32)
    inside = jnp.logical_and(loff[:, None, :] <= cslot[None, :, None], cslot[None, :, None] < lend[:, None, :])
    base = estart[None, :] + toff - loff
    dst = cslot[None, :] + jnp.sum(jnp.where(inside, base[:, None, :], 0), axis=2)
    dst = jnp.where(cslot[None, :] < lend[:, -1:], dst, 0).astype(jnp.int32)
    src = jnp.concatenate([src, jnp.zeros((CPB,), jnp.int32)])
    dst = jnp.concatenate([dst.reshape(-1), jnp.zeros((NCHT,), jnp.int32)])
    return src, be.astype(jnp.int32), bvalid.astype(jnp.int32), dst, n_blk


def _chunk_copy(src_hbm, chunk, buf, slot, j, sem):
    start = chunk * CH if isinstance(chunk, int) else pl.multiple_of(chunk * CH, CH)
    return pltpu.make_async_copy(src_hbm.at[pl.ds(start, CH), :],
                                 buf.at[slot, pl.ds(j * CH, CH), :], sem.at[slot])


class _ChunkGather:
    def __init__(self, list_ref, per_step, src_hbm, buf, sem):
        self.list_ref, self.per_step, self.src_hbm, self.buf, self.sem = list_ref, per_step, src_hbm, buf, sem
        self.step = pl.program_id(0)
        self.n_steps = pl.num_programs(0)

    def _issue(self, st, slot):
        for j in range(self.per_step):
            _chunk_copy(self.src_hbm, self.list_ref[st * self.per_step + j], self.buf, slot, j, self.sem).start()

    def _wait(self, slot):
        for j in range(self.per_step):
            _chunk_copy(self.src_hbm, 0, self.buf, slot, j, self.sem).wait()

    def arrive(self):
        @pl.when(self.step == 0)
        def _():
            self._issue(0, 0)

        slot = self.step % 2
        self._wait(slot)
        return slot

    def fetch_next(self):
        self._issue(self.step + 1, (self.step + 1) % 2)

    def drain(self):
        @pl.when(self.step == self.n_steps - 1)
        def _():
            self._wait((self.step + 1) % 2)


def _ffn_body(src_ref, be_ref, bv_ref, bf_ref, xs_hbm, wgu_ref, bg_ref, bl_ref, wdn_ref, bd_ref,
              yb_ref, xbuf, sem, wg_s, wl_s, wd_s):
    i = pl.program_id(0)
    rows = _ChunkGather(src_ref, CPB, xs_hbm, xbuf, sem)
    slot = rows.arrive()

    @pl.when(bf_ref[i] == 1)
    def _():
        r = lax.broadcasted_iota(jnp.int32, (2 * LANES, 2 * LANES), 0)
        c = lax.broadcasted_iota(jnp.int32, (2 * LANES, 2 * LANES), 1)
        split = (r == 2 * (c % LANES) + c // LANES).astype(BF16)
        for g in range(DFF // LANES):
            pair = wgu_ref[0, :, g * 2 * LANES:(g + 1) * 2 * LANES].astype(BF16)
            sep = jnp.dot(pair, split, preferred_element_type=F32)
            wg_s[:, g * LANES:(g + 1) * LANES] = sep[:, :LANES].astype(BF16)
            wl_s[:, g * LANES:(g + 1) * LANES] = sep[:, LANES:].astype(BF16)
        wd_s[...] = wdn_ref[0].astype(BF16)

    @pl.when(bv_ref[i] == 1)
    def _():
        rows.fetch_next()
        xb = xbuf[slot]
        glu = jnp.dot(xb, wg_s[...], preferred_element_type=F32) + bg_ref[0]
        lin = jnp.dot(xb, wl_s[...], preferred_element_type=F32) + bl_ref[0]
        glu = jnp.minimum(glu, LIMIT)
        lin = jnp.clip(lin, -LIMIT, LIMIT)
        act = glu * jax.nn.sigmoid(ALPHA * glu) * (lin + 1.0)
        out = jnp.dot(act.astype(BF16), wd_s[...], preferred_element_type=F32) + bd_ref[0]
        yb_ref[...] = out.astype(BF16)

    @pl.when(bv_ref[i] == 0)
    def _():
        rows.fetch_next()
        yb_ref[...] = jnp.zeros_like(yb_ref)

    rows.drain()


def expert_ffn(xs, src, be, bv, n_blk, w_gu, b_gu, w_down, b_down):
    bf = jnp.concatenate([jnp.ones((1,), jnp.int32), (be[1:] != be[:-1]).astype(jnp.int32)])
    bgu = b_gu.reshape(NE, 1, DFF, 2)
    wspec = lambda shp: pl.BlockSpec((1,) + shp, lambda i, src, be, bv, bf: (be[i], 0, 0))
    grid_spec = pltpu.PrefetchScalarGridSpec(
        num_scalar_prefetch=4,
        grid=(n_blk,),
        in_specs=[pl.BlockSpec(memory_space=pl.ANY),
                  wspec((D, 2 * DFF)), wspec((1, DFF)), wspec((1, DFF)),
                  wspec((DFF, D)), wspec((1, D))],
        out_specs=pl.BlockSpec((FBLK, D), lambda i, *_: (i, 0)),
        scratch_shapes=[pltpu.VMEM((2, FBLK, D), BF16), pltpu.SemaphoreType.DMA((2,)),
                        pltpu.VMEM((D, DFF), BF16), pltpu.VMEM((D, DFF), BF16), pltpu.VMEM((DFF, D), BF16)],
    )
    return pl.pallas_call(
        _ffn_body,
        grid_spec=grid_spec,
        out_shape=jax.ShapeDtypeStruct((n_blk * FBLK, D), BF16),
        compiler_params=_cp(("arbitrary",), VMEM_BIG),
        name="expert_ffn",
    )(src, be, bv, bf, xs, w_gu, bgu[..., 0], bgu[..., 1], w_down, b_down.reshape(NE, 1, D))


def _combine(dst_ref, yb_hbm, ybuf, sem, lrow_ref, gate_ref):
    rows = _ChunkGather(dst_ref, NCHT, yb_hbm, ybuf, sem)
    slot = rows.arrive()
    rows.fetch_next()
    scol = lax.broadcasted_iota(jnp.int32, (HALF, SROWS), 1)
    ws = []
    for rs in HALVES:
        lrow = lrow_ref[rs, :]
        gate = gate_ref[rs, :]
        w = jnp.zeros((HALF, SROWS), F32)
        for kk in range(TOPK):
            w = w + jnp.where(lrow[:, kk:kk + 1] == scol, gate[:, kk:kk + 1], 0.0)
        ws.append(w.astype(BF16))
    return [jnp.dot(w, ybuf[slot], preferred_element_type=F32) for w in ws], rows


def _odd_in_body(dst_ref, x_ref, yb_hbm, lrow_ref, gate_ref, mod0_ref, mod1_ref, n1_ref, wq_ref, wk_ref, wv_ref,
                 wgo_ref, wgd_ref, wgkf_ref, bgkf_ref, wgkb_ref, bgkb_ref,
                 x2_ref, q_ref, k_ref, v_ref, go_ref, gf_ref, gb_ref, ybuf, sem):
    m0 = mod0_ref[0]
    m1 = mod1_ref[0]
    ys, rows = _combine(dst_ref, yb_hbm, ybuf, sem, lrow_ref, gate_ref)
    hs = []
    for rs, y in zip(HALVES, ys):
        x2 = x_ref[rs, :] + m0[5:6] * y
        x2_ref[rs, :] = x2
        hs.append(_modulate(x2, n1_ref[...], m1[0:1], m1[1:2]).astype(BF16))
    for rs, h in zip(HALVES, hs):
        q_ref[rs, :] = jnp.dot(h, wq_ref[...], preferred_element_type=F32) * (GDK ** -0.5)
    for rs, h in zip(HALVES, hs):
        k_ref[rs, :] = jnp.dot(h, wk_ref[...], preferred_element_type=F32)
    for rs, h in zip(HALVES, hs):
        v_ref[rs, :] = jnp.dot(h, wv_ref[...], preferred_element_type=F32).astype(BF16)
    for rs, h in zip(HALVES, hs):
        go_ref[rs, :] = jnp.dot(h, wgo_ref[...], preferred_element_type=F32).astype(BF16)
    gds = [jnp.dot(h, wgd_ref[...], preferred_element_type=F32) for h in hs]

    def pieces(a):
        hi = a.astype(BF16)
        return hi, (a - hi.astype(F32)).astype(BF16)

    for w_ref, b_ref, g_ref in ((wgkf_ref, bgkf_ref, gf_ref), (wgkb_ref, bgkb_ref, gb_ref)):
        w_hi, w_lo = pieces(w_ref[...])
        for rs, gd in zip(HALVES, gds):
            gd_hi, gd_lo = pieces(gd)
            z = (jnp.dot(gd_hi, w_hi, preferred_element_type=F32) + jnp.dot(gd_hi, w_lo, preferred_element_type=F32)
                 + jnp.dot(gd_lo, w_hi, preferred_element_type=F32) + b_ref[...])
            g_ref[rs, :] = (jnp.minimum(z, 0.0) - jnp.log(1.0 + jnp.exp(-jnp.abs(z)))) / GNORM
    rows.drain()


def odd_in(dst, x1, yb, lrow, gate, mod0, mod1, n1g, wq, wk, wv, wgo, wgd, wgkf, bgkf, wgkb, bgkb):
    full = lambda w: pl.BlockSpec(w.shape, lambda i, d: (0,) * w.ndim)
    row = lambda w: pl.BlockSpec((TILE, w), lambda i, d: (i, 0))
    modspec = pl.BlockSpec((1, 6, D), lambda i, d: (_mod_index(i), 0, 0))
    grid_spec = pltpu.PrefetchScalarGridSpec(
        num_scalar_prefetch=1,
        grid=(NT_ALL,),
        in_specs=[row(D), pl.BlockSpec(memory_space=pl.ANY), row(LANES), row(LANES), modspec, modspec,
                  full(n1g), full(wq), full(wk), full(wv), full(wgo),
                  full(wgd), full(wgkf), full(bgkf), full(wgkb), full(bgkb)],
        out_specs=[row(D), row(GKW), row(GKW), row(GVW), row(GVW), row(GKW), row(GKW)],
        scratch_shapes=[pltpu.VMEM((2, SROWS, D), BF16), pltpu.SemaphoreType.DMA((2,))],
    )
    return pl.pallas_call(
        _odd_in_body,
        grid_spec=grid_spec,
        out_shape=[jax.ShapeDtypeStruct((T_ALL, D), F32),
                   jax.ShapeDtypeStruct((T_ALL, GKW), F32),
                   jax.ShapeDtypeStruct((T_ALL, GKW), F32),
                   jax.ShapeDtypeStruct((T_ALL, GVW), BF16),
                   jax.ShapeDtypeStruct((T_ALL, GVW), BF16),
                   jax.ShapeDtypeStruct((T_ALL, GKW), F32),
                   jax.ShapeDtypeStruct((T_ALL, GKW), F32)],
        compiler_params=_cp(("arbitrary",), VMEM_BIG),
        name="odd_in",
    )(dst, x1, yb, lrow, gate, mod0, mod1, n1g, wq, wk, wv, wgo, wgd, wgkf, bgkf, wgkb, bgkb)


def _gla_body(qf_ref, kf_ref, vf_ref, gf_ref, qb_ref, kb_ref, vb_ref, gb_ref, of_ref, ob_ref, sf_ref, sb_ref):
    @pl.when(pl.program_id(1) == 0)
    def _():
        sf_ref[...] = jnp.zeros_like(sf_ref)
        sb_ref[...] = jnp.zeros_like(sb_ref)

    r = lax.broadcasted_iota(jnp.int32, (GCHUNK, GCHUNK), 0)
    c = lax.broadcasted_iota(jnp.int32, (GCHUNK, GCHUNK), 1)
    dirs = []
    for s in range(GBS):
        at = lambda ref: ref.at[0, s]
        dirs.append((at(qf_ref), at(kf_ref), at(vf_ref), at(gf_ref), sf_ref.at[s], at(of_ref), c <= r, GCHUNK - 1))
        dirs.append((at(qb_ref), at(kb_ref), at(vb_ref), at(gb_ref), sb_ref.at[s], at(ob_ref), c >= r, 0))
    chains = []
    for q_ref, k_ref, v_ref, g_ref, s_ref, o_ref, mask, last in dirs:
        g = g_ref[...]
        g_hi = g.astype(BF16)
        g_lo = (g - g_hi.astype(F32)).astype(BF16)
        tri = mask.astype(BF16)
        cum = (jnp.dot(tri, g_hi, preferred_element_type=F32) + jnp.dot(tri, g_lo, preferred_element_type=F32))
        tot = cum[last:last + 1, :]
        q = q_ref[...]
        k = k_ref[...]
        qd = (q * jnp.exp(cum)).astype(BF16)
        kd = (k * jnp.exp(-cum)).astype(BF16)
        ke = (k * jnp.exp(tot - cum)).astype(BF16)
        dec = jnp.exp(tot)
        v = v_ref[...]
        for hd in range(GH):
            ks = slice(hd * GDK, (hd + 1) * GDK)
            vs = slice(hd * GDV, (hd + 1) * GDV)
            chains.append(dict(qd=qd[:, ks], kd=kd[:, ks], ke=ke[:, ks], dec=dec[:, ks], v=v[:, vs],
                               st=s_ref[hd], mask=mask))
    for ch in chains:
        sc = lax.dot_general(ch["qd"], ch["kd"], NT_DIMS, preferred_element_type=F32)
        ch["sc"] = jnp.where(ch["mask"], sc, 0.0).astype(BF16)
    for ch in chains:
        ch["out"] = (lax.dot_general(ch["qd"], ch["st"].astype(BF16), NT_DIMS, preferred_element_type=F32)
                     + jnp.dot(ch["sc"], ch["v"], preferred_element_type=F32))
    for ch in chains:
        ch["new"] = ch["st"] * ch["dec"] + lax.dot_general(ch["v"], ch["ke"], TN_DIMS, preferred_element_type=F32)
    for d, (_, _, _, _, s_ref, o_ref, _, _) in enumerate(dirs):
        o_ref[...] = jnp.concatenate([chains[d * GH + hd]["out"] for hd in range(GH)], axis=-1)
        for hd in range(GH):
            s_ref[hd] = chains[d * GH + hd]["new"]


def _bwd_chunk(c):
    return jnp.where(c < NCH_CTX, NCH_CTX - 1 - c, NCH_B + NCH_CTX - 1 - c)


def gla(q, k, v, gf, gb):
    fw = lambda w: pl.BlockSpec((1, GBS, GCHUNK, w), lambda b, c: (b, 0, c, 0))
    bw = lambda w: pl.BlockSpec((1, GBS, GCHUNK, w), lambda b, c: (b, 0, _bwd_chunk(c), 0))
    grouped = lambda a: a.reshape(B // GBS, GBS, ROWS_B, a.shape[-1])
    q, k, v, gf, gb = (grouped(a) for a in (q, k, v, gf, gb))
    oshape = jax.ShapeDtypeStruct((B // GBS, GBS, ROWS_B, GVW), F32)
    o_f, o_b = pl.pallas_call(
        _gla_body,
        grid=(B // GBS, NCH_B),
        in_specs=[fw(GKW), fw(GKW), fw(GVW), fw(GKW), bw(GKW), bw(GKW), bw(GVW), bw(GKW)],
        out_specs=[fw(GVW), bw(GVW)],
        out_shape=[oshape, oshape],
        scratch_shapes=[pltpu.VMEM((GBS, GH, GDV, GDK), F32), pltpu.VMEM((GBS, GH, GDV, GDK), F32)],
        compiler_params=_cp(("arbitrary", "arbitrary")),
        name="gla",
    )(q, k, v, gf, q, k, v, gb)
    return o_f.reshape(T_ALL, GVW), o_b.reshape(T_ALL, GVW)


def _out1_body(of_ref, ob_ref, go_ref, x_ref, mod_ref, gn_ref, wo_ref, n2_ref, rw_ref, rb_ref,
               x3_ref, lrow_ref, gate_ref, cnt_ref, xs_ref):
    m = mod_ref[0]
    o = of_ref[...] + ob_ref[...]
    gn = gn_ref[...]
    parts = [_rms(o[:, hd * GDV:(hd + 1) * GDV], gn) for hd in range(GH)]
    on = jnp.concatenate(parts, axis=-1) * _silu(go_ref[...].astype(F32))
    y = jnp.dot(on.astype(BF16), wo_ref[...], preferred_element_type=F32)
    x3 = x_ref[...] + m[2:3] * y
    x3_ref[...] = x3
    h2 = _modulate(x3, n2_ref[...], m[3:4], m[4:5])
    _route(h2, rw_ref, rb_ref, lrow_ref, gate_ref, cnt_ref, xs_ref)


def out1(of, ob, go, x2, mod, gng, wo, n2g, rw, rb):
    full = lambda w: pl.BlockSpec(w.shape, lambda n: (0,) * w.ndim)
    lat = lambda w: pl.BlockSpec((TILE, w), lambda n: (_lat_tile(n), 0))
    rspecs, rshapes = _route_out_specs(NT_LAT)
    return pl.pallas_call(
        _out1_body,
        grid=(NT_LAT,),
        in_specs=[lat(GVW), lat(GVW), lat(GVW), lat(D),
                  pl.BlockSpec((1, 6, D), lambda n: (n // LPB, 0, 0)),
                  full(gng), full(wo), full(n2g), full(rw), full(rb)],
        out_specs=[pl.BlockSpec((TILE, D), lambda n: (n, 0))] + rspecs,
        out_shape=[jax.ShapeDtypeStruct((T_LAT, D), F32)] + rshapes,
        compiler_params=_cp(("arbitrary",)),
        name="out1",
    )(of, ob, go, x2, mod, gng, wo, n2g, rw, rb)


def _final_body(dst_ref, x_ref, yb_hbm, lrow_ref, gate_ref, mod_ref, g_ref, o_ref, ybuf, sem):
    m = mod_ref[0]
    ys, rows = _combine(dst_ref, yb_hbm, ybuf, sem, lrow_ref, gate_ref)
    for rs, y in zip(HALVES, ys):
        o_ref[rs, :] = _rms(x_ref[rs, :] + m[5:6] * y, g_ref[...])
    rows.drain()


def final(dst, x3, yb, lrow, gate, mod, g):
    grid_spec = pltpu.PrefetchScalarGridSpec(
        num_scalar_prefetch=1,
        grid=(NT_LAT,),
        in_specs=[pl.BlockSpec((TILE, D), lambda n, d: (n, 0)),
                  pl.BlockSpec(memory_space=pl.ANY),
                  pl.BlockSpec((TILE, LANES), lambda n, d: (n, 0)),
                  pl.BlockSpec((TILE, LANES), lambda n, d: (n, 0)),
                  pl.BlockSpec((1, 6, D), lambda n, d: (n // LPB, 0, 0)),
                  pl.BlockSpec((1, D), lambda n, d: (0, 0))],
        out_specs=pl.BlockSpec((TILE, D), lambda n, d: (n, 0)),
        scratch_shapes=[pltpu.VMEM((2, SROWS, D), BF16), pltpu.SemaphoreType.DMA((2,))],
    )
    return pl.pallas_call(
        _final_body,
        grid_spec=grid_spec,
        out_shape=jax.ShapeDtypeStruct((T_LAT, D), F32),
        compiler_params=_cp(("arbitrary",)),
        name="final",
    )(dst, x3, yb, lrow, gate, mod, g)


def _rot_cols(w):
    a, b, c, d = jnp.split(w, 4, axis=-1)
    return jnp.concatenate([-b, a, -d, c], axis=-1)


def _rope_tables():
    n = np.arange(SEQ)
    row = (n // GRID_W).astype(np.float32)
    col = (n % GRID_W).astype(np.float32)
    axis_dim = ROPE // 2
    inv = (ROPE_BASE ** (-np.arange(0, axis_dim, 2, dtype=np.float32) / axis_dim)).astype(np.float32)
    ar = row[:, None] * inv
    ac = col[:, None] * inv
    ang = np.concatenate([ar, ar, ac, ac], axis=-1).astype(np.float32)
    cos = np.concatenate([np.ones((CTX, ROPE), np.float32), np.cos(ang)], axis=0)
    sin = np.concatenate([np.zeros((CTX, ROPE), np.float32), np.sin(ang)], axis=0)
    return jnp.asarray(np.tile(cos, (1, HEADS))), jnp.asarray(np.tile(sin, (1, HEADS)))


def _dft(n, scale):
    j = np.arange(n, dtype=np.int64)
    ang = 2.0 * np.pi * ((j[:, None] * j[None, :]) % n).astype(np.float64) / n
    return (np.cos(ang) * scale), (np.sin(ang) * scale)


def _dft_tables():
    cn, sn = _dft(SEQ, SEQ ** -0.5)
    c2, s2 = _dft(CTX, CTX ** -0.5)
    cg, sg = _dft(FGD, FGD ** -0.5)
    eye = np.eye(FG)
    to = lambda a: jnp.asarray(a.astype(np.float32)).astype(BF16)
    return to(cn), to(sn), to(c2), to(s2), to(np.kron(eye, cg)), to(np.kron(eye, sg))


def _router_pad(rw, rb):
    rwp = jnp.zeros((LANES, D), F32).at[:NE].set(rw.T)
    rbp = jnp.full((LANES, 1), -1e30, F32).at[:NE, 0].set(rb)
    return rwp, rbp


def kernel(x, c, ctx, c_ctx, final_norm_g, l0_mod_w, l0_mod_b, l0_norm1_g, l0_w_in, l0_q_norm_g, l0_w_q_up, l0_kv_norm_g, l0_w_kv_up, l0_w_out, l0_norm2_g, l0_router_w, l0_router_b, l0_w_gu, l0_b_gu, l0_w_down, l0_b_down, l1_mod_w, l1_mod_b, l1_norm1_g, l1_w_in, l1_w_gk_fwd, l1_b_gk_fwd, l1_w_gk_bwd, l1_b_gk_bwd, l1_gnorm_g, l1_w_out, l1_norm2_g, l1_router_w, l1_router_b, l1_w_gu, l1_b_gu, l1_w_down, l1_b_down):
    row = lambda g: g.reshape(1, -1)
    c2 = ctx.reshape(B * CTX, D)
    x2 = x.reshape(T_LAT, D)
    cc = jnp.zeros((16, D), F32).at[:B].set(c).at[B].set(c_ctx)
    mod0 = adaln(cc, l0_mod_w, l0_mod_b)
    mod1 = adaln(cc, l1_mod_w, l1_mod_b)

    w_in = l0_w_in
    wqa = w_in[:, :Q_LORA].astype(BF16)
    wkva = w_in[:, Q_LORA:Q_LORA + KV_LORA].astype(BF16)
    wkr = w_in[:, Q_LORA + KV_LORA:Q_LORA + KV_LORA + ROPE]
    wkr2 = jnp.concatenate([wkr, _rot_cols(wkr)], axis=-1).astype(BF16)
    wf = w_in[:, Q_LORA + KV_LORA + ROPE:].astype(BF16)
    wq3 = l0_w_q_up.reshape(Q_LORA, HEADS, QK)
    wq_nope = wq3[:, :, :NOPE].reshape(Q_LORA, HEADS * NOPE)
    wq_rope = wq3[:, :, NOPE:]
    wqup = jnp.concatenate([wq_nope, wq_rope.reshape(Q_LORA, HEADS * ROPE),
                            _rot_cols(wq_rope).reshape(Q_LORA, HEADS * ROPE)], axis=-1).astype(BF16)
    wkv3 = l0_w_kv_up.reshape(KV_LORA, HEADS, NOPE + VDIM)
    wkvup = jnp.concatenate([wkv3[:, :, :NOPE].reshape(KV_LORA, HEADS * NOPE),
                             wkv3[:, :, NOPE:].reshape(KV_LORA, HEADS * VDIM)], axis=-1).astype(BF16)
    cos6, sin6 = _rope_tables()
    q, k, v, uf = even_in(c2, x2, mod0, row(l0_norm1_g), wqa, wkva, wkr2, wf, row(l0_q_norm_g), wqup,
                          row(l0_kv_norm_g), wkvup, cos6, sin6)
    att = attention(q, k, v).reshape(T_ALL, HEADS * VDIM)
    fmix = fourier(uf.reshape(B, ROWS_B, FW), *_dft_tables()).reshape(T_ALL, FW)
    rw0, rb0 = _router_pad(l0_router_w, l0_router_b)
    x1, lrow0, gate0, cnt0, xs0 = out0(c2, x2, att, fmix, mod0, l0_w_out[:HEADS * VDIM].astype(BF16),
                                       l0_w_out[HEADS * VDIM:].astype(BF16), row(l0_norm2_g), rw0, rb0)
    src0, be0, bv0, dst0, nb0 = _chunk_lists(cnt0, NT_ALL)
    yb0 = expert_ffn(xs0, src0, be0, bv0, nb0, l0_w_gu, l0_b_gu, l0_w_down, l0_b_down)

    w1 = l1_w_in
    o = 0
    wq1 = w1[:, o:o + GKW].astype(BF16); o += GKW
    wk1 = w1[:, o:o + GKW].astype(BF16); o += GKW
    wv1 = w1[:, o:o + GVW].astype(BF16); o += GVW
    wgo = w1[:, o:o + GVW].astype(BF16); o += GVW
    wgd = jnp.zeros((D, LANES), F32).at[:, :2 * GRANK].set(w1[:, o:]).astype(BF16)
    wgkf = jnp.zeros((LANES, GKW), F32).at[:GRANK].set(l1_w_gk_fwd)
    wgkb = jnp.zeros((LANES, GKW), F32).at[GRANK:2 * GRANK].set(l1_w_gk_bwd)
    x2, q1, k1, v1, go, gf, gb = odd_in(dst0, x1, yb0, lrow0, gate0, mod0, mod1, row(l1_norm1_g), wq1, wk1, wv1,
                                        wgo, wgd, wgkf, row(l1_b_gk_fwd), wgkb, row(l1_b_gk_bwd))
    o_f, o_b = gla(q1, k1, v1, gf, gb)
    rw1, rb1 = _router_pad(l1_router_w, l1_router_b)
    x3, lrow1, gate1, cnt1, xs1 = out1(o_f, o_b, go, x2, mod1, row(l1_gnorm_g), l1_w_out.astype(BF16),
                                       row(l1_norm2_g), rw1, rb1)
    src1, be1, bv1, dst1, nb1 = _chunk_lists(cnt1, NT_LAT)
    yb1 = expert_ffn(xs1, src1, be1, bv1, nb1, l1_w_gu, l1_b_gu, l1_w_down, l1_b_down)
    out = final(dst1, x3, yb1, lrow1, gate1, mod1, row(final_norm_g))
    return out.reshape(B, SEQ, D)
```

```python
import jax
import jax.numpy as jnp
import numpy as np
from jax import lax
from jax.experimental import pallas as pl
from jax.experimental.pallas import tpu as pltpu

D = 1024
B = 8
SEQ = 4096
CTX = 256
GRID_W = 64
EPS = 1e-6
HEADS = 6
Q_LORA = 384
KV_LORA = 256
NOPE = 128
ROPE = 64
VDIM = 128
QK = NOPE + ROPE
ROPE_BASE = 10000.0
FG = 4
FGD = 64
FW = FG * FGD
GH = 4
GDK = 128
GDV = 256
GRANK = 16
GNORM = 16.0
GCHUNK = 64
GKW = GH * GDK
GVW = GH * GDV
NE = 32
TOPK = 4
DFF = 1024
ALPHA = 1.702
LIMIT = 7.0

LANES = 128
TILE = 256
HALF = TILE // 2
HALVES = (slice(0, HALF), slice(HALF, TILE))
ROWS_B = CTX + SEQ
TPB = ROWS_B // TILE
T_ALL = B * ROWS_B
NT_ALL = T_ALL // TILE
T_LAT = B * SEQ
NT_LAT = T_LAT // TILE
LPB = SEQ // TILE
HPS = 3
CH = 16
MAXCH = (TILE * TOPK + NE * (CH - 1)) // CH
NCHT = -(-MAXCH * CH // LANES) * LANES // CH
SROWS = NCHT * CH
SGRP = 512
FBLK = 512
CPB = FBLK // CH
GBS = 8
NCH_B = ROWS_B // GCHUNK
NCH_CTX = CTX // GCHUNK
V7X_VMEM_BYTES = 64 * 1024 * 1024
VMEM_BIG = V7X_VMEM_BYTES * 7 // 8
VMEM_MID = V7X_VMEM_BYTES * 5 // 8

F32 = jnp.float32
BF16 = jnp.bfloat16
HI = lax.Precision.HIGHEST
NT_DIMS = (((1,), (1,)), ((), ()))
TN_DIMS = (((0,), (0,)), ((), ()))


def _cp(sem, vmem=VMEM_MID):
    return pltpu.CompilerParams(dimension_semantics=sem, vmem_limit_bytes=vmem)


def _rms(x, g):
    return x * lax.rsqrt(jnp.mean(x * x, axis=-1, keepdims=True) + EPS) * g


def _modulate(x, g, sh, sc):
    return _rms(x, g) * (1.0 + sc) + sh


def _silu(x):
    return x * jax.nn.sigmoid(x)


def _mod_index(i):
    return jnp.where(i % TPB == 0, B, i // TPB)


def _lat_tile(n):
    return n + n // LPB + 1


def _adaln_body(c_ref, w_ref, b_ref, o_ref):
    s = _silu(c_ref[...])
    o_ref[...] = jnp.dot(s, w_ref[...], precision=HI, preferred_element_type=F32) + b_ref[...]


def adaln(cc, mod_w, mod_b):
    tn = 1536
    out = pl.pallas_call(
        _adaln_body,
        grid=(6 * D // tn,),
        in_specs=[pl.BlockSpec((16, D), lambda j: (0, 0)),
                  pl.BlockSpec((D, tn), lambda j: (0, j)),
                  pl.BlockSpec((1, tn), lambda j: (0, j))],
        out_specs=pl.BlockSpec((16, tn), lambda j: (0, j)),
        out_shape=jax.ShapeDtypeStruct((16, 6 * D), F32),
        compiler_params=_cp(("arbitrary",)),
        name="adaln",
    )(cc, mod_w, mod_b.reshape(1, 6 * D))
    return out.reshape(16, 6, D)


def _stream_tile(c_ref, x_ref):
    return jnp.where(pl.program_id(0) % TPB == 0, c_ref[...], x_ref[...])


def _stream_specs():
    return [pl.BlockSpec((TILE, D), lambda i: (i // TPB, 0)),
            pl.BlockSpec((TILE, D), lambda i: ((i // TPB) * LPB + jnp.maximum(i % TPB - 1, 0), 0))]


def _even_in_body(c_ref, x_ref, mod_ref, n1_ref, wqa_ref, wkva_ref, wkr_ref, wf_ref, qg_ref, wqup_ref,
                  kvg_ref, wkvup_ref, cos_ref, sin_ref, q_ref, k_ref, v_ref, uf_ref):
    m = mod_ref[0]
    h = _modulate(_stream_tile(c_ref, x_ref), n1_ref[...], m[0:1], m[1:2]).astype(BF16)
    cq = jnp.dot(h, wqa_ref[...], preferred_element_type=F32)
    ckv = jnp.dot(h, wkva_ref[...], preferred_element_type=F32)
    kr2 = jnp.dot(h, wkr_ref[...], preferred_element_type=F32)
    uf_ref[...] = jnp.dot(h, wf_ref[...], preferred_element_type=F32).astype(BF16)
    cos6 = cos_ref[...]
    sin6 = sin_ref[...]
    scale = QK ** -0.5
    qall = jnp.dot(_rms(cq, qg_ref[...]).astype(BF16), wqup_ref[...], preferred_element_type=F32)
    qn = qall[:, :HEADS * NOPE] * scale
    qr = (qall[:, HEADS * NOPE:HEADS * QK] * cos6 + qall[:, HEADS * QK:] * sin6) * scale
    kvall = jnp.dot(_rms(ckv, kvg_ref[...]).astype(BF16), wkvup_ref[...], preferred_element_type=F32)
    kr = (kr2[:, :ROPE] * cos6[:, :ROPE] + kr2[:, ROPE:] * sin6[:, :ROPE]).astype(BF16)
    ones_col = (lax.broadcasted_iota(jnp.int32, (TILE, VDIM), 1) == 0).astype(BF16)
    for hd in range(HEADS):
        q_ref[0, hd, :, 0:NOPE] = qn[:, hd * NOPE:(hd + 1) * NOPE].astype(BF16)
        q_ref[0, hd, :, NOPE:QK] = qr[:, hd * ROPE:(hd + 1) * ROPE].astype(BF16)
        k_ref[0, hd, :, 0:NOPE] = kvall[:, hd * NOPE:(hd + 1) * NOPE].astype(BF16)
        k_ref[0, hd, :, NOPE:QK] = kr
        v_ref[0, hd, :, 0:VDIM] = kvall[:, HEADS * NOPE + hd * VDIM:HEADS * NOPE + (hd + 1) * VDIM].astype(BF16)
        v_ref[0, hd, :, VDIM:2 * VDIM] = ones_col


def even_in(c2, x2, mod, n1g, wqa, wkva, wkr2, wf, qg, wqup, kvg, wkvup, cos6, sin6):
    full = lambda a: pl.BlockSpec(a.shape, lambda i: (0,) * a.ndim)
    hs = lambda w: pl.BlockSpec((1, HEADS, TILE, w), lambda i: (i // TPB, 0, i % TPB, 0))
    return pl.pallas_call(
        _even_in_body,
        grid=(NT_ALL,),
        in_specs=_stream_specs() + [
                  pl.BlockSpec((1, 6, D), lambda i: (_mod_index(i), 0, 0)),
                  full(n1g), full(wqa), full(wkva), full(wkr2), full(wf), full(qg), full(wqup),
                  full(kvg), full(wkvup),
                  pl.BlockSpec((TILE, HEADS * ROPE), lambda i: (i % TPB, 0)),
                  pl.BlockSpec((TILE, HEADS * ROPE), lambda i: (i % TPB, 0))],
        out_specs=[hs(QK), hs(QK), hs(2 * VDIM), pl.BlockSpec((TILE, FW), lambda i: (i, 0))],
        out_shape=[jax.ShapeDtypeStruct((B, HEADS, ROWS_B, QK), BF16),
                   jax.ShapeDtypeStruct((B, HEADS, ROWS_B, QK), BF16),
                   jax.ShapeDtypeStruct((B, HEADS, ROWS_B, 2 * VDIM), BF16),
                   jax.ShapeDtypeStruct((T_ALL, FW), BF16)],
        compiler_params=_cp(("arbitrary",)),
        name="even_in",
    )(c2, x2, mod, n1g, wqa, wkva, wkr2, wf, qg, wqup, kvg, wkvup, cos6, sin6)


def _attend(q_ref, k_ref, v_ref, o_ref, rows):
    ks = [k_ref[0, j, 0:rows, :] for j in range(HPS)]
    ss = [lax.dot_general(q_ref[0, j], ks[j], NT_DIMS, preferred_element_type=F32) for j in range(HPS)]
    ps = [jnp.exp(s - jnp.max(s, axis=-1, keepdims=True)).astype(BF16) for s in ss]
    ov = [jnp.dot(ps[j], v_ref[0, j, 0:rows, :], preferred_element_type=F32) for j in range(HPS)]
    o_ref[0] = jnp.concatenate([o[:, :VDIM] / o[:, VDIM:VDIM + 1] for o in ov], axis=-1).astype(BF16)


def _attn_body(q_ref, k_ref, v_ref, o_ref):
    qt = pl.program_id(2)

    @pl.when(qt == 0)
    def _():
        _attend(q_ref, k_ref, v_ref, o_ref, CTX)

    @pl.when(qt > 0)
    def _():
        _attend(q_ref, k_ref, v_ref, o_ref, ROWS_B)


def attention(q, k, v):
    return pl.pallas_call(
        _attn_body,
        grid=(B, HEADS // HPS, TPB),
        in_specs=[pl.BlockSpec((1, HPS, TILE, QK), lambda b, h, t: (b, h, t, 0)),
                  pl.BlockSpec((1, HPS, ROWS_B, QK), lambda b, h, t: (b, h, 0, 0)),
                  pl.BlockSpec((1, HPS, ROWS_B, 2 * VDIM), lambda b, h, t: (b, h, 0, 0))],
        out_specs=pl.BlockSpec((1, TILE, HPS * VDIM), lambda b, h, t: (b, t, h)),
        out_shape=jax.ShapeDtypeStruct((B, ROWS_B, HEADS * VDIM), BF16),
        compiler_params=_cp(("arbitrary", "arbitrary", "arbitrary"), VMEM_BIG),
        name="attention",
    )(q, k, v)


def _fourier_body(u_ref, cn_ref, sn_ref, c2_ref, s2_ref, cc_ref, sc_ref, o_ref):
    i = pl.program_id(0)

    def finish(a, b):
        o_ref[0] = (jnp.dot(a.astype(BF16), cc_ref[...], preferred_element_type=F32)
                    - jnp.dot(b.astype(BF16), sc_ref[...], preferred_element_type=F32)).astype(BF16)

    @pl.when(i == 0)
    def _():
        u = u_ref[0, 0:CTX, :]
        finish(jnp.dot(c2_ref[...], u, preferred_element_type=F32),
               jnp.dot(s2_ref[...], u, preferred_element_type=F32))

    @pl.when(i > 0)
    def _():
        u = u_ref[0, CTX:ROWS_B, :]
        finish(jnp.dot(cn_ref[...], u, preferred_element_type=F32),
               jnp.dot(sn_ref[...], u, preferred_element_type=F32))


def fourier(uf, cn, sn, c2, s2, ccb, scb):
    full = lambda a: pl.BlockSpec(a.shape, lambda i, b: (0,) * a.ndim)
    return pl.pallas_call(
        _fourier_body,
        grid=(TPB, B),
        in_specs=[pl.BlockSpec((1, ROWS_B, FW), lambda i, b: (b, 0, 0)),
                  pl.BlockSpec((TILE, SEQ), lambda i, b: (jnp.maximum(i - 1, 0), 0)),
                  pl.BlockSpec((TILE, SEQ), lambda i, b: (jnp.maximum(i - 1, 0), 0)),
                  full(c2), full(s2), full(ccb), full(scb)],
        out_specs=pl.BlockSpec((1, TILE, FW), lambda i, b: (b, i, 0)),
        out_shape=jax.ShapeDtypeStruct((B, ROWS_B, FW), BF16),
        compiler_params=_cp(("arbitrary", "arbitrary")),
        name="fourier",
    )(uf, cn, sn, c2, s2, ccb, scb)


def _route(h2, rw_ref, rb_ref, lrow_ref, gate_ref, cnt_ref, xs_ref):
    h_hi = h2.astype(BF16)
    h_lo = (h2 - h_hi.astype(F32)).astype(BF16)
    rw = rw_ref[...]
    w_hi = rw.astype(BF16)
    w_lo = (rw - w_hi.astype(F32)).astype(BF16)
    both = lax.dot_general(jnp.concatenate([w_hi, w_lo], axis=0), h_hi, NT_DIMS, preferred_element_type=F32)
    l = (both[:LANES] + both[LANES:]
         + lax.dot_general(w_hi, h_lo, NT_DIMS, preferred_element_type=F32) + rb_ref[...])
    sub = lax.broadcasted_iota(jnp.int32, l.shape, 0)
    vals, idxs = [], []
    for _ in range(TOPK):
        mx = jnp.max(l, axis=0, keepdims=True)
        am = jnp.min(jnp.where(l == mx, sub, LANES), axis=0, keepdims=True)
        vals.append(mx)
        idxs.append(am)
        l = jnp.where(sub == am, -jnp.inf, l)
    ex = [jnp.exp(vv - vals[0]) for vv in vals]
    den = ex[0] + ex[1] + ex[2] + ex[3]
    onehot = jnp.zeros(l.shape, F32)
    for am in idxs:
        onehot = onehot + (sub == am).astype(F32)
    r = lax.broadcasted_iota(jnp.int32, (TILE, TILE), 0)
    c = lax.broadcasted_iota(jnp.int32, (TILE, TILE), 1)
    rank = jnp.dot(onehot.astype(BF16), (r < c).astype(BF16), preferred_element_type=F32)
    cnt = jnp.sum(onehot, axis=1, keepdims=True)
    nch = jnp.ceil(cnt * (1.0 / CH))
    er = lax.broadcasted_iota(jnp.int32, (LANES, LANES), 0)
    ec = lax.broadcasted_iota(jnp.int32, (LANES, LANES), 1)
    seg = jnp.dot((ec < er).astype(BF16), jnp.broadcast_to(nch, (LANES, LANES)).astype(BF16),
                  preferred_element_type=F32)[:, 0:1] * CH
    dest = rank + seg
    lrow_t = jnp.zeros(l.shape, F32)
    gate_t = jnp.zeros(l.shape, F32)
    rows = []
    for kk in range(TOPK):
        pos = jnp.sum(jnp.where(sub == idxs[kk], dest, 0.0), axis=0, keepdims=True)
        rows.append(pos.astype(jnp.int32))
        lrow_t = jnp.where(sub == kk, pos, lrow_t)
        gate_t = jnp.where(sub == kk, ex[kk] / den, gate_t)
    lrow_ref[...] = lrow_t.T.astype(jnp.int32)
    gate_ref[...] = gate_t.T
    cnt_ref[0] = jnp.broadcast_to(cnt, (LANES, LANES)).T[0:8, :]
    for g in range(SROWS // SGRP):
        srow = g * SGRP + lax.broadcasted_iota(jnp.int32, (SGRP, TILE), 0)
        hit = rows[0] == srow
        for kk in range(1, TOPK):
            hit = jnp.logical_or(hit, rows[kk] == srow)
        xs_ref[g * SGRP:(g + 1) * SGRP, :] = jnp.dot(hit.astype(BF16), h_hi,
                                                    preferred_element_type=F32).astype(BF16)


def _route_out_specs(n_tiles):
    specs = [pl.BlockSpec((TILE, LANES), lambda i: (i, 0)),
             pl.BlockSpec((TILE, LANES), lambda i: (i, 0)),
             pl.BlockSpec((1, 8, LANES), lambda i: (i, 0, 0)),
             pl.BlockSpec((SROWS, D), lambda i: (i, 0))]
    shapes = [jax.ShapeDtypeStruct((n_tiles * TILE, LANES), jnp.int32),
              jax.ShapeDtypeStruct((n_tiles * TILE, LANES), F32),
              jax.ShapeDtypeStruct((n_tiles, 8, LANES), F32),
              jax.ShapeDtypeStruct((n_tiles * SROWS, D), BF16)]
    return specs, shapes


def _out0_body(c_ref, x_ref, a_ref, f_ref, mod_ref, woa_ref, wof_ref, n2_ref, rw_ref, rb_ref,
               x1_ref, lrow_ref, gate_ref, cnt_ref, xs_ref):
    m = mod_ref[0]
    y = (jnp.dot(a_ref[...], woa_ref[...], preferred_element_type=F32)
         + jnp.dot(f_ref[...], wof_ref[...], preferred_element_type=F32))
    x1 = _stream_tile(c_ref, x_ref) + m[2:3] * y
    x1_ref[...] = x1
    h2 = _modulate(x1, n2_ref[...], m[3:4], m[4:5])
    _route(h2, rw_ref, rb_ref, lrow_ref, gate_ref, cnt_ref, xs_ref)


def out0(c2, x2, a, f, mod, woa, wof, n2g, rw, rb):
    full = lambda w: pl.BlockSpec(w.shape, lambda i: (0,) * w.ndim)
    rspecs, rshapes = _route_out_specs(NT_ALL)
    return pl.pallas_call(
        _out0_body,
        grid=(NT_ALL,),
        in_specs=_stream_specs() + [
                  pl.BlockSpec((TILE, HEADS * VDIM), lambda i: (i, 0)),
                  pl.BlockSpec((TILE, FW), lambda i: (i, 0)),
                  pl.BlockSpec((1, 6, D), lambda i: (_mod_index(i), 0, 0)),
                  full(woa), full(wof), full(n2g), full(rw), full(rb)],
        out_specs=[pl.BlockSpec((TILE, D), lambda i: (i, 0))] + rspecs,
        out_shape=[jax.ShapeDtypeStruct((T_ALL, D), F32)] + rshapes,
        compiler_params=_cp(("arbitrary",)),
        name="out0",
    )(c2, x2, a, f, mod, woa, wof, n2g, rw, rb)


def _chunk_lists(cnt, n_tiles):
    n_blk = (n_tiles * MAXCH + CPB - 1) // CPB + NE
    cnt = cnt[:, 0, :NE].astype(jnp.int32)
    nch = (cnt + CH - 1) // CH
    lend = jnp.cumsum(nch, axis=1)
    loff = lend - nch
    tend = jnp.cumsum(nch, axis=0)
    toff = tend - nch
    tot = tend[-1]
    blk = (tot + CPB - 1) // CPB
    bend = jnp.cumsum(blk)
    estart = (bend - blk) * CPB
    nvalid = bend[-1]
    bi = jnp.arange(n_blk, dtype=jnp.int32)
    bvalid = bi < nvalid
    count_le = lambda ends, v: jnp.sum((ends[None, :] <= v[:, None]).astype(jnp.int32), axis=1)
    be = jnp.where(bvalid, jnp.minimum(count_le(bend, bi), NE - 1), jnp.sum((bend < nvalid).astype(jnp.int32)))
    g = jnp.arange(n_blk * CPB, dtype=jnp.int32)
    sel = (jnp.repeat(be, CPB)[:, None] == jnp.arange(NE, dtype=jnp.int32)[None, :]).astype(F32)
    pick = lambda table: jnp.dot(sel, table.astype(F32), precision=HI).astype(jnp.int32)
    o = g - pick(estart[:, None])[:, 0]
    gvalid = jnp.logical_and(jnp.repeat(bvalid, CPB), o < pick(tot[:, None])[:, 0])
    tend_g = pick(tend.T)
    tile = jnp.minimum(jnp.sum((tend_g <= o[:, None]).astype(jnp.int32), axis=1), n_tiles - 1)
    tsel = tile[:, None] == jnp.arange(n_tiles, dtype=jnp.int32)[None, :]
    shift = jnp.sum(jnp.where(tsel, pick((loff - toff).T), 0), axis=1)
    src = jnp.where(gvalid, tile * NCHT + o + shift, 0).astype(jnp.int32)
    cslot = jnp.arange(NCHT, dtype=jnp.int32)
    inside = jnp.logical_and(loff[:, None, :] <= cslot[None, :, None], cslot[None, :, None] < lend[:, None, :])
    base = estart[None, :] + toff - loff
    dst = cslot[None, :] + jnp.sum(jnp.where(inside, base[:, None, :], 0), axis=2)
    dst = jnp.where(cslot[None, :] < lend[:, -1:], dst, 0).astype(jnp.int32)
    src = jnp.concatenate([src, jnp.zeros((CPB,), jnp.int32)])
    dst = jnp.concatenate([dst.reshape(-1), jnp.zeros((NCHT,), jnp.int32)])
    return src, be.astype(jnp.int32), bvalid.astype(jnp.int32), dst, n_blk


def _chunk_copy(src_hbm, chunk, buf, slot, j, sem):
    start = chunk * CH if isinstance(chunk, int) else pl.multiple_of(chunk * CH, CH)
    return pltpu.make_async_copy(src_hbm.at[pl.ds(start, CH), :],
                                 buf.at[slot, pl.ds(j * CH, CH), :], sem.at[slot])


class _ChunkGather:
    def __init__(self, list_ref, per_step, src_hbm, buf, sem):
        self.list_ref, self.per_step, self.src_hbm, self.buf, self.sem = list_ref, per_step, src_hbm, buf, sem
        self.step = pl.program_id(0)
        self.n_steps = pl.num_programs(0)

    def _issue(self, st, slot):
        for j in range(self.per_step):
            _chunk_copy(self.src_hbm, self.list_ref[st * self.per_step + j], self.buf, slot, j, self.sem).start()

    def _wait(self, slot):
        for j in range(self.per_step):
            _chunk_copy(self.src_hbm, 0, self.buf, slot, j, self.sem).wait()

    def arrive(self):
        @pl.when(self.step == 0)
        def _():
            self._issue(0, 0)

        slot = self.step % 2
        self._wait(slot)
        return slot

    def fetch_next(self):
        self._issue(self.step + 1, (self.step + 1) % 2)

    def drain(self):
        @pl.when(self.step == self.n_steps - 1)
        def _():
            self._wait((self.step + 1) % 2)


def _ffn_body(src_ref, be_ref, bv_ref, bf_ref, xs_hbm, wgu_ref, bg_ref, bl_ref, wdn_ref, bd_ref,
              yb_ref, xbuf, sem, wg_s, wl_s, wd_s):
    i = pl.program_id(0)
    rows = _ChunkGather(src_ref, CPB, xs_hbm, xbuf, sem)
    slot = rows.arrive()

    @pl.when(bf_ref[i] == 1)
    def _():
        r = lax.broadcasted_iota(jnp.int32, (2 * LANES, 2 * LANES), 0)
        c = lax.broadcasted_iota(jnp.int32, (2 * LANES, 2 * LANES), 1)
        split = (r == 2 * (c % LANES) + c // LANES).astype(BF16)
        for g in range(DFF // LANES):
            pair = wgu_ref[0, :, g * 2 * LANES:(g + 1) * 2 * LANES].astype(BF16)
            sep = jnp.dot(pair, split, preferred_element_type=F32)
            wg_s[:, g * LANES:(g + 1) * LANES] = sep[:, :LANES].astype(BF16)
            wl_s[:, g * LANES:(g + 1) * LANES] = sep[:, LANES:].astype(BF16)
        wd_s[...] = wdn_ref[0].astype(BF16)

    @pl.when(bv_ref[i] == 1)
    def _():
        rows.fetch_next()
        xb = xbuf[slot]
        glu = jnp.dot(xb, wg_s[...], preferred_element_type=F32) + bg_ref[0]
        lin = jnp.dot(xb, wl_s[...], preferred_element_type=F32) + bl_ref[0]
        glu = jnp.minimum(glu, LIMIT)
        lin = jnp.clip(lin, -LIMIT, LIMIT)
        act = glu * jax.nn.sigmoid(ALPHA * glu) * (lin + 1.0)
        out = jnp.dot(act.astype(BF16), wd_s[...], preferred_element_type=F32) + bd_ref[0]
        yb_ref[...] = out.astype(BF16)

    @pl.when(bv_ref[i] == 0)
    def _():
        rows.fetch_next()
        yb_ref[...] = jnp.zeros_like(yb_ref)

    rows.drain()


def expert_ffn(xs, src, be, bv, n_blk, w_gu, b_gu, w_down, b_down):
    bf = jnp.concatenate([jnp.ones((1,), jnp.int32), (be[1:] != be[:-1]).astype(jnp.int32)])
    bgu = b_gu.reshape(NE, 1, DFF, 2)
    wspec = lambda shp: pl.BlockSpec((1,) + shp, lambda i, src, be, bv, bf: (be[i], 0, 0))
    grid_spec = pltpu.PrefetchScalarGridSpec(
        num_scalar_prefetch=4,
        grid=(n_blk,),
        in_specs=[pl.BlockSpec(memory_space=pl.ANY),
                  wspec((D, 2 * DFF)), wspec((1, DFF)), wspec((1, DFF)),
                  wspec((DFF, D)), wspec((1, D))],
        out_specs=pl.BlockSpec((FBLK, D), lambda i, *_: (i, 0)),
        scratch_shapes=[pltpu.VMEM((2, FBLK, D), BF16), pltpu.SemaphoreType.DMA((2,)),
                        pltpu.VMEM((D, DFF), BF16), pltpu.VMEM((D, DFF), BF16), pltpu.VMEM((DFF, D), BF16)],
    )
    return pl.pallas_call(
        _ffn_body,
        grid_spec=grid_spec,
        out_shape=jax.ShapeDtypeStruct((n_blk * FBLK, D), BF16),
        compiler_params=_cp(("arbitrary",), VMEM_BIG),
        name="expert_ffn",
    )(src, be, bv, bf, xs, w_gu, bgu[..., 0], bgu[..., 1], w_down, b_down.reshape(NE, 1, D))


def _combine(dst_ref, yb_hbm, ybuf, sem, lrow_ref, gate_ref):
    rows = _ChunkGather(dst_ref, NCHT, yb_hbm, ybuf, sem)
    slot = rows.arrive()
    rows.fetch_next()
    scol = lax.broadcasted_iota(jnp.int32, (HALF, SROWS), 1)
    ws = []
    for rs in HALVES:
        lrow = lrow_ref[rs, :]
        gate = gate_ref[rs, :]
        w = jnp.zeros((HALF, SROWS), F32)
        for kk in range(TOPK):
            w = w + jnp.where(lrow[:, kk:kk + 1] == scol, gate[:, kk:kk + 1], 0.0)
        ws.append(w.astype(BF16))
    return [jnp.dot(w, ybuf[slot], preferred_element_type=F32) for w in ws], rows


def _odd_in_body(dst_ref, x_ref, yb_hbm, lrow_ref, gate_ref, mod0_ref, mod1_ref, n1_ref, wq_ref, wk_ref, wv_ref,
                 wgo_ref, wgd_ref, wgkf_ref, bgkf_ref, wgkb_ref, bgkb_ref,
                 x2_ref, q_ref, k_ref, v_ref, go_ref, gf_ref, gb_ref, ybuf, sem):
    m0 = mod0_ref[0]
    m1 = mod1_ref[0]
    ys, rows = _combine(dst_ref, yb_hbm, ybuf, sem, lrow_ref, gate_ref)
    hs = []
    for rs, y in zip(HALVES, ys):
        x2 = x_ref[rs, :] + m0[5:6] * y
        x2_ref[rs, :] = x2
        hs.append(_modulate(x2, n1_ref[...], m1[0:1], m1[1:2]).astype(BF16))
    for rs, h in zip(HALVES, hs):
        q_ref[rs, :] = jnp.dot(h, wq_ref[...], preferred_element_type=F32) * (GDK ** -0.5)
    for rs, h in zip(HALVES, hs):
        k_ref[rs, :] = jnp.dot(h, wk_ref[...], preferred_element_type=F32)
    for rs, h in zip(HALVES, hs):
        v_ref[rs, :] = jnp.dot(h, wv_ref[...], preferred_element_type=F32).astype(BF16)
    for rs, h in zip(HALVES, hs):
        go_ref[rs, :] = jnp.dot(h, wgo_ref[...], preferred_element_type=F32).astype(BF16)
    gds = [jnp.dot(h, wgd_ref[...], preferred_element_type=F32) for h in hs]

    def pieces(a):
        hi = a.astype(BF16)
        return hi, (a - hi.astype(F32)).astype(BF16)

    for w_ref, b_ref, g_ref in ((wgkf_ref, bgkf_ref, gf_ref), (wgkb_ref, bgkb_ref, gb_ref)):
        w_hi, w_lo = pieces(w_ref[...])
        for rs, gd in zip(HALVES, gds):
            gd_hi, gd_lo = pieces(gd)
            z = (jnp.dot(gd_hi, w_hi, preferred_element_type=F32) + jnp.dot(gd_hi, w_lo, preferred_element_type=F32)
                 + jnp.dot(gd_lo, w_hi, preferred_element_type=F32) + b_ref[...])
            g_ref[rs, :] = (jnp.minimum(z, 0.0) - jnp.log(1.0 + jnp.exp(-jnp.abs(z)))) / GNORM
    rows.drain()


def odd_in(dst, x1, yb, lrow, gate, mod0, mod1, n1g, wq, wk, wv, wgo, wgd, wgkf, bgkf, wgkb, bgkb):
    full = lambda w: pl.BlockSpec(w.shape, lambda i, d: (0,) * w.ndim)
    row = lambda w: pl.BlockSpec((TILE, w), lambda i, d: (i, 0))
    modspec = pl.BlockSpec((1, 6, D), lambda i, d: (_mod_index(i), 0, 0))
    grid_spec = pltpu.PrefetchScalarGridSpec(
        num_scalar_prefetch=1,
        grid=(NT_ALL,),
        in_specs=[row(D), pl.BlockSpec(memory_space=pl.ANY), row(LANES), row(LANES), modspec, modspec,
                  full(n1g), full(wq), full(wk), full(wv), full(wgo),
                  full(wgd), full(wgkf), full(bgkf), full(wgkb), full(bgkb)],
        out_specs=[row(D), row(GKW), row(GKW), row(GVW), row(GVW), row(GKW), row(GKW)],
        scratch_shapes=[pltpu.VMEM((2, SROWS, D), BF16), pltpu.SemaphoreType.DMA((2,))],
    )
    return pl.pallas_call(
        _odd_in_body,
        grid_spec=grid_spec,
        out_shape=[jax.ShapeDtypeStruct((T_ALL, D), F32),
                   jax.ShapeDtypeStruct((T_ALL, GKW), F32),
                   jax.ShapeDtypeStruct((T_ALL, GKW), F32),
                   jax.ShapeDtypeStruct((T_ALL, GVW), BF16),
                   jax.ShapeDtypeStruct((T_ALL, GVW), BF16),
                   jax.ShapeDtypeStruct((T_ALL, GKW), F32),
                   jax.ShapeDtypeStruct((T_ALL, GKW), F32)],
        compiler_params=_cp(("arbitrary",), VMEM_BIG),
        name="odd_in",
    )(dst, x1, yb, lrow, gate, mod0, mod1, n1g, wq, wk, wv, wgo, wgd, wgkf, bgkf, wgkb, bgkb)


def _gla_body(qf_ref, kf_ref, vf_ref, gf_ref, qb_ref, kb_ref, vb_ref, gb_ref, of_ref, ob_ref, sf_ref, sb_ref):
    @pl.when(pl.program_id(1) == 0)
    def _():
        sf_ref[...] = jnp.zeros_like(sf_ref)
        sb_ref[...] = jnp.zeros_like(sb_ref)

    r = lax.broadcasted_iota(jnp.int32, (GCHUNK, GCHUNK), 0)
    c = lax.broadcasted_iota(jnp.int32, (GCHUNK, GCHUNK), 1)
    dirs = []
    for s in range(GBS):
        at = lambda ref: ref.at[0, s]
        dirs.append((at(qf_ref), at(kf_ref), at(vf_ref), at(gf_ref), sf_ref.at[s], at(of_ref), c <= r, GCHUNK - 1))
        dirs.append((at(qb_ref), at(kb_ref), at(vb_ref), at(gb_ref), sb_ref.at[s], at(ob_ref), c >= r, 0))
    chains = []
    for q_ref, k_ref, v_ref, g_ref, s_ref, o_ref, mask, last in dirs:
        g = g_ref[...]
        g_hi = g.astype(BF16)
        g_lo = (g - g_hi.astype(F32)).astype(BF16)
        tri = mask.astype(BF16)
        cum = (jnp.dot(tri, g_hi, preferred_element_type=F32) + jnp.dot(tri, g_lo, preferred_element_type=F32))
        tot = cum[last:last + 1, :]
        q = q_ref[...]
        k = k_ref[...]
        qd = (q * jnp.exp(cum)).astype(BF16)
        kd = (k * jnp.exp(-cum)).astype(BF16)
        ke = (k * jnp.exp(tot - cum)).astype(BF16)
        dec = jnp.exp(tot)
        v = v_ref[...]
        for hd in range(GH):
            ks = slice(hd * GDK, (hd + 1) * GDK)
            vs = slice(hd * GDV, (hd + 1) * GDV)
            chains.append(dict(qd=qd[:, ks], kd=kd[:, ks], ke=ke[:, ks], dec=dec[:, ks], v=v[:, vs],
                               st=s_ref[hd], mask=mask))
    for ch in chains:
        sc = lax.dot_general(ch["qd"], ch["kd"], NT_DIMS, preferred_element_type=F32)
        ch["sc"] = jnp.where(ch["mask"], sc, 0.0).astype(BF16)
    for ch in chains:
        ch["out"] = (lax.dot_general(ch["qd"], ch["st"].astype(BF16), NT_DIMS, preferred_element_type=F32)
                     + jnp.dot(ch["sc"], ch["v"], preferred_element_type=F32))
    for ch in chains:
        ch["new"] = ch["st"] * ch["dec"] + lax.dot_general(ch["v"], ch["ke"], TN_DIMS, preferred_element_type=F32)
    for d, (_, _, _, _, s_ref, o_ref, _, _) in enumerate(dirs):
        o_ref[...] = jnp.concatenate([chains[d * GH + hd]["out"] for hd in range(GH)], axis=-1)
        for hd in range(GH):
            s_ref[hd] = chains[d * GH + hd]["new"]


def _bwd_chunk(c):
    return jnp.where(c < NCH_CTX, NCH_CTX - 1 - c, NCH_B + NCH_CTX - 1 - c)


def gla(q, k, v, gf, gb):
    fw = lambda w: pl.BlockSpec((1, GBS, GCHUNK, w), lambda b, c: (b, 0, c, 0))
    bw = lambda w: pl.BlockSpec((1, GBS, GCHUNK, w), lambda b, c: (b, 0, _bwd_chunk(c), 0))
    grouped = lambda a: a.reshape(B // GBS, GBS, ROWS_B, a.shape[-1])
    q, k, v, gf, gb = (grouped(a) for a in (q, k, v, gf, gb))
    oshape = jax.ShapeDtypeStruct((B // GBS, GBS, ROWS_B, GVW), F32)
    o_f, o_b = pl.pallas_call(
        _gla_body,
        grid=(B // GBS, NCH_B),
        in_specs=[fw(GKW), fw(GKW), fw(GVW), fw(GKW), bw(GKW), bw(GKW), bw(GVW), bw(GKW)],
        out_specs=[fw(GVW), bw(GVW)],
        out_shape=[oshape, oshape],
        scratch_shapes=[pltpu.VMEM((GBS, GH, GDV, GDK), F32), pltpu.VMEM((GBS, GH, GDV, GDK), F32)],
        compiler_params=_cp(("arbitrary", "arbitrary")),
        name="gla",
    )(q, k, v, gf, q, k, v, gb)
    return o_f.reshape(T_ALL, GVW), o_b.reshape(T_ALL, GVW)


def _out1_body(of_ref, ob_ref, go_ref, x_ref, mod_ref, gn_ref, wo_ref, n2_ref, rw_ref, rb_ref,
               x3_ref, lrow_ref, gate_ref, cnt_ref, xs_ref):
    m = mod_ref[0]
    o = of_ref[...] + ob_ref[...]
    gn = gn_ref[...]
    parts = [_rms(o[:, hd * GDV:(hd + 1) * GDV], gn) for hd in range(GH)]
    on = jnp.concatenate(parts, axis=-1) * _silu(go_ref[...].astype(F32))
    y = jnp.dot(on.astype(BF16), wo_ref[...], preferred_element_type=F32)
    x3 = x_ref[...] + m[2:3] * y
    x3_ref[...] = x3
    h2 = _modulate(x3, n2_ref[...], m[3:4], m[4:5])
    _route(h2, rw_ref, rb_ref, lrow_ref, gate_ref, cnt_ref, xs_ref)


def out1(of, ob, go, x2, mod, gng, wo, n2g, rw, rb):
    full = lambda w: pl.BlockSpec(w.shape, lambda n: (0,) * w.ndim)
    lat = lambda w: pl.BlockSpec((TILE, w), lambda n: (_lat_tile(n), 0))
    rspecs, rshapes = _route_out_specs(NT_LAT)
    return pl.pallas_call(
        _out1_body,
        grid=(NT_LAT,),
        in_specs=[lat(GVW), lat(GVW), lat(GVW), lat(D),
                  pl.BlockSpec((1, 6, D), lambda n: (n // LPB, 0, 0)),
                  full(gng), full(wo), full(n2g), full(rw), full(rb)],
        out_specs=[pl.BlockSpec((TILE, D), lambda n: (n, 0))] + rspecs,
        out_shape=[jax.ShapeDtypeStruct((T_LAT, D), F32)] + rshapes,
        compiler_params=_cp(("arbitrary",)),
        name="out1",
    )(of, ob, go, x2, mod, gng, wo, n2g, rw, rb)


def _final_body(dst_ref, x_ref, yb_hbm, lrow_ref, gate_ref, mod_ref, g_ref, o_ref, ybuf, sem):
    m = mod_ref[0]
    ys, rows = _combine(dst_ref, yb_hbm, ybuf, sem, lrow_ref, gate_ref)
    for rs, y in zip(HALVES, ys):
        o_ref[rs, :] = _rms(x_ref[rs, :] + m[5:6] * y, g_ref[...])
    rows.drain()


def final(dst, x3, yb, lrow, gate, mod, g):
    grid_spec = pltpu.PrefetchScalarGridSpec(
        num_scalar_prefetch=1,
        grid=(NT_LAT,),
        in_specs=[pl.BlockSpec((TILE, D), lambda n, d: (n, 0)),
                  pl.BlockSpec(memory_space=pl.ANY),
                  pl.BlockSpec((TILE, LANES), lambda n, d: (n, 0)),
                  pl.BlockSpec((TILE, LANES), lambda n, d: (n, 0)),
                  pl.BlockSpec((1, 6, D), lambda n, d: (n // LPB, 0, 0)),
                  pl.BlockSpec((1, D), lambda n, d: (0, 0))],
        out_specs=pl.BlockSpec((TILE, D), lambda n, d: (n, 0)),
        scratch_shapes=[pltpu.VMEM((2, SROWS, D), BF16), pltpu.SemaphoreType.DMA((2,))],
    )
    return pl.pallas_call(
        _final_body,
        grid_spec=grid_spec,
        out_shape=jax.ShapeDtypeStruct((T_LAT, D), F32),
        compiler_params=_cp(("arbitrary",)),
        name="final",
    )(dst, x3, yb, lrow, gate, mod, g)


def _rot_cols(w):
    a, b, c, d = jnp.split(w, 4, axis=-1)
    return jnp.concatenate([-b, a, -d, c], axis=-1)


def _rope_tables():
    n = np.arange(SEQ)
    row = (n // GRID_W).astype(np.float32)
    col = (n % GRID_W).astype(np.float32)
    axis_dim = ROPE // 2
    inv = (ROPE_BASE ** (-np.arange(0, axis_dim, 2, dtype=np.float32) / axis_dim)).astype(np.float32)
    ar = row[:, None] * inv
    ac = col[:, None] * inv
    ang = np.concatenate([ar, ar, ac, ac], axis=-1).astype(np.float32)
    cos = np.concatenate([np.ones((CTX, ROPE), np.float32), np.cos(ang)], axis=0)
    sin = np.concatenate([np.zeros((CTX, ROPE), np.float32), np.sin(ang)], axis=0)
    return jnp.asarray(np.tile(cos, (1, HEADS))), jnp.asarray(np.tile(sin, (1, HEADS)))


def _dft(n, scale):
    j = np.arange(n, dtype=np.int64)
    ang = 2.0 * np.pi * ((j[:, None] * j[None, :]) % n).astype(np.float64) / n
    return (np.cos(ang) * scale), (np.sin(ang) * scale)


def _dft_tables():
    cn, sn = _dft(SEQ, SEQ ** -0.5)
    c2, s2 = _dft(CTX, CTX ** -0.5)
    cg, sg = _dft(FGD, FGD ** -0.5)
    eye = np.eye(FG)
    to = lambda a: jnp.asarray(a.astype(np.float32)).astype(BF16)
    return to(cn), to(sn), to(c2), to(s2), to(np.kron(eye, cg)), to(np.kron(eye, sg))


def _router_pad(rw, rb):
    rwp = jnp.zeros((LANES, D), F32).at[:NE].set(rw.T)
    rbp = jnp.full((LANES, 1), -1e30, F32).at[:NE, 0].set(rb)
    return rwp, rbp


def kernel(x, c, ctx, c_ctx, final_norm_g, l0_mod_w, l0_mod_b, l0_norm1_g, l0_w_in, l0_q_norm_g, l0_w_q_up, l0_kv_norm_g, l0_w_kv_up, l0_w_out, l0_norm2_g, l0_router_w, l0_router_b, l0_w_gu, l0_b_gu, l0_w_down, l0_b_down, l1_mod_w, l1_mod_b, l1_norm1_g, l1_w_in, l1_w_gk_fwd, l1_b_gk_fwd, l1_w_gk_bwd, l1_b_gk_bwd, l1_gnorm_g, l1_w_out, l1_norm2_g, l1_router_w, l1_router_b, l1_w_gu, l1_b_gu, l1_w_down, l1_b_down):
    row = lambda g: g.reshape(1, -1)
    c2 = ctx.reshape(B * CTX, D)
    x2 = x.reshape(T_LAT, D)
    cc = jnp.zeros((16, D), F32).at[:B].set(c).at[B].set(c_ctx)
    mod0 = adaln(cc, l0_mod_w, l0_mod_b)
    mod1 = adaln(cc, l1_mod_w, l1_mod_b)

    w_in = l0_w_in
    wqa = w_in[:, :Q_LORA].astype(BF16)
    wkva = w_in[:, Q_LORA:Q_LORA + KV_LORA].astype(BF16)
    wkr = w_in[:, Q_LORA + KV_LORA:Q_LORA + KV_LORA + ROPE]
    wkr2 = jnp.concatenate([wkr, _rot_cols(wkr)], axis=-1).astype(BF16)
    wf = w_in[:, Q_LORA + KV_LORA + ROPE:].astype(BF16)
    wq3 = l0_w_q_up.reshape(Q_LORA, HEADS, QK)
    wq_nope = wq3[:, :, :NOPE].reshape(Q_LORA, HEADS * NOPE)
    wq_rope = wq3[:, :, NOPE:]
    wqup = jnp.concatenate([wq_nope, wq_rope.reshape(Q_LORA, HEADS * ROPE),
                            _rot_cols(wq_rope).reshape(Q_LORA, HEADS * ROPE)], axis=-1).astype(BF16)
    wkv3 = l0_w_kv_up.reshape(KV_LORA, HEADS, NOPE + VDIM)
    wkvup = jnp.concatenate([wkv3[:, :, :NOPE].reshape(KV_LORA, HEADS * NOPE),
                             wkv3[:, :, NOPE:].reshape(KV_LORA, HEADS * VDIM)], axis=-1).astype(BF16)
    cos6, sin6 = _rope_tables()
    q, k, v, uf = even_in(c2, x2, mod0, row(l0_norm1_g), wqa, wkva, wkr2, wf, row(l0_q_norm_g), wqup,
                          row(l0_kv_norm_g), wkvup, cos6, sin6)
    att = attention(q, k, v).reshape(T_ALL, HEADS * VDIM)
    fmix = fourier(uf.reshape(B, ROWS_B, FW), *_dft_tables()).reshape(T_ALL, FW)
    rw0, rb0 = _router_pad(l0_router_w, l0_router_b)
    x1, lrow0, gate0, cnt0, xs0 = out0(c2, x2, att, fmix, mod0, l0_w_out[:HEADS * VDIM].astype(BF16),
                                       l0_w_out[HEADS * VDIM:].astype(BF16), row(l0_norm2_g), rw0, rb0)
    src0, be0, bv0, dst0, nb0 = _chunk_lists(cnt0, NT_ALL)
    yb0 = expert_ffn(xs0, src0, be0, bv0, nb0, l0_w_gu, l0_b_gu, l0_w_down, l0_b_down)

    w1 = l1_w_in
    o = 0
    wq1 = w1[:, o:o + GKW].astype(BF16); o += GKW
    wk1 = w1[:, o:o + GKW].astype(BF16); o += GKW
    wv1 = w1[:, o:o + GVW].astype(BF16); o += GVW
    wgo = w1[:, o:o + GVW].astype(BF16); o += GVW
    wgd = jnp.zeros((D, LANES), F32).at[:, :2 * GRANK].set(w1[:, o:]).astype(BF16)
    wgkf = jnp.zeros((LANES, GKW), F32).at[:GRANK].set(l1_w_gk_fwd)
    wgkb = jnp.zeros((LANES, GKW), F32).at[GRANK:2 * GRANK].set(l1_w_gk_bwd)
    x2, q1, k1, v1, go, gf, gb = odd_in(dst0, x1, yb0, lrow0, gate0, mod0, mod1, row(l1_norm1_g), wq1, wk1, wv1,
                                        wgo, wgd, wgkf, row(l1_b_gk_fwd), wgkb, row(l1_b_gk_bwd))
    o_f, o_b = gla(q1, k1, v1, gf, gb)
    rw1, rb1 = _router_pad(l1_router_w, l1_router_b)
    x3, lrow1, gate1, cnt1, xs1 = out1(o_f, o_b, go, x2, mod1, row(l1_gnorm_g), l1_w_out.astype(BF16),
                                       row(l1_norm2_g), rw1, rb1)
    src1, be1, bv1, dst1, nb1 = _chunk_lists(cnt1, NT_LAT)
    yb1 = expert_ffn(xs1, src1, be1, bv1, nb1, l1_w_gu, l1_b_gu, l1_w_down, l1_b_down)
    out = final(dst1, x3, yb1, lrow1, gate1, mod1, row(final_norm_g))
    return out.reshape(B, SEQ, D)
```

```python
import jax
import jax.numpy as jnp
import numpy as np
from jax import lax
from jax.experimental import pallas as pl
from jax.experimental.pallas import tpu as pltpu

D = 1024
B = 8
SEQ = 4096
CTX = 256
GRID_W = 64
EPS = 1e-6
HEADS = 6
Q_LORA = 384
KV_LORA = 256
NOPE = 128
ROPE = 64
VDIM = 128
QK = NOPE + ROPE
ROPE_BASE = 10000.0
FG = 4
FGD = 64
FW = FG * FGD
GH = 4
GDK = 128
GDV = 256
GRANK = 16
GNORM = 16.0
GCHUNK = 64
GKW = GH * GDK
GVW = GH * GDV
NE = 32
TOPK = 4
DFF = 1024
ALPHA = 1.702
LIMIT = 7.0

LANES = 128
TILE = 256
HALF = TILE // 2
HALVES = (slice(0, HALF), slice(HALF, TILE))
ROWS_B = CTX + SEQ
TPB = ROWS_B // TILE
T_ALL = B * ROWS_B
NT_ALL = T_ALL // TILE
T_LAT = B * SEQ
NT_LAT = T_LAT // TILE
LPB = SEQ // TILE
HPS = 3
CH = 16
MAXCH = (TILE * TOPK + NE * (CH - 1)) // CH
NCHT = -(-MAXCH * CH // LANES) * LANES // CH
SROWS = NCHT * CH
SGRP = 512
FBLK = 512
CPB = FBLK // CH
GBS = 8
NCH_B = ROWS_B // GCHUNK
NCH_CTX = CTX // GCHUNK
V7X_VMEM_BYTES = 64 * 1024 * 1024
VMEM_BIG = V7X_VMEM_BYTES * 7 // 8
VMEM_MID = V7X_VMEM_BYTES * 5 // 8

F32 = jnp.float32
BF16 = jnp.bfloat16
HI = lax.Precision.HIGHEST
NT_DIMS = (((1,), (1,)), ((), ()))
TN_DIMS = (((0,), (0,)), ((), ()))


def _cp(sem, vmem=VMEM_MID):
    return pltpu.CompilerParams(dimension_semantics=sem, vmem_limit_bytes=vmem)


def _rms(x, g):
    return x * lax.rsqrt(jnp.mean(x * x, axis=-1, keepdims=True) + EPS) * g


def _modulate(x, g, sh, sc):
    return _rms(x, g) * (1.0 + sc) + sh


def _silu(x):
    return x * jax.nn.sigmoid(x)


def _mod_index(i):
    return jnp.where(i % TPB == 0, B, i // TPB)


def _lat_tile(n):
    return n + n // LPB + 1


def _adaln_body(c_ref, w_ref, b_ref, o_ref):
    s = _silu(c_ref[...])
    o_ref[...] = jnp.dot(s, w_ref[...], precision=HI, preferred_element_type=F32) + b_ref[...]


def adaln(cc, mod_w, mod_b):
    tn = 1536
    out = pl.pallas_call(
        _adaln_body,
        grid=(6 * D // tn,),
        in_specs=[pl.BlockSpec((16, D), lambda j: (0, 0)),
                  pl.BlockSpec((D, tn), lambda j: (0, j)),
                  pl.BlockSpec((1, tn), lambda j: (0, j))],
        out_specs=pl.BlockSpec((16, tn), lambda j: (0, j)),
        out_shape=jax.ShapeDtypeStruct((16, 6 * D), F32),
        compiler_params=_cp(("arbitrary",)),
        name="adaln",
    )(cc, mod_w, mod_b.reshape(1, 6 * D))
    return out.reshape(16, 6, D)


def _stream_tile(c_ref, x_ref):
    return jnp.where(pl.program_id(0) % TPB == 0, c_ref[...], x_ref[...])


def _stream_specs():
    return [pl.BlockSpec((TILE, D), lambda i: (i // TPB, 0)),
            pl.BlockSpec((TILE, D), lambda i: ((i // TPB) * LPB + jnp.maximum(i % TPB - 1, 0), 0))]


def _even_in_body(c_ref, x_ref, mod_ref, n1_ref, wqa_ref, wkva_ref, wkr_ref, wf_ref, qg_ref, wqup_ref,
                  kvg_ref, wkvup_ref, cos_ref, sin_ref, q_ref, k_ref, v_ref, uf_ref):
    m = mod_ref[0]
    h = _modulate(_stream_tile(c_ref, x_ref), n1_ref[...], m[0:1], m[1:2]).astype(BF16)
    cq = jnp.dot(h, wqa_ref[...], preferred_element_type=F32)
    ckv = jnp.dot(h, wkva_ref[...], preferred_element_type=F32)
    kr2 = jnp.dot(h, wkr_ref[...], preferred_element_type=F32)
    uf_ref[...] = jnp.dot(h, wf_ref[...], preferred_element_type=F32).astype(BF16)
    cos6 = cos_ref[...]
    sin6 = sin_ref[...]
    scale = QK ** -0.5
    qall = jnp.dot(_rms(cq, qg_ref[...]).astype(BF16), wqup_ref[...], preferred_element_type=F32)
    qn = qall[:, :HEADS * NOPE] * scale
    qr = (qall[:, HEADS * NOPE:HEADS * QK] * cos6 + qall[:, HEADS * QK:] * sin6) * scale
    kvall = jnp.dot(_rms(ckv, kvg_ref[...]).astype(BF16), wkvup_ref[...], preferred_element_type=F32)
    kr = (kr2[:, :ROPE] * cos6[:, :ROPE] + kr2[:, ROPE:] * sin6[:, :ROPE]).astype(BF16)
    ones_col = (lax.broadcasted_iota(jnp.int32, (TILE, VDIM), 1) == 0).astype(BF16)
    for hd in range(HEADS):
        q_ref[0, hd, :, 0:NOPE] = qn[:, hd * NOPE:(hd + 1) * NOPE].astype(BF16)
        q_ref[0, hd, :, NOPE:QK] = qr[:, hd * ROPE:(hd + 1) * ROPE].astype(BF16)
        k_ref[0, hd, :, 0:NOPE] = kvall[:, hd * NOPE:(hd + 1) * NOPE].astype(BF16)
        k_ref[0, hd, :, NOPE:QK] = kr
        v_ref[0, hd, :, 0:VDIM] = kvall[:, HEADS * NOPE + hd * VDIM:HEADS * NOPE + (hd + 1) * VDIM].astype(BF16)
        v_ref[0, hd, :, VDIM:2 * VDIM] = ones_col


def even_in(c2, x2, mod, n1g, wqa, wkva, wkr2, wf, qg, wqup, kvg, wkvup, cos6, sin6):
    full = lambda a: pl.BlockSpec(a.shape, lambda i: (0,) * a.ndim)
    hs = lambda w: pl.BlockSpec((1, HEADS, TILE, w), lambda i: (i // TPB, 0, i % TPB, 0))
    return pl.pallas_call(
        _even_in_body,
        grid=(NT_ALL,),
        in_specs=_stream_specs() + [
                  pl.BlockSpec((1, 6, D), lambda i: (_mod_index(i), 0, 0)),
                  full(n1g), full(wqa), full(wkva), full(wkr2), full(wf), full(qg), full(wqup),
                  full(kvg), full(wkvup),
                  pl.BlockSpec((TILE, HEADS * ROPE), lambda i: (i % TPB, 0)),
                  pl.BlockSpec((TILE, HEADS * ROPE), lambda i: (i % TPB, 0))],
        out_specs=[hs(QK), hs(QK), hs(2 * VDIM), pl.BlockSpec((TILE, FW), lambda i: (i, 0))],
        out_shape=[jax.ShapeDtypeStruct((B, HEADS, ROWS_B, QK), BF16),
                   jax.ShapeDtypeStruct((B, HEADS, ROWS_B, QK), BF16),
                   jax.ShapeDtypeStruct((B, HEADS, ROWS_B, 2 * VDIM), BF16),
                   jax.ShapeDtypeStruct((T_ALL, FW), BF16)],
        compiler_params=_cp(("arbitrary",)),
        name="even_in",
    )(c2, x2, mod, n1g, wqa, wkva, wkr2, wf, qg, wqup, kvg, wkvup, cos6, sin6)


def _attend(q_ref, k_ref, v_ref, o_ref, rows):
    ks = [k_ref[0, j, 0:rows, :] for j in range(HPS)]
    ss = [lax.dot_general(q_ref[0, j], ks[j], NT_DIMS, preferred_element_type=F32) for j in range(HPS)]
    ps = [jnp.exp(s - jnp.max(s, axis=-1, keepdims=True)).astype(BF16) for s in ss]
    ov = [jnp.dot(ps[j], v_ref[0, j, 0:rows, :], preferred_element_type=F32) for j in range(HPS)]
    o_ref[0] = jnp.concatenate([o[:, :VDIM] / o[:, VDIM:VDIM + 1] for o in ov], axis=-1).astype(BF16)


def _attn_body(q_ref, k_ref, v_ref, o_ref):
    qt = pl.program_id(2)

    @pl.when(qt == 0)
    def _():
        _attend(q_ref, k_ref, v_ref, o_ref, CTX)

    @pl.when(qt > 0)
    def _():
        _attend(q_ref, k_ref, v_ref, o_ref, ROWS_B)


def attention(q, k, v):
    return pl.pallas_call(
        _attn_body,
        grid=(B, HEADS // HPS, TPB),
        in_specs=[pl.BlockSpec((1, HPS, TILE, QK), lambda b, h, t: (b, h, t, 0)),
                  pl.BlockSpec((1, HPS, ROWS_B, QK), lambda b, h, t: (b, h, 0, 0)),
                  pl.BlockSpec((1, HPS, ROWS_B, 2 * VDIM), lambda b, h, t: (b, h, 0, 0))],
        out_specs=pl.BlockSpec((1, TILE, HPS * VDIM), lambda b, h, t: (b, t, h)),
        out_shape=jax.ShapeDtypeStruct((B, ROWS_B, HEADS * VDIM), BF16),
        compiler_params=_cp(("arbitrary", "arbitrary", "arbitrary"), VMEM_BIG),
        name="attention",
    )(q, k, v)


def _fourier_body(u_ref, cn_ref, sn_ref, c2_ref, s2_ref, cc_ref, sc_ref, o_ref):
    i = pl.program_id(0)

    def finish(a, b):
        o_ref[0] = (jnp.dot(a.astype(BF16), cc_ref[...], preferred_element_type=F32)
                    - jnp.dot(b.astype(BF16), sc_ref[...], preferred_element_type=F32)).astype(BF16)

    @pl.when(i == 0)
    def _():
        u = u_ref[0, 0:CTX, :]
        finish(jnp.dot(c2_ref[...], u, preferred_element_type=F32),
               jnp.dot(s2_ref[...], u, preferred_element_type=F32))

    @pl.when(i > 0)
    def _():
        u = u_ref[0, CTX:ROWS_B, :]
        finish(jnp.dot(cn_ref[...], u, preferred_element_type=F32),
               jnp.dot(sn_ref[...], u, preferred_element_type=F32))


def fourier(uf, cn, sn, c2, s2, ccb, scb):
    full = lambda a: pl.BlockSpec(a.shape, lambda i, b: (0,) * a.ndim)
    return pl.pallas_call(
        _fourier_body,
        grid=(TPB, B),
        in_specs=[pl.BlockSpec((1, ROWS_B, FW), lambda i, b: (b, 0, 0)),
                  pl.BlockSpec((TILE, SEQ), lambda i, b: (jnp.maximum(i - 1, 0), 0)),
                  pl.BlockSpec((TILE, SEQ), lambda i, b: (jnp.maximum(i - 1, 0), 0)),
                  full(c2), full(s2), full(ccb), full(scb)],
        out_specs=pl.BlockSpec((1, TILE, FW), lambda i, b: (b, i, 0)),
        out_shape=jax.ShapeDtypeStruct((B, ROWS_B, FW), BF16),
        compiler_params=_cp(("arbitrary", "arbitrary")),
        name="fourier",
    )(uf, cn, sn, c2, s2, ccb, scb)


def _route(h2, rw_ref, rb_ref, lrow_ref, gate_ref, cnt_ref, xs_ref):
    h_hi = h2.astype(BF16)
    h_lo = (h2 - h_hi.astype(F32)).astype(BF16)
    rw = rw_ref[...]
    w_hi = rw.astype(BF16)
    w_lo = (rw - w_hi.astype(F32)).astype(BF16)
    both = lax.dot_general(jnp.concatenate([w_hi, w_lo], axis=0), h_hi, NT_DIMS, preferred_element_type=F32)
    l = (both[:LANES] + both[LANES:]
         + lax.dot_general(w_hi, h_lo, NT_DIMS, preferred_element_type=F32) + rb_ref[...])
    sub = lax.broadcasted_iota(jnp.int32, l.shape, 0)
    vals, idxs = [], []
    for _ in range(TOPK):
        mx = jnp.max(l, axis=0, keepdims=True)
        am = jnp.min(jnp.where(l == mx, sub, LANES), axis=0, keepdims=True)
        vals.append(mx)
        idxs.append(am)
        l = jnp.where(sub == am, -jnp.inf, l)
    ex = [jnp.exp(vv - vals[0]) for vv in vals]
    den = ex[0] + ex[1] + ex[2] + ex[3]
    onehot = jnp.zeros(l.shape, F32)
    for am in idxs:
        onehot = onehot + (sub == am).astype(F32)
    r = lax.broadcasted_iota(jnp.int32, (TILE, TILE), 0)
    c = lax.broadcasted_iota(jnp.int32, (TILE, TILE), 1)
    rank = jnp.dot(onehot.astype(BF16), (r < c).astype(BF16), preferred_element_type=F32)
    cnt = jnp.sum(onehot, axis=1, keepdims=True)
    nch = jnp.ceil(cnt * (1.0 / CH))
    er = lax.broadcasted_iota(jnp.int32, (LANES, LANES), 0)
    ec = lax.broadcasted_iota(jnp.int32, (LANES, LANES), 1)
    seg = jnp.dot((ec < er).astype(BF16), jnp.broadcast_to(nch, (LANES, LANES)).astype(BF16),
                  preferred_element_type=F32)[:, 0:1] * CH
    dest = rank + seg
    lrow_t = jnp.zeros(l.shape, F32)
    gate_t = jnp.zeros(l.shape, F32)
    rows = []
    for kk in range(TOPK):
        pos = jnp.sum(jnp.where(sub == idxs[kk], dest, 0.0), axis=0, keepdims=True)
        rows.append(pos.astype(jnp.int32))
        lrow_t = jnp.where(sub == kk, pos, lrow_t)
        gate_t = jnp.where(sub == kk, ex[kk] / den, gate_t)
    lrow_ref[...] = lrow_t.T.astype(jnp.int32)
    gate_ref[...] = gate_t.T
    cnt_ref[0] = jnp.broadcast_to(cnt, (LANES, LANES)).T[0:8, :]
    for g in range(SROWS // SGRP):
        srow = g * SGRP + lax.broadcasted_iota(jnp.int32, (SGRP, TILE), 0)
        hit = rows[0] == srow
        for kk in range(1, TOPK):
            hit = jnp.logical_or(hit, rows[kk] == srow)
        xs_ref[g * SGRP:(g + 1) * SGRP, :] = jnp.dot(hit.astype(BF16), h_hi,
                                                    preferred_element_type=F32).astype(BF16)


def _route_out_specs(n_tiles):
    specs = [pl.BlockSpec((TILE, LANES), lambda i: (i, 0)),
             pl.BlockSpec((TILE, LANES), lambda i: (i, 0)),
             pl.BlockSpec((1, 8, LANES), lambda i: (i, 0, 0)),
             pl.BlockSpec((SROWS, D), lambda i: (i, 0))]
    shapes = [jax.ShapeDtypeStruct((n_tiles * TILE, LANES), jnp.int32),
              jax.ShapeDtypeStruct((n_tiles * TILE, LANES), F32),
              jax.ShapeDtypeStruct((n_tiles, 8, LANES), F32),
              jax.ShapeDtypeStruct((n_tiles * SROWS, D), BF16)]
    return specs, shapes


def _out0_body(c_ref, x_ref, a_ref, f_ref, mod_ref, woa_ref, wof_ref, n2_ref, rw_ref, rb_ref,
               x1_ref, lrow_ref, gate_ref, cnt_ref, xs_ref):
    m = mod_ref[0]
    y = (jnp.dot(a_ref[...], woa_ref[...], preferred_element_type=F32)
         + jnp.dot(f_ref[...], wof_ref[...], preferred_element_type=F32))
    x1 = _stream_tile(c_ref, x_ref) + m[2:3] * y
    x1_ref[...] = x1
    h2 = _modulate(x1, n2_ref[...], m[3:4], m[4:5])
    _route(h2, rw_ref, rb_ref, lrow_ref, gate_ref, cnt_ref, xs_ref)


def out0(c2, x2, a, f, mod, woa, wof, n2g, rw, rb):
    full = lambda w: pl.BlockSpec(w.shape, lambda i: (0,) * w.ndim)
    rspecs, rshapes = _route_out_specs(NT_ALL)
    return pl.pallas_call(
        _out0_body,
        grid=(NT_ALL,),
        in_specs=_stream_specs() + [
                  pl.BlockSpec((TILE, HEADS * VDIM), lambda i: (i, 0)),
                  pl.BlockSpec((TILE, FW), lambda i: (i, 0)),
                  pl.BlockSpec((1, 6, D), lambda i: (_mod_index(i), 0, 0)),
                  full(woa), full(wof), full(n2g), full(rw), full(rb)],
        out_specs=[pl.BlockSpec((TILE, D), lambda i: (i, 0))] + rspecs,
        out_shape=[jax.ShapeDtypeStruct((T_ALL, D), F32)] + rshapes,
        compiler_params=_cp(("arbitrary",)),
        name="out0",
    )(c2, x2, a, f, mod, woa, wof, n2g, rw, rb)


def _chunk_lists(cnt, n_tiles):
    n_blk = (n_tiles * MAXCH + CPB - 1) // CPB + NE
    cnt = cnt[:, 0, :NE].astype(jnp.int32)
    nch = (cnt + CH - 1) // CH
    lend = jnp.cumsum(nch, axis=1)
    loff = lend - nch
    tend = jnp.cumsum(nch, axis=0)
    toff = tend - nch
    tot = tend[-1]
    blk = (tot + CPB - 1) // CPB
    bend = jnp.cumsum(blk)
    estart = (bend - blk) * CPB
    nvalid = bend[-1]
    bi = jnp.arange(n_blk, dtype=jnp.int32)
    bvalid = bi < nvalid
    count_le = lambda ends, v: jnp.sum((ends[None, :] <= v[:, None]).astype(jnp.int32), axis=1)
    be = jnp.where(bvalid, jnp.minimum(count_le(bend, bi), NE - 1), jnp.sum((bend < nvalid).astype(jnp.int32)))
    g = jnp.arange(n_blk * CPB, dtype=jnp.int32)
    sel = (jnp.repeat(be, CPB)[:, None] == jnp.arange(NE, dtype=jnp.int32)[None, :]).astype(F32)
    pick = lambda table: jnp.dot(sel, table.astype(F32), precision=HI).astype(jnp.int32)
    o = g - pick(estart[:, None])[:, 0]
    gvalid = jnp.logical_and(jnp.repeat(bvalid, CPB), o < pick(tot[:, None])[:, 0])
    tend_g = pick(tend.T)
    tile = jnp.minimum(jnp.sum((tend_g <= o[:, None]).astype(jnp.int32), axis=1), n_tiles - 1)
    tsel = tile[:, None] == jnp.arange(n_tiles, dtype=jnp.int32)[None, :]
    shift = jnp.sum(jnp.where(tsel, pick((loff - toff).T), 0), axis=1)
    src = jnp.where(gvalid, tile * NCHT + o + shift, 0).astype(jnp.int32)
    cslot = jnp.arange(NCHT, dtype=jnp.int32)
    inside = jnp.logical_and(loff[:, None, :] <= cslot[None, :, None], cslot[None, :, None] < lend[:, None, :])
    base = estart[None, :] + toff - loff
    dst = cslot[None, :] + jnp.sum(jnp.where(inside, base[:, None, :], 0), axis=2)
    dst = jnp.where(cslot[None, :] < lend[:, -1:], dst, 0).astype(jnp.int32)
    src = jnp.concatenate([src, jnp.zeros((CPB,), jnp.int32)])
    dst = jnp.concatenate([dst.reshape(-1), jnp.zeros((NCHT,), jnp.int32)])
    return src, be.astype(jnp.int32), bvalid.astype(jnp.int32), dst, n_blk


def _chunk_copy(src_hbm, chunk, buf, slot, j, sem):
    start = chunk * CH if isinstance(chunk, int) else pl.multiple_of(chunk * CH, CH)
    return pltpu.make_async_copy(src_hbm.at[pl.ds(start, CH), :],
                                 buf.at[slot, pl.ds(j * CH, CH), :], sem.at[slot])


class _ChunkGather:
    def __init__(self, list_ref, per_step, src_hbm, buf, sem):
        self.list_ref, self.per_step, self.src_hbm, self.buf, self.sem = list_ref, per_step, src_hbm, buf, sem
        self.step = pl.program_id(0)
        self.n_steps = pl.num_programs(0)

    def _issue(self, st, slot):
        for j in range(self.per_step):
            _chunk_copy(self.src_hbm, self.list_ref[st * self.per_step + j], self.buf, slot, j, self.sem).start()

    def _wait(self, slot):
        for j in range(self.per_step):
            _chunk_copy(self.src_hbm, 0, self.buf, slot, j, self.sem).wait()

    def arrive(self):
        @pl.when(self.step == 0)
        def _():
            self._issue(0, 0)

        slot = self.step % 2
        self._wait(slot)
        return slot

    def fetch_next(self):
        self._issue(self.step + 1, (self.step + 1) % 2)

    def drain(self):
        @pl.when(self.step == self.n_steps - 1)
        def _():
            self._wait((self.step + 1) % 2)


def _ffn_body(src_ref, be_ref, bv_ref, bf_ref, ws_ref, ne_ref, xs_hbm, wgu_hbm, bg_ref, bl_ref, wdn_hbm, bd_ref,
              yb_ref, xbuf, sem, wg_s, wl_s, wd_s, wgu_buf, wdn_buf, wsem):
    i = pl.program_id(0)
    rows = _ChunkGather(src_ref, CPB, xs_hbm, xbuf, sem)
    slot = rows.arrive()

    def weight_copies(e, s):
        return (pltpu.make_async_copy(wgu_hbm.at[e], wgu_buf.at[s], wsem.at[0, s]),
                pltpu.make_async_copy(wdn_hbm.at[e], wdn_buf.at[s], wsem.at[1, s]))

    @pl.when(bf_ref[i] == 1)
    def _():
        s = ws_ref[i]

        @pl.when(i == 0)
        def _():
            for cp in weight_copies(be_ref[0], 0):
                cp.start()

        for cp in weight_copies(be_ref[i], s):
            cp.wait()

        @pl.when(ne_ref[i] >= 0)
        def _():
            for cp in weight_copies(ne_ref[i], 1 - s):
                cp.start()

        r = lax.broadcasted_iota(jnp.int32, (2 * LANES, 2 * LANES), 0)
        c = lax.broadcasted_iota(jnp.int32, (2 * LANES, 2 * LANES), 1)
        split = (r == 2 * (c % LANES) + c // LANES).astype(BF16)
        for g in range(DFF // LANES):
            pair = wgu_buf[s, :, pl.ds(g * 2 * LANES, 2 * LANES)].astype(BF16)
            sep = jnp.dot(pair, split, preferred_element_type=F32)
            wg_s[:, g * LANES:(g + 1) * LANES] = sep[:, :LANES].astype(BF16)
            wl_s[:, g * LANES:(g + 1) * LANES] = sep[:, LANES:].astype(BF16)
        wd_s[...] = wdn_buf[s].astype(BF16)

    @pl.when(bv_ref[i] == 1)
    def _():
        rows.fetch_next()
        xb = xbuf[slot]
        glu = jnp.dot(xb, wg_s[...], preferred_element_type=F32) + bg_ref[0]
        lin = jnp.dot(xb, wl_s[...], preferred_element_type=F32) + bl_ref[0]
        glu = jnp.minimum(glu, LIMIT)
        lin = jnp.clip(lin, -LIMIT, LIMIT)
        act = glu * jax.nn.sigmoid(ALPHA * glu) * (lin + 1.0)
        out = jnp.dot(act.astype(BF16), wd_s[...], preferred_element_type=F32) + bd_ref[0]
        yb_ref[...] = out.astype(BF16)

    @pl.when(bv_ref[i] == 0)
    def _():
        rows.fetch_next()
        yb_ref[...] = jnp.zeros_like(yb_ref)

    rows.drain()


def expert_ffn(xs, src, be, bv, n_blk, w_gu, b_gu, w_down, b_down):
    bi = jnp.arange(n_blk, dtype=jnp.int32)
    bf = jnp.concatenate([jnp.ones((1,), jnp.int32), (be[1:] != be[:-1]).astype(jnp.int32)])
    ws = (jnp.cumsum(bf) - 1) % 2
    later_first = jnp.logical_and(bf[None, :] == 1, bi[None, :] > bi[:, None])
    nxt = jnp.min(jnp.where(later_first, bi[None, :], n_blk), axis=1)
    ne = jnp.sum(jnp.where(bi[None, :] == nxt[:, None], be[None, :], 0), axis=1)
    ne = jnp.where(nxt < n_blk, ne, -1)
    bgu = b_gu.reshape(NE, 1, DFF, 2)
    wspec = lambda shp: pl.BlockSpec((1,) + shp, lambda i, src, be, *_: (be[i], 0, 0))
    grid_spec = pltpu.PrefetchScalarGridSpec(
        num_scalar_prefetch=6,
        grid=(n_blk,),
        in_specs=[pl.BlockSpec(memory_space=pl.ANY),
                  pl.BlockSpec(memory_space=pl.ANY), wspec((1, DFF)), wspec((1, DFF)),
                  pl.BlockSpec(memory_space=pl.ANY), wspec((1, D))],
        out_specs=pl.BlockSpec((FBLK, D), lambda i, *_: (i, 0)),
        scratch_shapes=[pltpu.VMEM((2, FBLK, D), BF16), pltpu.SemaphoreType.DMA((2,)),
                        pltpu.VMEM((D, DFF), BF16), pltpu.VMEM((D, DFF), BF16), pltpu.VMEM((DFF, D), BF16),
                        pltpu.VMEM((2, D, 2 * DFF), F32), pltpu.VMEM((2, DFF, D), F32),
                        pltpu.SemaphoreType.DMA((2, 2))],
    )
    return pl.pallas_call(
        _ffn_body,
        grid_spec=grid_spec,
        out_shape=jax.ShapeDtypeStruct((n_blk * FBLK, D), BF16),
        compiler_params=_cp(("arbitrary",), VMEM_BIG),
        name="expert_ffn",
    )(src, be, bv, bf, ws.astype(jnp.int32), ne.astype(jnp.int32),
      xs, w_gu, bgu[..., 0], bgu[..., 1], w_down, b_down.reshape(NE, 1, D))


def _combine(dst_ref, yb_hbm, ybuf, sem, lrow_ref, gate_ref):
    rows = _ChunkGather(dst_ref, NCHT, yb_hbm, ybuf, sem)
    slot = rows.arrive()
    rows.fetch_next()
    scol = lax.broadcasted_iota(jnp.int32, (HALF, SROWS), 1)
    ws = []
    for rs in HALVES:
        lrow = lrow_ref[rs, :]
        gate = gate_ref[rs, :]
        w = jnp.zeros((HALF, SROWS), F32)
        for kk in range(TOPK):
            w = w + jnp.where(lrow[:, kk:kk + 1] == scol, gate[:, kk:kk + 1], 0.0)
        ws.append(w.astype(BF16))
    return [jnp.dot(w, ybuf[slot], preferred_element_type=F32) for w in ws], rows


def _odd_in_body(dst_ref, x_ref, yb_hbm, lrow_ref, gate_ref, mod0_ref, mod1_ref, n1_ref, wq_ref, wk_ref, wv_ref,
                 wgo_ref, wgd_ref, wgkf_ref, bgkf_ref, wgkb_ref, bgkb_ref,
                 x2_ref, q_ref, k_ref, v_ref, go_ref, gf_ref, gb_ref, ybuf, sem):
    m0 = mod0_ref[0]
    m1 = mod1_ref[0]
    ys, rows = _combine(dst_ref, yb_hbm, ybuf, sem, lrow_ref, gate_ref)
    hs = []
    for rs, y in zip(HALVES, ys):
        x2 = x_ref[rs, :] + m0[5:6] * y
        x2_ref[rs, :] = x2
        hs.append(_modulate(x2, n1_ref[...], m1[0:1], m1[1:2]).astype(BF16))
    for rs, h in zip(HALVES, hs):
        q_ref[rs, :] = jnp.dot(h, wq_ref[...], preferred_element_type=F32) * (GDK ** -0.5)
    for rs, h in zip(HALVES, hs):
        k_ref[rs, :] = jnp.dot(h, wk_ref[...], preferred_element_type=F32)
    for rs, h in zip(HALVES, hs):
        v_ref[rs, :] = jnp.dot(h, wv_ref[...], preferred_element_type=F32).astype(BF16)
    for rs, h in zip(HALVES, hs):
        go_ref[rs, :] = jnp.dot(h, wgo_ref[...], preferred_element_type=F32).astype(BF16)
    gds = [jnp.dot(h, wgd_ref[...], preferred_element_type=F32) for h in hs]

    def pieces(a):
        hi = a.astype(BF16)
        return hi, (a - hi.astype(F32)).astype(BF16)

    for w_ref, b_ref, g_ref in ((wgkf_ref, bgkf_ref, gf_ref), (wgkb_ref, bgkb_ref, gb_ref)):
        w_hi, w_lo = pieces(w_ref[...])
        for rs, gd in zip(HALVES, gds):
            gd_hi, gd_lo = pieces(gd)
            z = (jnp.dot(gd_hi, w_hi, preferred_element_type=F32) + jnp.dot(gd_hi, w_lo, preferred_element_type=F32)
                 + jnp.dot(gd_lo, w_hi, preferred_element_type=F32) + b_ref[...])
            g_ref[rs, :] = (jnp.minimum(z, 0.0) - jnp.log(1.0 + jnp.exp(-jnp.abs(z)))) / GNORM
    rows.drain()


def odd_in(dst, x1, yb, lrow, gate, mod0, mod1, n1g, wq, wk, wv, wgo, wgd, wgkf, bgkf, wgkb, bgkb):
    full = lambda w: pl.BlockSpec(w.shape, lambda i, d: (0,) * w.ndim)
    row = lambda w: pl.BlockSpec((TILE, w), lambda i, d: (i, 0))
    modspec = pl.BlockSpec((1, 6, D), lambda i, d: (_mod_index(i), 0, 0))
    grid_spec = pltpu.PrefetchScalarGridSpec(
        num_scalar_prefetch=1,
        grid=(NT_ALL,),
        in_specs=[row(D), pl.BlockSpec(memory_space=pl.ANY), row(LANES), row(LANES), modspec, modspec,
                  full(n1g), full(wq), full(wk), full(wv), full(wgo),
                  full(wgd), full(wgkf), full(bgkf), full(wgkb), full(bgkb)],
        out_specs=[row(D), row(GKW), row(GKW), row(GVW), row(GVW), row(GKW), row(GKW)],
        scratch_shapes=[pltpu.VMEM((2, SROWS, D), BF16), pltpu.SemaphoreType.DMA((2,))],
    )
    return pl.pallas_call(
        _odd_in_body,
        grid_spec=grid_spec,
        out_shape=[jax.ShapeDtypeStruct((T_ALL, D), F32),
                   jax.ShapeDtypeStruct((T_ALL, GKW), F32),
                   jax.ShapeDtypeStruct((T_ALL, GKW), F32),
                   jax.ShapeDtypeStruct((T_ALL, GVW), BF16),
                   jax.ShapeDtypeStruct((T_ALL, GVW), BF16),
                   jax.ShapeDtypeStruct((T_ALL, GKW), F32),
                   jax.ShapeDtypeStruct((T_ALL, GKW), F32)],
        compiler_params=_cp(("arbitrary",), VMEM_BIG),
        name="odd_in",
    )(dst, x1, yb, lrow, gate, mod0, mod1, n1g, wq, wk, wv, wgo, wgd, wgkf, bgkf, wgkb, bgkb)


def _gla_body(qf_ref, kf_ref, vf_ref, gf_ref, qb_ref, kb_ref, vb_ref, gb_ref, of_ref, ob_ref, sf_ref, sb_ref):
    @pl.when(pl.program_id(1) == 0)
    def _():
        sf_ref[...] = jnp.zeros_like(sf_ref)
        sb_ref[...] = jnp.zeros_like(sb_ref)

    r = lax.broadcasted_iota(jnp.int32, (GCHUNK, GCHUNK), 0)
    c = lax.broadcasted_iota(jnp.int32, (GCHUNK, GCHUNK), 1)
    dirs = []
    for s in range(GBS):
        at = lambda ref: ref.at[0, s]
        dirs.append((at(qf_ref), at(kf_ref), at(vf_ref), at(gf_ref), sf_ref.at[s], at(of_ref), c <= r, GCHUNK - 1))
        dirs.append((at(qb_ref), at(kb_ref), at(vb_ref), at(gb_ref), sb_ref.at[s], at(ob_ref), c >= r, 0))
    chains = []
    for q_ref, k_ref, v_ref, g_ref, s_ref, o_ref, mask, last in dirs:
        g = g_ref[...]
        g_hi = g.astype(BF16)
        g_lo = (g - g_hi.astype(F32)).astype(BF16)
        tri = mask.astype(BF16)
        cum = (jnp.dot(tri, g_hi, preferred_element_type=F32) + jnp.dot(tri, g_lo, preferred_element_type=F32))
        tot = cum[last:last + 1, :]
        q = q_ref[...]
        k = k_ref[...]
        qd = (q * jnp.exp(cum)).astype(BF16)
        kd = (k * jnp.exp(-cum)).astype(BF16)
        ke = (k * jnp.exp(tot - cum)).astype(BF16)
        dec = jnp.exp(tot)
        v = v_ref[...]
        for hd in range(GH):
            ks = slice(hd * GDK, (hd + 1) * GDK)
            vs = slice(hd * GDV, (hd + 1) * GDV)
            chains.append(dict(qd=qd[:, ks], kd=kd[:, ks], ke=ke[:, ks], dec=dec[:, ks], v=v[:, vs],
                               st=s_ref[hd], mask=mask))
    for ch in chains:
        sc = lax.dot_general(ch["qd"], ch["kd"], NT_DIMS, preferred_element_type=F32)
        ch["sc"] = jnp.where(ch["mask"], sc, 0.0).astype(BF16)
    for ch in chains:
        ch["out"] = (lax.dot_general(ch["qd"], ch["st"].astype(BF16), NT_DIMS, preferred_element_type=F32)
                     + jnp.dot(ch["sc"], ch["v"], preferred_element_type=F32))
    for ch in chains:
        ch["new"] = ch["st"] * ch["dec"] + lax.dot_general(ch["v"], ch["ke"], TN_DIMS, preferred_element_type=F32)
    for d, (_, _, _, _, s_ref, o_ref, _, _) in enumerate(dirs):
        o_ref[...] = jnp.concatenate([chains[d * GH + hd]["out"] for hd in range(GH)], axis=-1)
        for hd in range(GH):
            s_ref[hd] = chains[d * GH + hd]["new"]


def _bwd_chunk(c):
    return jnp.where(c < NCH_CTX, NCH_CTX - 1 - c, NCH_B + NCH_CTX - 1 - c)


def gla(q, k, v, gf, gb):
    fw = lambda w: pl.BlockSpec((1, GBS, GCHUNK, w), lambda b, c: (b, 0, c, 0))
    bw = lambda w: pl.BlockSpec((1, GBS, GCHUNK, w), lambda b, c: (b, 0, _bwd_chunk(c), 0))
    grouped = lambda a: a.reshape(B // GBS, GBS, ROWS_B, a.shape[-1])
    q, k, v, gf, gb = (grouped(a) for a in (q, k, v, gf, gb))
    oshape = jax.ShapeDtypeStruct((B // GBS, GBS, ROWS_B, GVW), F32)
    o_f, o_b = pl.pallas_call(
        _gla_body,
        grid=(B // GBS, NCH_B),
        in_specs=[fw(GKW), fw(GKW), fw(GVW), fw(GKW), bw(GKW), bw(GKW), bw(GVW), bw(GKW)],
        out_specs=[fw(GVW), bw(GVW)],
        out_shape=[oshape, oshape],
        scratch_shapes=[pltpu.VMEM((GBS, GH, GDV, GDK), F32), pltpu.VMEM((GBS, GH, GDV, GDK), F32)],
        compiler_params=_cp(("arbitrary", "arbitrary")),
        name="gla",
    )(q, k, v, gf, q, k, v, gb)
    return o_f.reshape(T_ALL, GVW), o_b.reshape(T_ALL, GVW)


def _out1_body(of_ref, ob_ref, go_ref, x_ref, mod_ref, gn_ref, wo_ref, n2_ref, rw_ref, rb_ref,
               x3_ref, lrow_ref, gate_ref, cnt_ref, xs_ref):
    m = mod_ref[0]
    o = of_ref[...] + ob_ref[...]
    gn = gn_ref[...]
    parts = [_rms(o[:, hd * GDV:(hd + 1) * GDV], gn) for hd in range(GH)]
    on = jnp.concatenate(parts, axis=-1) * _silu(go_ref[...].astype(F32))
    y = jnp.dot(on.astype(BF16), wo_ref[...], preferred_element_type=F32)
    x3 = x_ref[...] + m[2:3] * y
    x3_ref[...] = x3
    h2 = _modulate(x3, n2_ref[...], m[3:4], m[4:5])
    _route(h2, rw_ref, rb_ref, lrow_ref, gate_ref, cnt_ref, xs_ref)


def out1(of, ob, go, x2, mod, gng, wo, n2g, rw, rb):
    full = lambda w: pl.BlockSpec(w.shape, lambda n: (0,) * w.ndim)
    lat = lambda w: pl.BlockSpec((TILE, w), lambda n: (_lat_tile(n), 0))
    rspecs, rshapes = _route_out_specs(NT_LAT)
    return pl.pallas_call(
        _out1_body,
        grid=(NT_LAT,),
        in_specs=[lat(GVW), lat(GVW), lat(GVW), lat(D),
                  pl.BlockSpec((1, 6, D), lambda n: (n // LPB, 0, 0)),
                  full(gng), full(wo), full(n2g), full(rw), full(rb)],
        out_specs=[pl.BlockSpec((TILE, D), lambda n: (n, 0))] + rspecs,
        out_shape=[jax.ShapeDtypeStruct((T_LAT, D), F32)] + rshapes,
        compiler_params=_cp(("arbitrary",)),
        name="out1",
    )(of, ob, go, x2, mod, gng, wo, n2g, rw, rb)


def _final_body(dst_ref, x_ref, yb_hbm, lrow_ref, gate_ref, mod_ref, g_ref, o_ref, ybuf, sem):
    m = mod_ref[0]
    ys, rows = _combine(dst_ref, yb_hbm, ybuf, sem, lrow_ref, gate_ref)
    for rs, y in zip(HALVES, ys):
        o_ref[rs, :] = _rms(x_ref[rs, :] + m[5:6] * y, g_ref[...])
    rows.drain()


def final(dst, x3, yb, lrow, gate, mod, g):
    grid_spec = pltpu.PrefetchScalarGridSpec(
        num_scalar_prefetch=1,
        grid=(NT_LAT,),
        in_specs=[pl.BlockSpec((TILE, D), lambda n, d: (n, 0)),
                  pl.BlockSpec(memory_space=pl.ANY),
                  pl.BlockSpec((TILE, LANES), lambda n, d: (n, 0)),
                  pl.BlockSpec((TILE, LANES), lambda n, d: (n, 0)),
                  pl.BlockSpec((1, 6, D), lambda n, d: (n // LPB, 0, 0)),
                  pl.BlockSpec((1, D), lambda n, d: (0, 0))],
        out_specs=pl.BlockSpec((TILE, D), lambda n, d: (n, 0)),
        scratch_shapes=[pltpu.VMEM((2, SROWS, D), BF16), pltpu.SemaphoreType.DMA((2,))],
    )
    return pl.pallas_call(
        _final_body,
        grid_spec=grid_spec,
        out_shape=jax.ShapeDtypeStruct((T_LAT, D), F32),
        compiler_params=_cp(("arbitrary",)),
        name="final",
    )(dst, x3, yb, lrow, gate, mod, g)


def _rot_cols(w):
    a, b, c, d = jnp.split(w, 4, axis=-1)
    return jnp.concatenate([-b, a, -d, c], axis=-1)


def _rope_tables():
    n = np.arange(SEQ)
    row = (n // GRID_W).astype(np.float32)
    col = (n % GRID_W).astype(np.float32)
    axis_dim = ROPE // 2
    inv = (ROPE_BASE ** (-np.arange(0, axis_dim, 2, dtype=np.float32) / axis_dim)).astype(np.float32)
    ar = row[:, None] * inv
    ac = col[:, None] * inv
    ang = np.concatenate([ar, ar, ac, ac], axis=-1).astype(np.float32)
    cos = np.concatenate([np.ones((CTX, ROPE), np.float32), np.cos(ang)], axis=0)
    sin = np.concatenate([np.zeros((CTX, ROPE), np.float32), np.sin(ang)], axis=0)
    return jnp.asarray(np.tile(cos, (1, HEADS))), jnp.asarray(np.tile(sin, (1, HEADS)))


def _dft(n, scale):
    j = np.arange(n, dtype=np.int64)
    ang = 2.0 * np.pi * ((j[:, None] * j[None, :]) % n).astype(np.float64) / n
    return (np.cos(ang) * scale), (np.sin(ang) * scale)


def _dft_tables():
    cn, sn = _dft(SEQ, SEQ ** -0.5)
    c2, s2 = _dft(CTX, CTX ** -0.5)
    cg, sg = _dft(FGD, FGD ** -0.5)
    eye = np.eye(FG)
    to = lambda a: jnp.asarray(a.astype(np.float32)).astype(BF16)
    return to(cn), to(sn), to(c2), to(s2), to(np.kron(eye, cg)), to(np.kron(eye, sg))


def _router_pad(rw, rb):
    rwp = jnp.zeros((LANES, D), F32).at[:NE].set(rw.T)
    rbp = jnp.full((LANES, 1), -1e30, F32).at[:NE, 0].set(rb)
    return rwp, rbp


def kernel(x, c, ctx, c_ctx, final_norm_g, l0_mod_w, l0_mod_b, l0_norm1_g, l0_w_in, l0_q_norm_g, l0_w_q_up, l0_kv_norm_g, l0_w_kv_up, l0_w_out, l0_norm2_g, l0_router_w, l0_router_b, l0_w_gu, l0_b_gu, l0_w_down, l0_b_down, l1_mod_w, l1_mod_b, l1_norm1_g, l1_w_in, l1_w_gk_fwd, l1_b_gk_fwd, l1_w_gk_bwd, l1_b_gk_bwd, l1_gnorm_g, l1_w_out, l1_norm2_g, l1_router_w, l1_router_b, l1_w_gu, l1_b_gu, l1_w_down, l1_b_down):
    row = lambda g: g.reshape(1, -1)
    c2 = ctx.reshape(B * CTX, D)
    x2 = x.reshape(T_LAT, D)
    cc = jnp.zeros((16, D), F32).at[:B].set(c).at[B].set(c_ctx)
    mod0 = adaln(cc, l0_mod_w, l0_mod_b)
    mod1 = adaln(cc, l1_mod_w, l1_mod_b)

    w_in = l0_w_in
    wqa = w_in[:, :Q_LORA].astype(BF16)
    wkva = w_in[:, Q_LORA:Q_LORA + KV_LORA].astype(BF16)
    wkr = w_in[:, Q_LORA + KV_LORA:Q_LORA + KV_LORA + ROPE]
    wkr2 = jnp.concatenate([wkr, _rot_cols(wkr)], axis=-1).astype(BF16)
    wf = w_in[:, Q_LORA + KV_LORA + ROPE:].astype(BF16)
    wq3 = l0_w_q_up.reshape(Q_LORA, HEADS, QK)
    wq_nope = wq3[:, :, :NOPE].reshape(Q_LORA, HEADS * NOPE)
    wq_rope = wq3[:, :, NOPE:]
    wqup = jnp.concatenate([wq_nope, wq_rope.reshape(Q_LORA, HEADS * ROPE),
                            _rot_cols(wq_rope).reshape(Q_LORA, HEADS * ROPE)], axis=-1).astype(BF16)
    wkv3 = l0_w_kv_up.reshape(KV_LORA, HEADS, NOPE + VDIM)
    wkvup = jnp.concatenate([wkv3[:, :, :NOPE].reshape(KV_LORA, HEADS * NOPE),
                             wkv3[:, :, NOPE:].reshape(KV_LORA, HEADS * VDIM)], axis=-1).astype(BF16)
    cos6, sin6 = _rope_tables()
    q, k, v, uf = even_in(c2, x2, mod0, row(l0_norm1_g), wqa, wkva, wkr2, wf, row(l0_q_norm_g), wqup,
                          row(l0_kv_norm_g), wkvup, cos6, sin6)
    att = attention(q, k, v).reshape(T_ALL, HEADS * VDIM)
    fmix = fourier(uf.reshape(B, ROWS_B, FW), *_dft_tables()).reshape(T_ALL, FW)
    rw0, rb0 = _router_pad(l0_router_w, l0_router_b)
    x1, lrow0, gate0, cnt0, xs0 = out0(c2, x2, att, fmix, mod0, l0_w_out[:HEADS * VDIM].astype(BF16),
                                       l0_w_out[HEADS * VDIM:].astype(BF16), row(l0_norm2_g), rw0, rb0)
    src0, be0, bv0, dst0, nb0 = _chunk_lists(cnt0, NT_ALL)
    yb0 = expert_ffn(xs0, src0, be0, bv0, nb0, l0_w_gu, l0_b_gu, l0_w_down, l0_b_down)

    w1 = l1_w_in
    o = 0
    wq1 = w1[:, o:o + GKW].astype(BF16); o += GKW
    wk1 = w1[:, o:o + GKW].astype(BF16); o += GKW
    wv1 = w1[:, o:o + GVW].astype(BF16); o += GVW
    wgo = w1[:, o:o + GVW].astype(BF16); o += GVW
    wgd = jnp.zeros((D, LANES), F32).at[:, :2 * GRANK].set(w1[:, o:]).astype(BF16)
    wgkf = jnp.zeros((LANES, GKW), F32).at[:GRANK].set(l1_w_gk_fwd)
    wgkb = jnp.zeros((LANES, GKW), F32).at[GRANK:2 * GRANK].set(l1_w_gk_bwd)
    x2, q1, k1, v1, go, gf, gb = odd_in(dst0, x1, yb0, lrow0, gate0, mod0, mod1, row(l1_norm1_g), wq1, wk1, wv1,
                                        wgo, wgd, wgkf, row(l1_b_gk_fwd), wgkb, row(l1_b_gk_bwd))
    o_f, o_b = gla(q1, k1, v1, gf, gb)
    rw1, rb1 = _router_pad(l1_router_w, l1_router_b)
    x3, lrow1, gate1, cnt1, xs1 = out1(o_f, o_b, go, x2, mod1, row(l1_gnorm_g), l1_w_out.astype(BF16),
                                       row(l1_norm2_g), rw1, rb1)
    src1, be1, bv1, dst1, nb1 = _chunk_lists(cnt1, NT_LAT)
    yb1 = expert_ffn(xs1, src1, be1, bv1, nb1, l1_w_gu, l1_b_gu, l1_w_down, l1_b_down)
    out = final(dst1, x3, yb1, lrow1, gate1, mod1, row(final_norm_g))
    return out.reshape(B, SEQ, D)
```

```python
import jax
import jax.numpy as jnp
import numpy as np
from jax import lax
from jax.experimental import pallas as pl
from jax.experimental.pallas import tpu as pltpu

D = 1024
B = 8
SEQ = 4096
CTX = 256
GRID_W = 64
EPS = 1e-6
HEADS = 6
Q_LORA = 384
KV_LORA = 256
NOPE = 128
ROPE = 64
VDIM = 128
QK = NOPE + ROPE
ROPE_BASE = 10000.0
FG = 4
FGD = 64
FW = FG * FGD
GH = 4
GDK = 128
GDV = 256
GRANK = 16
GNORM = 16.0
GCHUNK = 64
GKW = GH * GDK
GVW = GH * GDV
NE = 32
TOPK = 4
DFF = 1024
ALPHA = 1.702
LIMIT = 7.0

LANES = 128
TILE = 256
HALF = TILE // 2
HALVES = (slice(0, HALF), slice(HALF, TILE))
ROWS_B = CTX + SEQ
TPB = ROWS_B // TILE
T_ALL = B * ROWS_B
NT_ALL = T_ALL // TILE
T_LAT = B * SEQ
NT_LAT = T_LAT // TILE
LPB = SEQ // TILE
HPS = 3
CH = 16
MAXCH = (TILE * TOPK + NE * (CH - 1)) // CH
NCHT = -(-MAXCH * CH // LANES) * LANES // CH
SROWS = NCHT * CH
SGRP = 512
FBLK = 512
CPB = FBLK // CH
GBS = 8
NCH_B = ROWS_B // GCHUNK
NCH_CTX = CTX // GCHUNK
V7X_VMEM_BYTES = 64 * 1024 * 1024
VMEM_BIG = V7X_VMEM_BYTES * 7 // 8
VMEM_MID = V7X_VMEM_BYTES * 5 // 8

F32 = jnp.float32
BF16 = jnp.bfloat16
HI = lax.Precision.HIGHEST
NT_DIMS = (((1,), (1,)), ((), ()))
TN_DIMS = (((0,), (0,)), ((), ()))


def _cp(sem, vmem=VMEM_MID):
    return pltpu.CompilerParams(dimension_semantics=sem, vmem_limit_bytes=vmem)


def _rms(x, g):
    return x * lax.rsqrt(jnp.mean(x * x, axis=-1, keepdims=True) + EPS) * g


def _modulate(x, g, sh, sc):
    return _rms(x, g) * (1.0 + sc) + sh


def _silu(x):
    return x * jax.nn.sigmoid(x)


def _mod_index(i):
    return jnp.where(i % TPB == 0, B, i // TPB)


def _lat_tile(n):
    return n + n // LPB + 1


def _adaln_body(c_ref, w_ref, b_ref, o_ref):
    s = _silu(c_ref[...])
    o_ref[...] = jnp.dot(s, w_ref[...], precision=HI, preferred_element_type=F32) + b_ref[...]


def adaln(cc, mod_w, mod_b):
    tn = 1536
    out = pl.pallas_call(
        _adaln_body,
        grid=(6 * D // tn,),
        in_specs=[pl.BlockSpec((16, D), lambda j: (0, 0)),
                  pl.BlockSpec((D, tn), lambda j: (0, j)),
                  pl.BlockSpec((1, tn), lambda j: (0, j))],
        out_specs=pl.BlockSpec((16, tn), lambda j: (0, j)),
        out_shape=jax.ShapeDtypeStruct((16, 6 * D), F32),
        compiler_params=_cp(("arbitrary",)),
        name="adaln",
    )(cc, mod_w, mod_b.reshape(1, 6 * D))
    return out.reshape(16, 6, D)


def _stream_tile(c_ref, x_ref):
    return jnp.where(pl.program_id(0) % TPB == 0, c_ref[...], x_ref[...])


def _stream_specs():
    return [pl.BlockSpec((TILE, D), lambda i: (i // TPB, 0)),
            pl.BlockSpec((TILE, D), lambda i: ((i // TPB) * LPB + jnp.maximum(i % TPB - 1, 0), 0))]


def _even_in_body(c_ref, x_ref, mod_ref, n1_ref, wqa_ref, wkva_ref, wkr_ref, wf_ref, qg_ref, wqup_ref,
                  kvg_ref, wkvup_ref, cos_ref, sin_ref, q_ref, k_ref, v_ref, uf_ref):
    m = mod_ref[0]
    h = _modulate(_stream_tile(c_ref, x_ref), n1_ref[...], m[0:1], m[1:2]).astype(BF16)
    cq = jnp.dot(h, wqa_ref[...], preferred_element_type=F32)
    ckv = jnp.dot(h, wkva_ref[...], preferred_element_type=F32)
    kr2 = jnp.dot(h, wkr_ref[...], preferred_element_type=F32)
    uf_ref[...] = jnp.dot(h, wf_ref[...], preferred_element_type=F32).astype(BF16)
    cos6 = cos_ref[...]
    sin6 = sin_ref[...]
    scale = QK ** -0.5
    qall = jnp.dot(_rms(cq, qg_ref[...]).astype(BF16), wqup_ref[...], preferred_element_type=F32)
    qn = qall[:, :HEADS * NOPE] * scale
    qr = (qall[:, HEADS * NOPE:HEADS * QK] * cos6 + qall[:, HEADS * QK:] * sin6) * scale
    kvall = jnp.dot(_rms(ckv, kvg_ref[...]).astype(BF16), wkvup_ref[...], preferred_element_type=F32)
    kr = (kr2[:, :ROPE] * cos6[:, :ROPE] + kr2[:, ROPE:] * sin6[:, :ROPE]).astype(BF16)
    ones_col = (lax.broadcasted_iota(jnp.int32, (TILE, VDIM), 1) == 0).astype(BF16)
    for hd in range(HEADS):
        q_ref[0, hd, :, 0:NOPE] = qn[:, hd * NOPE:(hd + 1) * NOPE].astype(BF16)
        q_ref[0, hd, :, NOPE:QK] = qr[:, hd * ROPE:(hd + 1) * ROPE].astype(BF16)
        k_ref[0, hd, :, 0:NOPE] = kvall[:, hd * NOPE:(hd + 1) * NOPE].astype(BF16)
        k_ref[0, hd, :, NOPE:QK] = kr
        v_ref[0, hd, :, 0:VDIM] = kvall[:, HEADS * NOPE + hd * VDIM:HEADS * NOPE + (hd + 1) * VDIM].astype(BF16)
        v_ref[0, hd, :, VDIM:2 * VDIM] = ones_col


def even_in(c2, x2, mod, n1g, wqa, wkva, wkr2, wf, qg, wqup, kvg, wkvup, cos6, sin6):
    full = lambda a: pl.BlockSpec(a.shape, lambda i: (0,) * a.ndim)
    hs = lambda w: pl.BlockSpec((1, HEADS, TILE, w), lambda i: (i // TPB, 0, i % TPB, 0))
    return pl.pallas_call(
        _even_in_body,
        grid=(NT_ALL,),
        in_specs=_stream_specs() + [
                  pl.BlockSpec((1, 6, D), lambda i: (_mod_index(i), 0, 0)),
                  full(n1g), full(wqa), full(wkva), full(wkr2), full(wf), full(qg), full(wqup),
                  full(kvg), full(wkvup),
                  pl.BlockSpec((TILE, HEADS * ROPE), lambda i: (i % TPB, 0)),
                  pl.BlockSpec((TILE, HEADS * ROPE), lambda i: (i % TPB, 0))],
        out_specs=[hs(QK), hs(QK), hs(2 * VDIM), pl.BlockSpec((TILE, FW), lambda i: (i, 0))],
        out_shape=[jax.ShapeDtypeStruct((B, HEADS, ROWS_B, QK), BF16),
                   jax.ShapeDtypeStruct((B, HEADS, ROWS_B, QK), BF16),
                   jax.ShapeDtypeStruct((B, HEADS, ROWS_B, 2 * VDIM), BF16),
                   jax.ShapeDtypeStruct((T_ALL, FW), BF16)],
        compiler_params=_cp(("arbitrary",)),
        name="even_in",
    )(c2, x2, mod, n1g, wqa, wkva, wkr2, wf, qg, wqup, kvg, wkvup, cos6, sin6)


def _attend(q_ref, k_ref, v_ref, o_ref, rows):
    ks = [k_ref[0, j, 0:rows, :] for j in range(HPS)]
    ss = [lax.dot_general(q_ref[0, j], ks[j], NT_DIMS, preferred_element_type=F32) for j in range(HPS)]
    ps = [jnp.exp(s - jnp.max(s, axis=-1, keepdims=True)).astype(BF16) for s in ss]
    ov = [jnp.dot(ps[j], v_ref[0, j, 0:rows, :], preferred_element_type=F32) for j in range(HPS)]
    o_ref[0] = jnp.concatenate([o[:, :VDIM] / o[:, VDIM:VDIM + 1] for o in ov], axis=-1).astype(BF16)


def _attn_body(q_ref, k_ref, v_ref, o_ref):
    qt = pl.program_id(2)

    @pl.when(qt == 0)
    def _():
        _attend(q_ref, k_ref, v_ref, o_ref, CTX)

    @pl.when(qt > 0)
    def _():
        _attend(q_ref, k_ref, v_ref, o_ref, ROWS_B)


def attention(q, k, v):
    return pl.pallas_call(
        _attn_body,
        grid=(B, HEADS // HPS, TPB),
        in_specs=[pl.BlockSpec((1, HPS, TILE, QK), lambda b, h, t: (b, h, t, 0)),
                  pl.BlockSpec((1, HPS, ROWS_B, QK), lambda b, h, t: (b, h, 0, 0)),
                  pl.BlockSpec((1, HPS, ROWS_B, 2 * VDIM), lambda b, h, t: (b, h, 0, 0))],
        out_specs=pl.BlockSpec((1, TILE, HPS * VDIM), lambda b, h, t: (b, t, h)),
        out_shape=jax.ShapeDtypeStruct((B, ROWS_B, HEADS * VDIM), BF16),
        compiler_params=_cp(("arbitrary", "arbitrary", "arbitrary"), VMEM_BIG),
        name="attention",
    )(q, k, v)


def _fourier_body(u_ref, cn_ref, sn_ref, c2_ref, s2_ref, cc_ref, sc_ref, o_ref):
    i = pl.program_id(0)

    def finish(a, b):
        o_ref[0] = (jnp.dot(a.astype(BF16), cc_ref[...], preferred_element_type=F32)
                    - jnp.dot(b.astype(BF16), sc_ref[...], preferred_element_type=F32)).astype(BF16)

    @pl.when(i == 0)
    def _():
        u = u_ref[0, 0:CTX, :]
        finish(jnp.dot(c2_ref[...], u, preferred_element_type=F32),
               jnp.dot(s2_ref[...], u, preferred_element_type=F32))

    @pl.when(i > 0)
    def _():
        u = u_ref[0, CTX:ROWS_B, :]
        finish(jnp.dot(cn_ref[...], u, preferred_element_type=F32),
               jnp.dot(sn_ref[...], u, preferred_element_type=F32))


def fourier(uf, cn, sn, c2, s2, ccb, scb):
    full = lambda a: pl.BlockSpec(a.shape, lambda i, b: (0,) * a.ndim)
    return pl.pallas_call(
        _fourier_body,
        grid=(TPB, B),
        in_specs=[pl.BlockSpec((1, ROWS_B, FW), lambda i, b: (b, 0, 0)),
                  pl.BlockSpec((TILE, SEQ), lambda i, b: (jnp.maximum(i - 1, 0), 0)),
                  pl.BlockSpec((TILE, SEQ), lambda i, b: (jnp.maximum(i - 1, 0), 0)),
                  full(c2), full(s2), full(ccb), full(scb)],
        out_specs=pl.BlockSpec((1, TILE, FW), lambda i, b: (b, i, 0)),
        out_shape=jax.ShapeDtypeStruct((B, ROWS_B, FW), BF16),
        compiler_params=_cp(("arbitrary", "arbitrary")),
        name="fourier",
    )(uf, cn, sn, c2, s2, ccb, scb)


def _route_tiles(h2s, rw_ref, rb_ref, outs):
    rw = rw_ref[...]
    w_hi = rw.astype(BF16)
    w_lo = (rw - w_hi.astype(F32)).astype(BF16)
    w_both = jnp.concatenate([w_hi, w_lo], axis=0)
    h_his = [h2.astype(BF16) for h2 in h2s]
    h_los = [(h2 - h_hi.astype(F32)).astype(BF16) for h2, h_hi in zip(h2s, h_his)]
    boths = [lax.dot_general(w_both, h_hi, NT_DIMS, preferred_element_type=F32) for h_hi in h_his]
    ls = [both[:LANES] + both[LANES:] + lax.dot_general(w_hi, h_lo, NT_DIMS, preferred_element_type=F32)
          + rb_ref[...] for both, h_lo in zip(boths, h_los)]
    sub = lax.broadcasted_iota(jnp.int32, (LANES, TILE), 0)
    picks = []
    for l in ls:
        vals, idxs = [], []
        for _ in range(TOPK):
            mx = jnp.max(l, axis=0, keepdims=True)
            am = jnp.min(jnp.where(l == mx, sub, LANES), axis=0, keepdims=True)
            vals.append(mx)
            idxs.append(am)
            l = jnp.where(sub == am, -jnp.inf, l)
        ex = [jnp.exp(vv - vals[0]) for vv in vals]
        den = ex[0] + ex[1] + ex[2] + ex[3]
        onehot = jnp.zeros((LANES, TILE), F32)
        for am in idxs:
            onehot = onehot + (sub == am).astype(F32)
        picks.append((idxs, ex, den, onehot))
    r = lax.broadcasted_iota(jnp.int32, (TILE, TILE), 0)
    c = lax.broadcasted_iota(jnp.int32, (TILE, TILE), 1)
    earlier = (r < c).astype(BF16)
    ranks = [jnp.dot(onehot.astype(BF16), earlier, preferred_element_type=F32)
             for _, _, _, onehot in picks]
    er = lax.broadcasted_iota(jnp.int32, (LANES, LANES), 0)
    ec = lax.broadcasted_iota(jnp.int32, (LANES, LANES), 1)
    below = (ec < er).astype(BF16)
    cnts = [jnp.sum(onehot, axis=1, keepdims=True) for _, _, _, onehot in picks]
    segs = [jnp.dot(below, jnp.broadcast_to(jnp.ceil(cnt * (1.0 / CH)), (LANES, LANES)).astype(BF16),
                    preferred_element_type=F32)[:, 0:1] * CH
            for cnt in cnts]
    all_rows = []
    for (idxs, ex, den, _), rank, seg, cnt, (lrow_ref, gate_ref, cnt_ref, _) in zip(picks, ranks, segs, cnts, outs):
        dest = rank + seg
        lrow_t = jnp.zeros((LANES, TILE), F32)
        gate_t = jnp.zeros((LANES, TILE), F32)
        rows = []
        for kk in range(TOPK):
            pos = jnp.sum(jnp.where(sub == idxs[kk], dest, 0.0), axis=0, keepdims=True)
            rows.append(pos.astype(jnp.int32))
            lrow_t = jnp.where(sub == kk, pos, lrow_t)
            gate_t = jnp.where(sub == kk, ex[kk] / den, gate_t)
        lrow_ref[...] = lrow_t.T.astype(jnp.int32)
        gate_ref[...] = gate_t.T
        cnt_ref[...] = jnp.broadcast_to(cnt, (LANES, LANES)).T[0:8, :]
        all_rows.append(rows)
    for g in range(SROWS // SGRP):
        srow = g * SGRP + lax.broadcasted_iota(jnp.int32, (SGRP, TILE), 0)
        for rows, h_hi, (_, _, _, xs_ref) in zip(all_rows, h_his, outs):
            hit = rows[0] == srow
            for kk in range(1, TOPK):
                hit = jnp.logical_or(hit, rows[kk] == srow)
            xs_ref[g * SGRP:(g + 1) * SGRP, :] = jnp.dot(hit.astype(BF16), h_hi,
                                                        preferred_element_type=F32).astype(BF16)


def _route_out_shapes(n_tiles):
    return [jax.ShapeDtypeStruct((n_tiles * TILE, LANES), jnp.int32),
            jax.ShapeDtypeStruct((n_tiles * TILE, LANES), F32),
            jax.ShapeDtypeStruct((n_tiles, 8, LANES), F32),
            jax.ShapeDtypeStruct((n_tiles * SROWS, D), BF16)]


def _pair_spec(shp):
    return pl.BlockSpec((2 * shp[0],) + shp[1:], lambda n: (n,) + (0,) * (len(shp) - 1))


ROUTE_PAIR_SPECS = [_pair_spec((TILE, LANES)), _pair_spec((TILE, LANES)), _pair_spec((1, 8, LANES)),
                    _pair_spec((SROWS, D))]


def _out0_body(c0_ref, x0_ref, mod0_ref, c1_ref, x1_ref, mod1_ref, a_ref, f_ref, woa_ref, wof_ref, n2_ref,
               rw_ref, rb_ref, xo_ref, lrow_ref, gate_ref, cnt_ref, xs_ref):
    tiles = ((c0_ref, x0_ref, mod0_ref), (c1_ref, x1_ref, mod1_ref))
    ys = []
    for k in range(2):
        rs = pl.ds(k * TILE, TILE)
        ys.append(jnp.dot(a_ref[rs, :], woa_ref[...], preferred_element_type=F32)
                  + jnp.dot(f_ref[rs, :], wof_ref[...], preferred_element_type=F32))
    h2s, outs = [], []
    for k, ((c_ref, x_ref, mod_ref), y) in enumerate(zip(tiles, ys)):
        rs = pl.ds(k * TILE, TILE)
        m = mod_ref[0]
        is_ctx = (2 * pl.program_id(0) + k) % TPB == 0
        x1 = jnp.where(is_ctx, c_ref[...], x_ref[...]) + m[2:3] * y
        xo_ref[rs, :] = x1
        h2s.append(_modulate(x1, n2_ref[...], m[3:4], m[4:5]))
        outs.append((lrow_ref.at[rs, :], gate_ref.at[rs, :], cnt_ref.at[k], xs_ref.at[pl.ds(k * SROWS, SROWS), :]))
    _route_tiles(h2s, rw_ref, rb_ref, outs)


def out0(c2, x2, a, f, mod, woa, wof, n2g, rw, rb):
    full = lambda w: pl.BlockSpec(w.shape, lambda n: (0,) * w.ndim)

    def stream(k):
        tile = lambda n: 2 * n + k
        return [pl.BlockSpec((TILE, D), lambda n: (tile(n) // TPB, 0)),
                pl.BlockSpec((TILE, D), lambda n: ((tile(n) // TPB) * LPB + jnp.maximum(tile(n) % TPB - 1, 0), 0)),
                pl.BlockSpec((1, 6, D), lambda n: (_mod_index(tile(n)), 0, 0))]

    return pl.pallas_call(
        _out0_body,
        grid=(NT_ALL // 2,),
        in_specs=stream(0) + stream(1) + [_pair_spec((TILE, HEADS * VDIM)), _pair_spec((TILE, FW)),
                                          full(woa), full(wof), full(n2g), full(rw), full(rb)],
        out_specs=[_pair_spec((TILE, D))] + ROUTE_PAIR_SPECS,
        out_shape=[jax.ShapeDtypeStruct((T_ALL, D), F32)] + _route_out_shapes(NT_ALL),
        compiler_params=_cp(("arbitrary",), VMEM_BIG),
        name="out0",
    )(c2, x2, mod, c2, x2, mod, a, f, woa, wof, n2g, rw, rb)


def _chunk_lists(cnt, n_tiles):
    n_blk = (n_tiles * MAXCH + CPB - 1) // CPB + NE
    cnt = cnt[:, 0, :NE].astype(jnp.int32)
    nch = (cnt + CH - 1) // CH
    lend = jnp.cumsum(nch, axis=1)
    loff = lend - nch
    tend = jnp.cumsum(nch, axis=0)
    toff = tend - nch
    tot = tend[-1]
    blk = (tot + CPB - 1) // CPB
    bend = jnp.cumsum(blk)
    estart = (bend - blk) * CPB
    nvalid = bend[-1]
    bi = jnp.arange(n_blk, dtype=jnp.int32)
    bvalid = bi < nvalid
    count_le = lambda ends, v: jnp.sum((ends[None, :] <= v[:, None]).astype(jnp.int32), axis=1)
    be = jnp.where(bvalid, jnp.minimum(count_le(bend, bi), NE - 1), jnp.sum((bend < nvalid).astype(jnp.int32)))
    g = jnp.arange(n_blk * CPB, dtype=jnp.int32)
    sel = (jnp.repeat(be, CPB)[:, None] == jnp.arange(NE, dtype=jnp.int32)[None, :]).astype(F32)
    pick = lambda table: jnp.dot(sel, table.astype(F32), precision=HI).astype(jnp.int32)
    o = g - pick(estart[:, None])[:, 0]
    gvalid = jnp.logical_and(jnp.repeat(bvalid, CPB), o < pick(tot[:, None])[:, 0])
    tend_g = pick(tend.T)
    tile = jnp.minimum(jnp.sum((tend_g <= o[:, None]).astype(jnp.int32), axis=1), n_tiles - 1)
    tsel = tile[:, None] == jnp.arange(n_tiles, dtype=jnp.int32)[None, :]
    shift = jnp.sum(jnp.where(tsel, pick((loff - toff).T), 0), axis=1)
    src = jnp.where(gvalid, tile * NCHT + o + shift, 0).astype(jnp.int32)
    cslot = jnp.arange(NCHT, dtype=jnp.int32)
    inside = jnp.logical_and(loff[:, None, :] <= cslot[None, :, None], cslot[None, :, None] < lend[:, None, :])
    base = estart[None, :] + toff - loff
    dst = cslot[None, :] + jnp.sum(jnp.where(inside, base[:, None, :], 0), axis=2)
    dst = jnp.where(cslot[None, :] < lend[:, -1:], dst, 0).astype(jnp.int32)
    src = jnp.concatenate([src, jnp.zeros((CPB,), jnp.int32)])
    dst = jnp.concatenate([dst.reshape(-1), jnp.zeros((NCHT,), jnp.int32)])
    return src, be.astype(jnp.int32), bvalid.astype(jnp.int32), dst, n_blk


def _chunk_copy(src_hbm, chunk, buf, slot, j, sem):
    start = chunk * CH if isinstance(chunk, int) else pl.multiple_of(chunk * CH, CH)
    return pltpu.make_async_copy(src_hbm.at[pl.ds(start, CH), :],
                                 buf.at[slot, pl.ds(j * CH, CH), :], sem.at[slot])


class _ChunkGather:
    def __init__(self, list_ref, per_step, src_hbm, buf, sem):
        self.list_ref, self.per_step, self.src_hbm, self.buf, self.sem = list_ref, per_step, src_hbm, buf, sem
        self.step = pl.program_id(0)
        self.n_steps = pl.num_programs(0)

    def _issue(self, st, slot):
        for j in range(self.per_step):
            _chunk_copy(self.src_hbm, self.list_ref[st * self.per_step + j], self.buf, slot, j, self.sem).start()

    def _wait(self, slot):
        for j in range(self.per_step):
            _chunk_copy(self.src_hbm, 0, self.buf, slot, j, self.sem).wait()

    def arrive(self):
        @pl.when(self.step == 0)
        def _():
            self._issue(0, 0)

        slot = self.step % 2
        self._wait(slot)
        return slot

    def fetch_next(self):
        self._issue(self.step + 1, (self.step + 1) % 2)

    def drain(self):
        @pl.when(self.step == self.n_steps - 1)
        def _():
            self._wait((self.step + 1) % 2)


def _ffn_body(src_ref, be_ref, bv_ref, bf_ref, ws_ref, ne_ref, xs_hbm, wgu_hbm, bg_ref, bl_ref, wdn_hbm, bd_ref,
              yb_ref, xbuf, sem, wg_s, wl_s, wd_s, wgu_buf, wdn_buf, wsem):
    i = pl.program_id(0)
    rows = _ChunkGather(src_ref, CPB, xs_hbm, xbuf, sem)
    slot = rows.arrive()

    def weight_copies(e, s):
        return (pltpu.make_async_copy(wgu_hbm.at[e], wgu_buf.at[s], wsem.at[0, s]),
                pltpu.make_async_copy(wdn_hbm.at[e], wdn_buf.at[s], wsem.at[1, s]))

    @pl.when(bf_ref[i] == 1)
    def _():
        s = ws_ref[i]

        @pl.when(i == 0)
        def _():
            for cp in weight_copies(be_ref[0], 0):
                cp.start()

        for cp in weight_copies(be_ref[i], s):
            cp.wait()

        @pl.when(ne_ref[i] >= 0)
        def _():
            for cp in weight_copies(ne_ref[i], 1 - s):
                cp.start()

        r = lax.broadcasted_iota(jnp.int32, (2 * LANES, 2 * LANES), 0)
        c = lax.broadcasted_iota(jnp.int32, (2 * LANES, 2 * LANES), 1)
        split = (r == 2 * (c % LANES) + c // LANES).astype(BF16)
        for g in range(DFF // LANES):
            pair = wgu_buf[s, :, pl.ds(g * 2 * LANES, 2 * LANES)].astype(BF16)
            sep = jnp.dot(pair, split, preferred_element_type=F32)
            wg_s[:, g * LANES:(g + 1) * LANES] = sep[:, :LANES].astype(BF16)
            wl_s[:, g * LANES:(g + 1) * LANES] = sep[:, LANES:].astype(BF16)
        wd_s[...] = wdn_buf[s].astype(BF16)

    @pl.when(bv_ref[i] == 1)
    def _():
        rows.fetch_next()
        xb = xbuf[slot]
        glu = jnp.dot(xb, wg_s[...], preferred_element_type=F32) + bg_ref[0]
        lin = jnp.dot(xb, wl_s[...], preferred_element_type=F32) + bl_ref[0]
        glu = jnp.minimum(glu, LIMIT)
        lin = jnp.clip(lin, -LIMIT, LIMIT)
        act = glu * jax.nn.sigmoid(ALPHA * glu) * (lin + 1.0)
        out = jnp.dot(act.astype(BF16), wd_s[...], preferred_element_type=F32) + bd_ref[0]
        yb_ref[...] = out.astype(BF16)

    @pl.when(bv_ref[i] == 0)
    def _():
        rows.fetch_next()
        yb_ref[...] = jnp.zeros_like(yb_ref)

    rows.drain()


def expert_ffn(xs, src, be, bv, n_blk, w_gu, b_gu, w_down, b_down):
    bi = jnp.arange(n_blk, dtype=jnp.int32)
    bf = jnp.concatenate([jnp.ones((1,), jnp.int32), (be[1:] != be[:-1]).astype(jnp.int32)])
    ws = (jnp.cumsum(bf) - 1) % 2
    later_first = jnp.logical_and(bf[None, :] == 1, bi[None, :] > bi[:, None])
    nxt = jnp.min(jnp.where(later_first, bi[None, :], n_blk), axis=1)
    ne = jnp.sum(jnp.where(bi[None, :] == nxt[:, None], be[None, :], 0), axis=1)
    ne = jnp.where(nxt < n_blk, ne, -1)
    bgu = b_gu.reshape(NE, 1, DFF, 2)
    wspec = lambda shp: pl.BlockSpec((1,) + shp, lambda i, src, be, *_: (be[i], 0, 0))
    grid_spec = pltpu.PrefetchScalarGridSpec(
        num_scalar_prefetch=6,
        grid=(n_blk,),
        in_specs=[pl.BlockSpec(memory_space=pl.ANY),
                  pl.BlockSpec(memory_space=pl.ANY), wspec((1, DFF)), wspec((1, DFF)),
                  pl.BlockSpec(memory_space=pl.ANY), wspec((1, D))],
        out_specs=pl.BlockSpec((FBLK, D), lambda i, *_: (i, 0)),
        scratch_shapes=[pltpu.VMEM((2, FBLK, D), BF16), pltpu.SemaphoreType.DMA((2,)),
                        pltpu.VMEM((D, DFF), BF16), pltpu.VMEM((D, DFF), BF16), pltpu.VMEM((DFF, D), BF16),
                        pltpu.VMEM((2, D, 2 * DFF), F32), pltpu.VMEM((2, DFF, D), F32),
                        pltpu.SemaphoreType.DMA((2, 2))],
    )
    return pl.pallas_call(
        _ffn_body,
        grid_spec=grid_spec,
        out_shape=jax.ShapeDtypeStruct((n_blk * FBLK, D), BF16),
        compiler_params=_cp(("arbitrary",), VMEM_BIG),
        name="expert_ffn",
    )(src, be, bv, bf, ws.astype(jnp.int32), ne.astype(jnp.int32),
      xs, w_gu, bgu[..., 0], bgu[..., 1], w_down, b_down.reshape(NE, 1, D))


def _combine(dst_ref, yb_hbm, ybuf, sem, lrow_ref, gate_ref):
    rows = _ChunkGather(dst_ref, NCHT, yb_hbm, ybuf, sem)
    slot = rows.arrive()
    rows.fetch_next()
    scol = lax.broadcasted_iota(jnp.int32, (HALF, SROWS), 1)
    ws = []
    for rs in HALVES:
        lrow = lrow_ref[rs, :]
        gate = gate_ref[rs, :]
        w = jnp.zeros((HALF, SROWS), F32)
        for kk in range(TOPK):
            w = w + jnp.where(lrow[:, kk:kk + 1] == scol, gate[:, kk:kk + 1], 0.0)
        ws.append(w.astype(BF16))
    return [jnp.dot(w, ybuf[slot], preferred_element_type=F32) for w in ws], rows


def _odd_in_body(dst_ref, x_ref, yb_hbm, lrow_ref, gate_ref, mod0_ref, mod1_ref, n1_ref, wq_ref, wk_ref, wv_ref,
                 wgo_ref, wgd_ref, wgkf_ref, bgkf_ref, wgkb_ref, bgkb_ref,
                 x2_ref, q_ref, k_ref, v_ref, go_ref, gf_ref, gb_ref, ybuf, sem):
    m0 = mod0_ref[0]
    m1 = mod1_ref[0]
    ys, rows = _combine(dst_ref, yb_hbm, ybuf, sem, lrow_ref, gate_ref)
    hs = []
    for rs, y in zip(HALVES, ys):
        x2 = x_ref[rs, :] + m0[5:6] * y
        x2_ref[rs, :] = x2
        hs.append(_modulate(x2, n1_ref[...], m1[0:1], m1[1:2]).astype(BF16))
    for rs, h in zip(HALVES, hs):
        q_ref[rs, :] = jnp.dot(h, wq_ref[...], preferred_element_type=F32) * (GDK ** -0.5)
    for rs, h in zip(HALVES, hs):
        k_ref[rs, :] = jnp.dot(h, wk_ref[...], preferred_element_type=F32)
    for rs, h in zip(HALVES, hs):
        v_ref[rs, :] = jnp.dot(h, wv_ref[...], preferred_element_type=F32).astype(BF16)
    for rs, h in zip(HALVES, hs):
        go_ref[rs, :] = jnp.dot(h, wgo_ref[...], preferred_element_type=F32).astype(BF16)
    gds = [jnp.dot(h, wgd_ref[...], preferred_element_type=F32) for h in hs]

    def pieces(a):
        hi = a.astype(BF16)
        return hi, (a - hi.astype(F32)).astype(BF16)

    for w_ref, b_ref, g_ref in ((wgkf_ref, bgkf_ref, gf_ref), (wgkb_ref, bgkb_ref, gb_ref)):
        w_hi, w_lo = pieces(w_ref[...])
        for rs, gd in zip(HALVES, gds):
            gd_hi, gd_lo = pieces(gd)
            z = (jnp.dot(gd_hi, w_hi, preferred_element_type=F32) + jnp.dot(gd_hi, w_lo, preferred_element_type=F32)
                 + jnp.dot(gd_lo, w_hi, preferred_element_type=F32) + b_ref[...])
            g_ref[rs, :] = (jnp.minimum(z, 0.0) - jnp.log(1.0 + jnp.exp(-jnp.abs(z)))) / GNORM
    rows.drain()


def odd_in(dst, x1, yb, lrow, gate, mod0, mod1, n1g, wq, wk, wv, wgo, wgd, wgkf, bgkf, wgkb, bgkb):
    full = lambda w: pl.BlockSpec(w.shape, lambda i, d: (0,) * w.ndim)
    row = lambda w: pl.BlockSpec((TILE, w), lambda i, d: (i, 0))
    modspec = pl.BlockSpec((1, 6, D), lambda i, d: (_mod_index(i), 0, 0))
    grid_spec = pltpu.PrefetchScalarGridSpec(
        num_scalar_prefetch=1,
        grid=(NT_ALL,),
        in_specs=[row(D), pl.BlockSpec(memory_space=pl.ANY), row(LANES), row(LANES), modspec, modspec,
                  full(n1g), full(wq), full(wk), full(wv), full(wgo),
                  full(wgd), full(wgkf), full(bgkf), full(wgkb), full(bgkb)],
        out_specs=[row(D), row(GKW), row(GKW), row(GVW), row(GVW), row(GKW), row(GKW)],
        scratch_shapes=[pltpu.VMEM((2, SROWS, D), BF16), pltpu.SemaphoreType.DMA((2,))],
    )
    return pl.pallas_call(
        _odd_in_body,
        grid_spec=grid_spec,
        out_shape=[jax.ShapeDtypeStruct((T_ALL, D), F32),
                   jax.ShapeDtypeStruct((T_ALL, GKW), F32),
                   jax.ShapeDtypeStruct((T_ALL, GKW), F32),
                   jax.ShapeDtypeStruct((T_ALL, GVW), BF16),
                   jax.ShapeDtypeStruct((T_ALL, GVW), BF16),
                   jax.ShapeDtypeStruct((T_ALL, GKW), F32),
                   jax.ShapeDtypeStruct((T_ALL, GKW), F32)],
        compiler_params=_cp(("arbitrary",), VMEM_BIG),
        name="odd_in",
    )(dst, x1, yb, lrow, gate, mod0, mod1, n1g, wq, wk, wv, wgo, wgd, wgkf, bgkf, wgkb, bgkb)


def _gla_body(qf_ref, kf_ref, vf_ref, gf_ref, qb_ref, kb_ref, vb_ref, gb_ref, of_ref, ob_ref, sf_ref, sb_ref):
    @pl.when(pl.program_id(1) == 0)
    def _():
        sf_ref[...] = jnp.zeros_like(sf_ref)
        sb_ref[...] = jnp.zeros_like(sb_ref)

    r = lax.broadcasted_iota(jnp.int32, (GCHUNK, GCHUNK), 0)
    c = lax.broadcasted_iota(jnp.int32, (GCHUNK, GCHUNK), 1)
    dirs = []
    for s in range(GBS):
        at = lambda ref: ref.at[0, s]
        dirs.append((at(qf_ref), at(kf_ref), at(vf_ref), at(gf_ref), sf_ref.at[s], at(of_ref), c <= r, GCHUNK - 1))
        dirs.append((at(qb_ref), at(kb_ref), at(vb_ref), at(gb_ref), sb_ref.at[s], at(ob_ref), c >= r, 0))
    chains = []
    for q_ref, k_ref, v_ref, g_ref, s_ref, o_ref, mask, last in dirs:
        g = g_ref[...]
        g_hi = g.astype(BF16)
        g_lo = (g - g_hi.astype(F32)).astype(BF16)
        tri = mask.astype(BF16)
        cum = (jnp.dot(tri, g_hi, preferred_element_type=F32) + jnp.dot(tri, g_lo, preferred_element_type=F32))
        tot = cum[last:last + 1, :]
        q = q_ref[...]
        k = k_ref[...]
        qd = (q * jnp.exp(cum)).astype(BF16)
        kd = (k * jnp.exp(-cum)).astype(BF16)
        ke = (k * jnp.exp(tot - cum)).astype(BF16)
        dec = jnp.exp(tot)
        v = v_ref[...]
        for hd in range(GH):
            ks = slice(hd * GDK, (hd + 1) * GDK)
            vs = slice(hd * GDV, (hd + 1) * GDV)
            chains.append(dict(qd=qd[:, ks], kd=kd[:, ks], ke=ke[:, ks], dec=dec[:, ks], v=v[:, vs],
                               st=s_ref[hd], mask=mask))
    for ch in chains:
        sc = lax.dot_general(ch["qd"], ch["kd"], NT_DIMS, preferred_element_type=F32)
        ch["sc"] = jnp.where(ch["mask"], sc, 0.0).astype(BF16)
    for ch in chains:
        ch["out"] = (lax.dot_general(ch["qd"], ch["st"].astype(BF16), NT_DIMS, preferred_element_type=F32)
                     + jnp.dot(ch["sc"], ch["v"], preferred_element_type=F32))
    for ch in chains:
        ch["new"] = ch["st"] * ch["dec"] + lax.dot_general(ch["v"], ch["ke"], TN_DIMS, preferred_element_type=F32)
    for d, (_, _, _, _, s_ref, o_ref, _, _) in enumerate(dirs):
        o_ref[...] = jnp.concatenate([chains[d * GH + hd]["out"] for hd in range(GH)], axis=-1)
        for hd in range(GH):
            s_ref[hd] = chains[d * GH + hd]["new"]


def _bwd_chunk(c):
    return jnp.where(c < NCH_CTX, NCH_CTX - 1 - c, NCH_B + NCH_CTX - 1 - c)


def gla(q, k, v, gf, gb):
    fw = lambda w: pl.BlockSpec((1, GBS, GCHUNK, w), lambda b, c: (b, 0, c, 0))
    bw = lambda w: pl.BlockSpec((1, GBS, GCHUNK, w), lambda b, c: (b, 0, _bwd_chunk(c), 0))
    grouped = lambda a: a.reshape(B // GBS, GBS, ROWS_B, a.shape[-1])
    q, k, v, gf, gb = (grouped(a) for a in (q, k, v, gf, gb))
    oshape = jax.ShapeDtypeStruct((B // GBS, GBS, ROWS_B, GVW), F32)
    o_f, o_b = pl.pallas_call(
        _gla_body,
        grid=(B // GBS, NCH_B),
        in_specs=[fw(GKW), fw(GKW), fw(GVW), fw(GKW), bw(GKW), bw(GKW), bw(GVW), bw(GKW)],
        out_specs=[fw(GVW), bw(GVW)],
        out_shape=[oshape, oshape],
        scratch_shapes=[pltpu.VMEM((GBS, GH, GDV, GDK), F32), pltpu.VMEM((GBS, GH, GDV, GDK), F32)],
        compiler_params=_cp(("arbitrary", "arbitrary")),
        name="gla",
    )(q, k, v, gf, q, k, v, gb)
    return o_f.reshape(T_ALL, GVW), o_b.reshape(T_ALL, GVW)


def _out1_body(of0_ref, ob0_ref, go0_ref, x0_ref, of1_ref, ob1_ref, go1_ref, x1_ref,
               mod_ref, gn_ref, wo_ref, n2_ref, rw_ref, rb_ref,
               x3_ref, lrow_ref, gate_ref, cnt_ref, xs_ref):
    m = mod_ref[0]
    gn = gn_ref[...]
    tiles = ((of0_ref, ob0_ref, go0_ref, x0_ref), (of1_ref, ob1_ref, go1_ref, x1_ref))
    ons = []
    for of_ref, ob_ref, go_ref, _ in tiles:
        o = of_ref[...] + ob_ref[...]
        parts = [_rms(o[:, hd * GDV:(hd + 1) * GDV], gn) for hd in range(GH)]
        ons.append((jnp.concatenate(parts, axis=-1) * _silu(go_ref[...].astype(F32))).astype(BF16))
    ys = [jnp.dot(on, wo_ref[...], preferred_element_type=F32) for on in ons]
    h2s, outs = [], []
    for k, ((_, _, _, x_ref), y) in enumerate(zip(tiles, ys)):
        rs = pl.ds(k * TILE, TILE)
        x3 = x_ref[...] + m[2:3] * y
        x3_ref[rs, :] = x3
        h2s.append(_modulate(x3, n2_ref[...], m[3:4], m[4:5]))
        outs.append((lrow_ref.at[rs, :], gate_ref.at[rs, :], cnt_ref.at[k], xs_ref.at[pl.ds(k * SROWS, SROWS), :]))
    _route_tiles(h2s, rw_ref, rb_ref, outs)


def out1(of, ob, go, x2, mod, gng, wo, n2g, rw, rb):
    full = lambda w: pl.BlockSpec(w.shape, lambda n: (0,) * w.ndim)
    lat = lambda w, k: pl.BlockSpec((TILE, w), lambda n: (_lat_tile(2 * n + k), 0))
    return pl.pallas_call(
        _out1_body,
        grid=(NT_LAT // 2,),
        in_specs=[lat(GVW, 0), lat(GVW, 0), lat(GVW, 0), lat(D, 0), lat(GVW, 1), lat(GVW, 1), lat(GVW, 1), lat(D, 1),
                  pl.BlockSpec((1, 6, D), lambda n: (2 * n // LPB, 0, 0)),
                  full(gng), full(wo), full(n2g), full(rw), full(rb)],
        out_specs=[_pair_spec((TILE, D))] + ROUTE_PAIR_SPECS,
        out_shape=[jax.ShapeDtypeStruct((T_LAT, D), F32)] + _route_out_shapes(NT_LAT),
        compiler_params=_cp(("arbitrary",), VMEM_BIG),
        name="out1",
    )(of, ob, go, x2, of, ob, go, x2, mod, gng, wo, n2g, rw, rb)


def _final_body(dst_ref, x_ref, yb_hbm, lrow_ref, gate_ref, mod_ref, g_ref, o_ref, ybuf, sem):
    m = mod_ref[0]
    ys, rows = _combine(dst_ref, yb_hbm, ybuf, sem, lrow_ref, gate_ref)
    for rs, y in zip(HALVES, ys):
        o_ref[rs, :] = _rms(x_ref[rs, :] + m[5:6] * y, g_ref[...])
    rows.drain()


def final(dst, x3, yb, lrow, gate, mod, g):
    grid_spec = pltpu.PrefetchScalarGridSpec(
        num_scalar_prefetch=1,
        grid=(NT_LAT,),
        in_specs=[pl.BlockSpec((TILE, D), lambda n, d: (n, 0)),
                  pl.BlockSpec(memory_space=pl.ANY),
                  pl.BlockSpec((TILE, LANES), lambda n, d: (n, 0)),
                  pl.BlockSpec((TILE, LANES), lambda n, d: (n, 0)),
                  pl.BlockSpec((1, 6, D), lambda n, d: (n // LPB, 0, 0)),
                  pl.BlockSpec((1, D), lambda n, d: (0, 0))],
        out_specs=pl.BlockSpec((TILE, D), lambda n, d: (n, 0)),
        scratch_shapes=[pltpu.VMEM((2, SROWS, D), BF16), pltpu.SemaphoreType.DMA((2,))],
    )
    return pl.pallas_call(
        _final_body,
        grid_spec=grid_spec,
        out_shape=jax.ShapeDtypeStruct((T_LAT, D), F32),
        compiler_params=_cp(("arbitrary",)),
        name="final",
    )(dst, x3, yb, lrow, gate, mod, g)


def _rot_cols(w):
    a, b, c, d = jnp.split(w, 4, axis=-1)
    return jnp.concatenate([-b, a, -d, c], axis=-1)


def _rope_tables():
    n = np.arange(SEQ)
    row = (n // GRID_W).astype(np.float32)
    col = (n % GRID_W).astype(np.float32)
    axis_dim = ROPE // 2
    inv = (ROPE_BASE ** (-np.arange(0, axis_dim, 2, dtype=np.float32) / axis_dim)).astype(np.float32)
    ar = row[:, None] * inv
    ac = col[:, None] * inv
    ang = np.concatenate([ar, ar, ac, ac], axis=-1).astype(np.float32)
    cos = np.concatenate([np.ones((CTX, ROPE), np.float32), np.cos(ang)], axis=0)
    sin = np.concatenate([np.zeros((CTX, ROPE), np.float32), np.sin(ang)], axis=0)
    return jnp.asarray(np.tile(cos, (1, HEADS))), jnp.asarray(np.tile(sin, (1, HEADS)))


def _dft(n, scale):
    j = np.arange(n, dtype=np.int64)
    ang = 2.0 * np.pi * ((j[:, None] * j[None, :]) % n).astype(np.float64) / n
    return (np.cos(ang) * scale), (np.sin(ang) * scale)


def _dft_tables():
    cn, sn = _dft(SEQ, SEQ ** -0.5)
    c2, s2 = _dft(CTX, CTX ** -0.5)
    cg, sg = _dft(FGD, FGD ** -0.5)
    eye = np.eye(FG)
    to = lambda a: jnp.asarray(a.astype(np.float32)).astype(BF16)
    return to(cn), to(sn), to(c2), to(s2), to(np.kron(eye, cg)), to(np.kron(eye, sg))


def _router_pad(rw, rb):
    rwp = jnp.zeros((LANES, D), F32).at[:NE].set(rw.T)
    rbp = jnp.full((LANES, 1), -1e30, F32).at[:NE, 0].set(rb)
    return rwp, rbp


def kernel(x, c, ctx, c_ctx, final_norm_g, l0_mod_w, l0_mod_b, l0_norm1_g, l0_w_in, l0_q_norm_g, l0_w_q_up, l0_kv_norm_g, l0_w_kv_up, l0_w_out, l0_norm2_g, l0_router_w, l0_router_b, l0_w_gu, l0_b_gu, l0_w_down, l0_b_down, l1_mod_w, l1_mod_b, l1_norm1_g, l1_w_in, l1_w_gk_fwd, l1_b_gk_fwd, l1_w_gk_bwd, l1_b_gk_bwd, l1_gnorm_g, l1_w_out, l1_norm2_g, l1_router_w, l1_router_b, l1_w_gu, l1_b_gu, l1_w_down, l1_b_down):
    row = lambda g: g.reshape(1, -1)
    c2 = ctx.reshape(B * CTX, D)
    x2 = x.reshape(T_LAT, D)
    cc = jnp.zeros((16, D), F32).at[:B].set(c).at[B].set(c_ctx)
    mod0 = adaln(cc, l0_mod_w, l0_mod_b)
    mod1 = adaln(cc, l1_mod_w, l1_mod_b)

    w_in = l0_w_in
    wqa = w_in[:, :Q_LORA].astype(BF16)
    wkva = w_in[:, Q_LORA:Q_LORA + KV_LORA].astype(BF16)
    wkr = w_in[:, Q_LORA + KV_LORA:Q_LORA + KV_LORA + ROPE]
    wkr2 = jnp.concatenate([wkr, _rot_cols(wkr)], axis=-1).astype(BF16)
    wf = w_in[:, Q_LORA + KV_LORA + ROPE:].astype(BF16)
    wq3 = l0_w_q_up.reshape(Q_LORA, HEADS, QK)
    wq_nope = wq3[:, :, :NOPE].reshape(Q_LORA, HEADS * NOPE)
    wq_rope = wq3[:, :, NOPE:]
    wqup = jnp.concatenate([wq_nope, wq_rope.reshape(Q_LORA, HEADS * ROPE),
                            _rot_cols(wq_rope).reshape(Q_LORA, HEADS * ROPE)], axis=-1).astype(BF16)
    wkv3 = l0_w_kv_up.reshape(KV_LORA, HEADS, NOPE + VDIM)
    wkvup = jnp.concatenate([wkv3[:, :, :NOPE].reshape(KV_LORA, HEADS * NOPE),
                             wkv3[:, :, NOPE:].reshape(KV_LORA, HEADS * VDIM)], axis=-1).astype(BF16)
    cos6, sin6 = _rope_tables()
    q, k, v, uf = even_in(c2, x2, mod0, row(l0_norm1_g), wqa, wkva, wkr2, wf, row(l0_q_norm_g), wqup,
                          row(l0_kv_norm_g), wkvup, cos6, sin6)
    att = attention(q, k, v).reshape(T_ALL, HEADS * VDIM)
    fmix = fourier(uf.reshape(B, ROWS_B, FW), *_dft_tables()).reshape(T_ALL, FW)
    rw0, rb0 = _router_pad(l0_router_w, l0_router_b)
    x1, lrow0, gate0, cnt0, xs0 = out0(c2, x2, att, fmix, mod0, l0_w_out[:HEADS * VDIM].astype(BF16),
                                       l0_w_out[HEADS * VDIM:].astype(BF16), row(l0_norm2_g), rw0, rb0)
    src0, be0, bv0, dst0, nb0 = _chunk_lists(cnt0, NT_ALL)
    yb0 = expert_ffn(xs0, src0, be0, bv0, nb0, l0_w_gu, l0_b_gu, l0_w_down, l0_b_down)

    w1 = l1_w_in
    o = 0
    wq1 = w1[:, o:o + GKW].astype(BF16); o += GKW
    wk1 = w1[:, o:o + GKW].astype(BF16); o += GKW
    wv1 = w1[:, o:o + GVW].astype(BF16); o += GVW
    wgo = w1[:, o:o + GVW].astype(BF16); o += GVW
    wgd = jnp.zeros((D, LANES), F32).at[:, :2 * GRANK].set(w1[:, o:]).astype(BF16)
    wgkf = jnp.zeros((LANES, GKW), F32).at[:GRANK].set(l1_w_gk_fwd)
    wgkb = jnp.zeros((LANES, GKW), F32).at[GRANK:2 * GRANK].set(l1_w_gk_bwd)
    x2, q1, k1, v1, go, gf, gb = odd_in(dst0, x1, yb0, lrow0, gate0, mod0, mod1, row(l1_norm1_g), wq1, wk1, wv1,
                                        wgo, wgd, wgkf, row(l1_b_gk_fwd), wgkb, row(l1_b_gk_bwd))
    o_f, o_b = gla(q1, k1, v1, gf, gb)
    rw1, rb1 = _router_pad(l1_router_w, l1_router_b)
    x3, lrow1, gate1, cnt1, xs1 = out1(o_f, o_b, go, x2, mod1, row(l1_gnorm_g), l1_w_out.astype(BF16),
                                       row(l1_norm2_g), rw1, rb1)
    src1, be1, bv1, dst1, nb1 = _chunk_lists(cnt1, NT_LAT)
    yb1 = expert_ffn(xs1, src1, be1, bv1, nb1, l1_w_gu, l1_b_gu, l1_w_down, l1_b_down)
    out = final(dst1, x3, yb1, lrow1, gate1, mod1, row(final_norm_g))
    return out.reshape(B, SEQ, D)
```

```python
import jax
import jax.numpy as jnp
import numpy as np
from jax import lax
from jax.experimental import pallas as pl
from jax.experimental.pallas import tpu as pltpu

D = 1024
B = 8
SEQ = 4096
CTX = 256
GRID_W = 64
EPS = 1e-6
HEADS = 6
Q_LORA = 384
KV_LORA = 256
NOPE = 128
ROPE = 64
VDIM = 128
QK = NOPE + ROPE
ROPE_BASE = 10000.0
FG = 4
FGD = 64
FW = FG * FGD
GH = 4
GDK = 128
GDV = 256
GRANK = 16
GNORM = 16.0
GCHUNK = 64
GKW = GH * GDK
GVW = GH * GDV
NE = 32
TOPK = 4
DFF = 1024
ALPHA = 1.702
LIMIT = 7.0

LANES = 128
TILE = 256
HALF = TILE // 2
ROWS_B = CTX + SEQ
TPB = ROWS_B // TILE
T_ALL = B * ROWS_B
NT_ALL = T_ALL // TILE
T_LAT = B * SEQ
NT_LAT = T_LAT // TILE
LPB = SEQ // TILE
HPS = 3
CH = 16
MAXCH = (TILE * TOPK + NE * (CH - 1)) // CH
NCHT = -(-MAXCH * CH // LANES) * LANES // CH
SROWS = NCHT * CH
SGRP = 512
FBLK = 512
CPB = FBLK // CH
GBS = 8
NCH_B = ROWS_B // GCHUNK
NCH_CTX = CTX // GCHUNK
V7X_VMEM_BYTES = 64 * 1024 * 1024
VMEM_BIG = V7X_VMEM_BYTES * 7 // 8
VMEM_MID = V7X_VMEM_BYTES * 5 // 8

F32 = jnp.float32
BF16 = jnp.bfloat16
HI = lax.Precision.HIGHEST
NT_DIMS = (((1,), (1,)), ((), ()))
TN_DIMS = (((0,), (0,)), ((), ()))


def _cp(sem, vmem=VMEM_MID):
    return pltpu.CompilerParams(dimension_semantics=sem, vmem_limit_bytes=vmem)


def _rms(x, g):
    return x * lax.rsqrt(jnp.mean(x * x, axis=-1, keepdims=True) + EPS) * g


def _modulate(x, g, sh, sc):
    return _rms(x, g) * (1.0 + sc) + sh


def _silu(x):
    return x * jax.nn.sigmoid(x)


def _mod_index(i):
    return jnp.where(i % TPB == 0, B, i // TPB)


def _lat_tile(n):
    return n + n // LPB + 1


def _adaln_body(c_ref, w_ref, b_ref, o_ref):
    s = _silu(c_ref[...])
    o_ref[...] = jnp.dot(s, w_ref[...], precision=HI, preferred_element_type=F32) + b_ref[...]


def adaln(cc, mod_w, mod_b):
    tn = 1536
    out = pl.pallas_call(
        _adaln_body,
        grid=(6 * D // tn,),
        in_specs=[pl.BlockSpec((16, D), lambda j: (0, 0)),
                  pl.BlockSpec((D, tn), lambda j: (0, j)),
                  pl.BlockSpec((1, tn), lambda j: (0, j))],
        out_specs=pl.BlockSpec((16, tn), lambda j: (0, j)),
        out_shape=jax.ShapeDtypeStruct((16, 6 * D), F32),
        compiler_params=_cp(("arbitrary",)),
        name="adaln",
    )(cc, mod_w, mod_b.reshape(1, 6 * D))
    return out.reshape(16, 6, D)


def _stream_tile(c_ref, x_ref):
    return jnp.where(pl.program_id(0) % TPB == 0, c_ref[...], x_ref[...])


def _stream_specs():
    return [pl.BlockSpec((TILE, D), lambda i: (i // TPB, 0)),
            pl.BlockSpec((TILE, D), lambda i: ((i // TPB) * LPB + jnp.maximum(i % TPB - 1, 0), 0))]


def _even_in_body(c_ref, x_ref, mod_ref, n1_ref, wqa_ref, wkva_ref, wkr_ref, wf_ref, qg_ref, wqup_ref,
                  kvg_ref, wkvup_ref, cos_ref, sin_ref, q_ref, k_ref, v_ref, uf_ref):
    m = mod_ref[0]
    h = _modulate(_stream_tile(c_ref, x_ref), n1_ref[...], m[0:1], m[1:2]).astype(BF16)
    cq = jnp.dot(h, wqa_ref[...], preferred_element_type=F32)
    ckv = jnp.dot(h, wkva_ref[...], preferred_element_type=F32)
    kr2 = jnp.dot(h, wkr_ref[...], preferred_element_type=F32)
    uf_ref[...] = jnp.dot(h, wf_ref[...], preferred_element_type=F32).astype(BF16)
    cos6 = cos_ref[...]
    sin6 = sin_ref[...]
    scale = QK ** -0.5
    qall = jnp.dot(_rms(cq, qg_ref[...]).astype(BF16), wqup_ref[...], preferred_element_type=F32)
    qn = qall[:, :HEADS * NOPE] * scale
    qr = (qall[:, HEADS * NOPE:HEADS * QK] * cos6 + qall[:, HEADS * QK:] * sin6) * scale
    kvall = jnp.dot(_rms(ckv, kvg_ref[...]).astype(BF16), wkvup_ref[...], preferred_element_type=F32)
    kr = (kr2[:, :ROPE] * cos6[:, :ROPE] + kr2[:, ROPE:] * sin6[:, :ROPE]).astype(BF16)
    ones_col = (lax.broadcasted_iota(jnp.int32, (TILE, VDIM), 1) == 0).astype(BF16)
    for hd in range(HEADS):
        q_ref[0, hd, :, 0:NOPE] = qn[:, hd * NOPE:(hd + 1) * NOPE].astype(BF16)
        q_ref[0, hd, :, NOPE:QK] = qr[:, hd * ROPE:(hd + 1) * ROPE].astype(BF16)
        k_ref[0, hd, :, 0:NOPE] = kvall[:, hd * NOPE:(hd + 1) * NOPE].astype(BF16)
        k_ref[0, hd, :, NOPE:QK] = kr
        v_ref[0, hd, :, 0:VDIM] = kvall[:, HEADS * NOPE + hd * VDIM:HEADS * NOPE + (hd + 1) * VDIM].astype(BF16)
        v_ref[0, hd, :, VDIM:2 * VDIM] = ones_col


def even_in(c2, x2, mod, n1g, wqa, wkva, wkr2, wf, qg, wqup, kvg, wkvup, cos6, sin6):
    full = lambda a: pl.BlockSpec(a.shape, lambda i: (0,) * a.ndim)
    hs = lambda w: pl.BlockSpec((1, HEADS, TILE, w), lambda i: (i // TPB, 0, i % TPB, 0))
    return pl.pallas_call(
        _even_in_body,
        grid=(NT_ALL,),
        in_specs=_stream_specs() + [
                  pl.BlockSpec((1, 6, D), lambda i: (_mod_index(i), 0, 0)),
                  full(n1g), full(wqa), full(wkva), full(wkr2), full(wf), full(qg), full(wqup),
                  full(kvg), full(wkvup),
                  pl.BlockSpec((TILE, HEADS * ROPE), lambda i: (i % TPB, 0)),
                  pl.BlockSpec((TILE, HEADS * ROPE), lambda i: (i % TPB, 0))],
        out_specs=[hs(QK), hs(QK), hs(2 * VDIM), pl.BlockSpec((TILE, FW), lambda i: (i, 0))],
        out_shape=[jax.ShapeDtypeStruct((B, HEADS, ROWS_B, QK), BF16),
                   jax.ShapeDtypeStruct((B, HEADS, ROWS_B, QK), BF16),
                   jax.ShapeDtypeStruct((B, HEADS, ROWS_B, 2 * VDIM), BF16),
                   jax.ShapeDtypeStruct((T_ALL, FW), BF16)],
        compiler_params=_cp(("arbitrary",)),
        name="even_in",
    )(c2, x2, mod, n1g, wqa, wkva, wkr2, wf, qg, wqup, kvg, wkvup, cos6, sin6)


def _attend(q_ref, k_ref, v_ref, o_ref, rows):
    ks = [k_ref[0, j, 0:rows, :] for j in range(HPS)]
    ss = [lax.dot_general(q_ref[0, j], ks[j], NT_DIMS, preferred_element_type=F32) for j in range(HPS)]
    ps = [jnp.exp(s - jnp.max(s, axis=-1, keepdims=True)).astype(BF16) for s in ss]
    ov = [jnp.dot(ps[j], v_ref[0, j, 0:rows, :], preferred_element_type=F32) for j in range(HPS)]
    o_ref[0] = jnp.concatenate([o[:, :VDIM] / o[:, VDIM:VDIM + 1] for o in ov], axis=-1).astype(BF16)


def _attn_body(q_ref, k_ref, v_ref, o_ref):
    qt = pl.program_id(2)

    @pl.when(qt == 0)
    def _():
        _attend(q_ref, k_ref, v_ref, o_ref, CTX)

    @pl.when(qt > 0)
    def _():
        _attend(q_ref, k_ref, v_ref, o_ref, ROWS_B)


def attention(q, k, v):
    return pl.pallas_call(
        _attn_body,
        grid=(B, HEADS // HPS, TPB),
        in_specs=[pl.BlockSpec((1, HPS, TILE, QK), lambda b, h, t: (b, h, t, 0)),
                  pl.BlockSpec((1, HPS, ROWS_B, QK), lambda b, h, t: (b, h, 0, 0)),
                  pl.BlockSpec((1, HPS, ROWS_B, 2 * VDIM), lambda b, h, t: (b, h, 0, 0))],
        out_specs=pl.BlockSpec((1, TILE, HPS * VDIM), lambda b, h, t: (b, t, h)),
        out_shape=jax.ShapeDtypeStruct((B, ROWS_B, HEADS * VDIM), BF16),
        compiler_params=_cp(("arbitrary", "arbitrary", "arbitrary"), VMEM_BIG),
        name="attention",
    )(q, k, v)


def _fourier_body(u_ref, cn_ref, sn_ref, c2_ref, s2_ref, cc_ref, sc_ref, o_ref):
    i = pl.program_id(0)

    def finish(a, b):
        o_ref[0] = (jnp.dot(a.astype(BF16), cc_ref[...], preferred_element_type=F32)
                    - jnp.dot(b.astype(BF16), sc_ref[...], preferred_element_type=F32)).astype(BF16)

    @pl.when(i == 0)
    def _():
        u = u_ref[0, 0:CTX, :]
        finish(jnp.dot(c2_ref[...], u, preferred_element_type=F32),
               jnp.dot(s2_ref[...], u, preferred_element_type=F32))

    @pl.when(i > 0)
    def _():
        u = u_ref[0, CTX:ROWS_B, :]
        finish(jnp.dot(cn_ref[...], u, preferred_element_type=F32),
               jnp.dot(sn_ref[...], u, preferred_element_type=F32))


def fourier(uf, cn, sn, c2, s2, ccb, scb):
    full = lambda a: pl.BlockSpec(a.shape, lambda i, b: (0,) * a.ndim)
    return pl.pallas_call(
        _fourier_body,
        grid=(TPB, B),
        in_specs=[pl.BlockSpec((1, ROWS_B, FW), lambda i, b: (b, 0, 0)),
                  pl.BlockSpec((TILE, SEQ), lambda i, b: (jnp.maximum(i - 1, 0), 0)),
                  pl.BlockSpec((TILE, SEQ), lambda i, b: (jnp.maximum(i - 1, 0), 0)),
                  full(c2), full(s2), full(ccb), full(scb)],
        out_specs=pl.BlockSpec((1, TILE, FW), lambda i, b: (b, i, 0)),
        out_shape=jax.ShapeDtypeStruct((B, ROWS_B, FW), BF16),
        compiler_params=_cp(("arbitrary", "arbitrary")),
        name="fourier",
    )(uf, cn, sn, c2, s2, ccb, scb)


def _route_tiles(h2s, rw_ref, rb_ref, outs):
    rw = rw_ref[...]
    w_hi = rw.astype(BF16)
    w_lo = (rw - w_hi.astype(F32)).astype(BF16)
    w_both = jnp.concatenate([w_hi, w_lo], axis=0)
    h_his = [h2.astype(BF16) for h2 in h2s]
    h_los = [(h2 - h_hi.astype(F32)).astype(BF16) for h2, h_hi in zip(h2s, h_his)]
    boths = [lax.dot_general(w_both, h_hi, NT_DIMS, preferred_element_type=F32) for h_hi in h_his]
    ls = [both[:LANES] + both[LANES:] + lax.dot_general(w_hi, h_lo, NT_DIMS, preferred_element_type=F32)
          + rb_ref[...] for both, h_lo in zip(boths, h_los)]
    sub = lax.broadcasted_iota(jnp.int32, (LANES, TILE), 0)
    picks = []
    for l in ls:
        vals, idxs = [], []
        for _ in range(TOPK):
            mx = jnp.max(l, axis=0, keepdims=True)
            am = jnp.min(jnp.where(l == mx, sub, LANES), axis=0, keepdims=True)
            vals.append(mx)
            idxs.append(am)
            l = jnp.where(sub == am, -jnp.inf, l)
        ex = [jnp.exp(vv - vals[0]) for vv in vals]
        den = ex[0] + ex[1] + ex[2] + ex[3]
        onehot = jnp.zeros((LANES, TILE), F32)
        for am in idxs:
            onehot = onehot + (sub == am).astype(F32)
        picks.append((idxs, ex, den, onehot))
    r = lax.broadcasted_iota(jnp.int32, (TILE, TILE), 0)
    c = lax.broadcasted_iota(jnp.int32, (TILE, TILE), 1)
    earlier = (r < c).astype(BF16)
    ranks = [jnp.dot(onehot.astype(BF16), earlier, preferred_element_type=F32)
             for _, _, _, onehot in picks]
    er = lax.broadcasted_iota(jnp.int32, (LANES, LANES), 0)
    ec = lax.broadcasted_iota(jnp.int32, (LANES, LANES), 1)
    below = (ec < er).astype(BF16)
    cnts = [jnp.sum(onehot, axis=1, keepdims=True) for _, _, _, onehot in picks]
    segs = [jnp.dot(below, jnp.broadcast_to(jnp.ceil(cnt * (1.0 / CH)), (LANES, LANES)).astype(BF16),
                    preferred_element_type=F32)[:, 0:1] * CH
            for cnt in cnts]
    all_rows = []
    for (idxs, ex, den, _), rank, seg, cnt, (lrow_ref, gate_ref, cnt_ref, _) in zip(picks, ranks, segs, cnts, outs):
        dest = rank + seg
        lrow_t = jnp.zeros((LANES, TILE), F32)
        gate_t = jnp.zeros((LANES, TILE), F32)
        rows = []
        for kk in range(TOPK):
            pos = jnp.sum(jnp.where(sub == idxs[kk], dest, 0.0), axis=0, keepdims=True)
            rows.append(pos.astype(jnp.int32))
            lrow_t = jnp.where(sub == kk, pos, lrow_t)
            gate_t = jnp.where(sub == kk, ex[kk] / den, gate_t)
        lrow_ref[...] = lrow_t.T.astype(jnp.int32)
        gate_ref[...] = gate_t.T
        cnt_ref[...] = jnp.broadcast_to(cnt, (LANES, LANES)).T[0:8, :]
        all_rows.append(rows)
    for g in range(SROWS // SGRP):
        srow = g * SGRP + lax.broadcasted_iota(jnp.int32, (SGRP, TILE), 0)
        for rows, h_hi, (_, _, _, xs_ref) in zip(all_rows, h_his, outs):
            hit = rows[0] == srow
            for kk in range(1, TOPK):
                hit = jnp.logical_or(hit, rows[kk] == srow)
            xs_ref[g * SGRP:(g + 1) * SGRP, :] = jnp.dot(hit.astype(BF16), h_hi,
                                                        preferred_element_type=F32).astype(BF16)


def _route_out_shapes(n_tiles):
    return [jax.ShapeDtypeStruct((n_tiles * TILE, LANES), jnp.int32),
            jax.ShapeDtypeStruct((n_tiles * TILE, LANES), F32),
            jax.ShapeDtypeStruct((n_tiles, 8, LANES), F32),
            jax.ShapeDtypeStruct((n_tiles * SROWS, D), BF16)]


def _pair_spec(shp):
    return pl.BlockSpec((2 * shp[0],) + shp[1:], lambda n: (n,) + (0,) * (len(shp) - 1))


ROUTE_PAIR_SPECS = [_pair_spec((TILE, LANES)), _pair_spec((TILE, LANES)), _pair_spec((1, 8, LANES)),
                    _pair_spec((SROWS, D))]


def _out0_body(c0_ref, x0_ref, mod0_ref, c1_ref, x1_ref, mod1_ref, a_ref, f_ref, woa_ref, wof_ref, n2_ref,
               rw_ref, rb_ref, xo_ref, lrow_ref, gate_ref, cnt_ref, xs_ref):
    tiles = ((c0_ref, x0_ref, mod0_ref), (c1_ref, x1_ref, mod1_ref))
    ys = []
    for k in range(2):
        rs = pl.ds(k * TILE, TILE)
        ys.append(jnp.dot(a_ref[rs, :], woa_ref[...], preferred_element_type=F32)
                  + jnp.dot(f_ref[rs, :], wof_ref[...], preferred_element_type=F32))
    h2s, outs = [], []
    for k, ((c_ref, x_ref, mod_ref), y) in enumerate(zip(tiles, ys)):
        rs = pl.ds(k * TILE, TILE)
        m = mod_ref[0]
        is_ctx = (2 * pl.program_id(0) + k) % TPB == 0
        x1 = jnp.where(is_ctx, c_ref[...], x_ref[...]) + m[2:3] * y
        xo_ref[rs, :] = x1
        h2s.append(_modulate(x1, n2_ref[...], m[3:4], m[4:5]))
        outs.append((lrow_ref.at[rs, :], gate_ref.at[rs, :], cnt_ref.at[k], xs_ref.at[pl.ds(k * SROWS, SROWS), :]))
    _route_tiles(h2s, rw_ref, rb_ref, outs)


def out0(c2, x2, a, f, mod, woa, wof, n2g, rw, rb):
    full = lambda w: pl.BlockSpec(w.shape, lambda n: (0,) * w.ndim)

    def stream(k):
        tile = lambda n: 2 * n + k
        return [pl.BlockSpec((TILE, D), lambda n: (tile(n) // TPB, 0)),
                pl.BlockSpec((TILE, D), lambda n: ((tile(n) // TPB) * LPB + jnp.maximum(tile(n) % TPB - 1, 0), 0)),
                pl.BlockSpec((1, 6, D), lambda n: (_mod_index(tile(n)), 0, 0))]

    return pl.pallas_call(
        _out0_body,
        grid=(NT_ALL // 2,),
        in_specs=stream(0) + stream(1) + [_pair_spec((TILE, HEADS * VDIM)), _pair_spec((TILE, FW)),
                                          full(woa), full(wof), full(n2g), full(rw), full(rb)],
        out_specs=[_pair_spec((TILE, D))] + ROUTE_PAIR_SPECS,
        out_shape=[jax.ShapeDtypeStruct((T_ALL, D), F32)] + _route_out_shapes(NT_ALL),
        compiler_params=_cp(("arbitrary",), VMEM_BIG),
        name="out0",
    )(c2, x2, mod, c2, x2, mod, a, f, woa, wof, n2g, rw, rb)


def _chunk_lists(cnt, n_tiles):
    n_blk = (n_tiles * MAXCH + CPB - 1) // CPB + NE
    cnt = cnt[:, 0, :NE].astype(jnp.int32)
    nch = (cnt + CH - 1) // CH
    lend = jnp.cumsum(nch, axis=1)
    loff = lend - nch
    tend = jnp.cumsum(nch, axis=0)
    toff = tend - nch
    tot = tend[-1]
    blk = (tot + CPB - 1) // CPB
    bend = jnp.cumsum(blk)
    estart = (bend - blk) * CPB
    nvalid = bend[-1]
    bi = jnp.arange(n_blk, dtype=jnp.int32)
    bvalid = bi < nvalid
    count_le = lambda ends, v: jnp.sum((ends[None, :] <= v[:, None]).astype(jnp.int32), axis=1)
    be = jnp.where(bvalid, jnp.minimum(count_le(bend, bi), NE - 1), jnp.sum((bend < nvalid).astype(jnp.int32)))
    g = jnp.arange(n_blk * CPB, dtype=jnp.int32)
    sel = (jnp.repeat(be, CPB)[:, None] == jnp.arange(NE, dtype=jnp.int32)[None, :]).astype(F32)
    pick = lambda table: jnp.dot(sel, table.astype(F32), precision=HI).astype(jnp.int32)
    o = g - pick(estart[:, None])[:, 0]
    gvalid = jnp.logical_and(jnp.repeat(bvalid, CPB), o < pick(tot[:, None])[:, 0])
    tend_g = pick(tend.T)
    tile = jnp.minimum(jnp.sum((tend_g <= o[:, None]).astype(jnp.int32), axis=1), n_tiles - 1)
    tsel = tile[:, None] == jnp.arange(n_tiles, dtype=jnp.int32)[None, :]
    shift = jnp.sum(jnp.where(tsel, pick((loff - toff).T), 0), axis=1)
    src = jnp.where(gvalid, tile * NCHT + o + shift, 0).astype(jnp.int32)
    cslot = jnp.arange(NCHT, dtype=jnp.int32)
    inside = jnp.logical_and(loff[:, None, :] <= cslot[None, :, None], cslot[None, :, None] < lend[:, None, :])
    base = estart[None, :] + toff - loff
    dst = cslot[None, :] + jnp.sum(jnp.where(inside, base[:, None, :], 0), axis=2)
    dst = jnp.where(cslot[None, :] < lend[:, -1:], dst, 0).astype(jnp.int32)
    src = jnp.concatenate([src, jnp.zeros((CPB,), jnp.int32)])
    dst = jnp.concatenate([dst.reshape(-1), jnp.zeros((2 * NCHT,), jnp.int32)])
    return src, be.astype(jnp.int32), bvalid.astype(jnp.int32), dst, n_blk


def _chunk_copy(src_hbm, chunk, buf, slot, j, sem):
    start = chunk * CH if isinstance(chunk, int) else pl.multiple_of(chunk * CH, CH)
    return pltpu.make_async_copy(src_hbm.at[pl.ds(start, CH), :],
                                 buf.at[slot, pl.ds(j * CH, CH), :], sem.at[slot])


class _ChunkGather:
    def __init__(self, list_ref, per_step, src_hbm, buf, sem):
        self.list_ref, self.per_step, self.src_hbm, self.buf, self.sem = list_ref, per_step, src_hbm, buf, sem
        self.step = pl.program_id(0)
        self.n_steps = pl.num_programs(0)

    def _issue(self, st, slot):
        for j in range(self.per_step):
            _chunk_copy(self.src_hbm, self.list_ref[st * self.per_step + j], self.buf, slot, j, self.sem).start()

    def _wait(self, slot):
        for j in range(self.per_step):
            _chunk_copy(self.src_hbm, 0, self.buf, slot, j, self.sem).wait()

    def arrive(self):
        @pl.when(self.step == 0)
        def _():
            self._issue(0, 0)

        slot = self.step % 2
        self._wait(slot)
        return slot

    def fetch_next(self):
        self._issue(self.step + 1, (self.step + 1) % 2)

    def drain(self):
        @pl.when(self.step == self.n_steps - 1)
        def _():
            self._wait((self.step + 1) % 2)


def _ffn_body(src_ref, be_ref, bv_ref, bf_ref, ws_ref, ne_ref, xs_hbm, wgu_hbm, bg_ref, bl_ref, wdn_hbm, bd_ref,
              yb_ref, xbuf, sem, wg_s, wl_s, wd_s, wgu_buf, wdn_buf, wsem):
    i = pl.program_id(0)
    rows = _ChunkGather(src_ref, CPB, xs_hbm, xbuf, sem)
    slot = rows.arrive()

    def weight_copies(e, s):
        return (pltpu.make_async_copy(wgu_hbm.at[e], wgu_buf.at[s], wsem.at[0, s]),
                pltpu.make_async_copy(wdn_hbm.at[e], wdn_buf.at[s], wsem.at[1, s]))

    @pl.when(bf_ref[i] == 1)
    def _():
        s = ws_ref[i]

        @pl.when(i == 0)
        def _():
            for cp in weight_copies(be_ref[0], 0):
                cp.start()

        for cp in weight_copies(be_ref[i], s):
            cp.wait()

        @pl.when(ne_ref[i] >= 0)
        def _():
            for cp in weight_copies(ne_ref[i], 1 - s):
                cp.start()

        r = lax.broadcasted_iota(jnp.int32, (2 * LANES, 2 * LANES), 0)
        c = lax.broadcasted_iota(jnp.int32, (2 * LANES, 2 * LANES), 1)
        split = (r == 2 * (c % LANES) + c // LANES).astype(BF16)
        for g in range(DFF // LANES):
            pair = wgu_buf[s, :, pl.ds(g * 2 * LANES, 2 * LANES)].astype(BF16)
            sep = jnp.dot(pair, split, preferred_element_type=F32)
            wg_s[:, g * LANES:(g + 1) * LANES] = sep[:, :LANES].astype(BF16)
            wl_s[:, g * LANES:(g + 1) * LANES] = sep[:, LANES:].astype(BF16)
        wd_s[...] = wdn_buf[s].astype(BF16)

    @pl.when(bv_ref[i] == 1)
    def _():
        rows.fetch_next()
        xb = xbuf[slot]
        glu = jnp.dot(xb, wg_s[...], preferred_element_type=F32) + bg_ref[0]
        lin = jnp.dot(xb, wl_s[...], preferred_element_type=F32) + bl_ref[0]
        glu = jnp.minimum(glu, LIMIT)
        lin = jnp.clip(lin, -LIMIT, LIMIT)
        act = glu * jax.nn.sigmoid(ALPHA * glu) * (lin + 1.0)
        out = jnp.dot(act.astype(BF16), wd_s[...], preferred_element_type=F32) + bd_ref[0]
        yb_ref[...] = out.astype(BF16)

    @pl.when(bv_ref[i] == 0)
    def _():
        rows.fetch_next()
        yb_ref[...] = jnp.zeros_like(yb_ref)

    rows.drain()


def expert_ffn(xs, src, be, bv, n_blk, w_gu, b_gu, w_down, b_down):
    bi = jnp.arange(n_blk, dtype=jnp.int32)
    bf = jnp.concatenate([jnp.ones((1,), jnp.int32), (be[1:] != be[:-1]).astype(jnp.int32)])
    ws = (jnp.cumsum(bf) - 1) % 2
    later_first = jnp.logical_and(bf[None, :] == 1, bi[None, :] > bi[:, None])
    nxt = jnp.min(jnp.where(later_first, bi[None, :], n_blk), axis=1)
    ne = jnp.sum(jnp.where(bi[None, :] == nxt[:, None], be[None, :], 0), axis=1)
    ne = jnp.where(nxt < n_blk, ne, -1)
    bgu = b_gu.reshape(NE, 1, DFF, 2)
    wspec = lambda shp: pl.BlockSpec((1,) + shp, lambda i, src, be, *_: (be[i], 0, 0))
    grid_spec = pltpu.PrefetchScalarGridSpec(
        num_scalar_prefetch=6,
        grid=(n_blk,),
        in_specs=[pl.BlockSpec(memory_space=pl.ANY),
                  pl.BlockSpec(memory_space=pl.ANY), wspec((1, DFF)), wspec((1, DFF)),
                  pl.BlockSpec(memory_space=pl.ANY), wspec((1, D))],
        out_specs=pl.BlockSpec((FBLK, D), lambda i, *_: (i, 0)),
        scratch_shapes=[pltpu.VMEM((2, FBLK, D), BF16), pltpu.SemaphoreType.DMA((2,)),
                        pltpu.VMEM((D, DFF), BF16), pltpu.VMEM((D, DFF), BF16), pltpu.VMEM((DFF, D), BF16),
                        pltpu.VMEM((2, D, 2 * DFF), F32), pltpu.VMEM((2, DFF, D), F32),
                        pltpu.SemaphoreType.DMA((2, 2))],
    )
    return pl.pallas_call(
        _ffn_body,
        grid_spec=grid_spec,
        out_shape=jax.ShapeDtypeStruct((n_blk * FBLK, D), BF16),
        compiler_params=_cp(("arbitrary",), VMEM_BIG),
        name="expert_ffn",
    )(src, be, bv, bf, ws.astype(jnp.int32), ne.astype(jnp.int32),
      xs, w_gu, bgu[..., 0], bgu[..., 1], w_down, b_down.reshape(NE, 1, D))


def _combine(dst_ref, yb_hbm, ybuf, sem, lrow_ref, gate_ref):
    rows = _ChunkGather(dst_ref, 2 * NCHT, yb_hbm, ybuf, sem)
    slot = rows.arrive()
    rows.fetch_next()
    scol = lax.broadcasted_iota(jnp.int32, (HALF, SROWS), 1)
    ws = []
    for q in range(2 * TILE // HALF):
        rs = pl.ds(q * HALF, HALF)
        lrow = lrow_ref[rs, :]
        gate = gate_ref[rs, :]
        w = jnp.zeros((HALF, SROWS), F32)
        for kk in range(TOPK):
            w = w + jnp.where(lrow[:, kk:kk + 1] == scol, gate[:, kk:kk + 1], 0.0)
        ws.append(w.astype(BF16))
    tile_rows = lambda q: pl.ds((q * HALF // TILE) * SROWS, SROWS)
    return [jnp.dot(w, ybuf[slot, tile_rows(q), :], preferred_element_type=F32) for q, w in enumerate(ws)], rows


def _odd_in_body(dst_ref, x_ref, yb_hbm, lrow_ref, gate_ref, mod0a_ref, mod1a_ref, mod0b_ref, mod1b_ref, n1_ref,
                 wq_ref, wk_ref, wv_ref, wgo_ref, wgd_ref, wgkf_ref, bgkf_ref, wgkb_ref, bgkb_ref,
                 x2_ref, q_ref, k_ref, v_ref, go_ref, gf_ref, gb_ref, ybuf, sem):
    mods = ((mod0a_ref[0], mod1a_ref[0]), (mod0b_ref[0], mod1b_ref[0]))
    ys, rows = _combine(dst_ref, yb_hbm, ybuf, sem, lrow_ref, gate_ref)
    groups = [pl.ds(q * HALF, HALF) for q in range(len(ys))]
    hs = []
    for q, (rs, y) in enumerate(zip(groups, ys)):
        m0, m1 = mods[q * HALF // TILE]
        x2 = x_ref[rs, :] + m0[5:6] * y
        x2_ref[rs, :] = x2
        hs.append(_modulate(x2, n1_ref[...], m1[0:1], m1[1:2]).astype(BF16))
    for rs, h in zip(groups, hs):
        q_ref[rs, :] = jnp.dot(h, wq_ref[...], preferred_element_type=F32) * (GDK ** -0.5)
    for rs, h in zip(groups, hs):
        k_ref[rs, :] = jnp.dot(h, wk_ref[...], preferred_element_type=F32)
    for rs, h in zip(groups, hs):
        v_ref[rs, :] = jnp.dot(h, wv_ref[...], preferred_element_type=F32).astype(BF16)
    for rs, h in zip(groups, hs):
        go_ref[rs, :] = jnp.dot(h, wgo_ref[...], preferred_element_type=F32).astype(BF16)
    gds = [jnp.dot(h, wgd_ref[...], preferred_element_type=F32) for h in hs]

    def pieces(a):
        hi = a.astype(BF16)
        return hi, (a - hi.astype(F32)).astype(BF16)

    for w_ref, b_ref, g_ref in ((wgkf_ref, bgkf_ref, gf_ref), (wgkb_ref, bgkb_ref, gb_ref)):
        w_hi, w_lo = pieces(w_ref[...])
        for rs, gd in zip(groups, gds):
            gd_hi, gd_lo = pieces(gd)
            z = (jnp.dot(gd_hi, w_hi, preferred_element_type=F32) + jnp.dot(gd_hi, w_lo, preferred_element_type=F32)
                 + jnp.dot(gd_lo, w_hi, preferred_element_type=F32) + b_ref[...])
            g_ref[rs, :] = (jnp.minimum(z, 0.0) - jnp.log(1.0 + jnp.exp(-jnp.abs(z)))) / GNORM
    rows.drain()


def odd_in(dst, x1, yb, lrow, gate, mod0, mod1, n1g, wq, wk, wv, wgo, wgd, wgkf, bgkf, wgkb, bgkb):
    full = lambda w: pl.BlockSpec(w.shape, lambda n, d: (0,) * w.ndim)
    row = lambda w: pl.BlockSpec((2 * TILE, w), lambda n, d: (n, 0))
    modspec = lambda k: pl.BlockSpec((1, 6, D), lambda n, d: (_mod_index(2 * n + k), 0, 0))
    grid_spec = pltpu.PrefetchScalarGridSpec(
        num_scalar_prefetch=1,
        grid=(NT_ALL // 2,),
        in_specs=[row(D), pl.BlockSpec(memory_space=pl.ANY), row(LANES), row(LANES),
                  modspec(0), modspec(0), modspec(1), modspec(1),
                  full(n1g), full(wq), full(wk), full(wv), full(wgo),
                  full(wgd), full(wgkf), full(bgkf), full(wgkb), full(bgkb)],
        out_specs=[row(D), row(GKW), row(GKW), row(GVW), row(GVW), row(GKW), row(GKW)],
        scratch_shapes=[pltpu.VMEM((2, 2 * SROWS, D), BF16), pltpu.SemaphoreType.DMA((2,))],
    )
    return pl.pallas_call(
        _odd_in_body,
        grid_spec=grid_spec,
        out_shape=[jax.ShapeDtypeStruct((T_ALL, D), F32),
                   jax.ShapeDtypeStruct((T_ALL, GKW), F32),
                   jax.ShapeDtypeStruct((T_ALL, GKW), F32),
                   jax.ShapeDtypeStruct((T_ALL, GVW), BF16),
                   jax.ShapeDtypeStruct((T_ALL, GVW), BF16),
                   jax.ShapeDtypeStruct((T_ALL, GKW), F32),
                   jax.ShapeDtypeStruct((T_ALL, GKW), F32)],
        compiler_params=_cp(("arbitrary",), VMEM_BIG),
        name="odd_in",
    )(dst, x1, yb, lrow, gate, mod0, mod1, mod0, mod1, n1g, wq, wk, wv, wgo, wgd, wgkf, bgkf, wgkb, bgkb)


def _gla_body(qf_ref, kf_ref, vf_ref, gf_ref, qb_ref, kb_ref, vb_ref, gb_ref, of_ref, ob_ref, sf_ref, sb_ref):
    @pl.when(pl.program_id(1) == 0)
    def _():
        sf_ref[...] = jnp.zeros_like(sf_ref)
        sb_ref[...] = jnp.zeros_like(sb_ref)

    r = lax.broadcasted_iota(jnp.int32, (GCHUNK, GCHUNK), 0)
    c = lax.broadcasted_iota(jnp.int32, (GCHUNK, GCHUNK), 1)
    dirs = []
    for s in range(GBS):
        at = lambda ref: ref.at[0, s]
        dirs.append((at(qf_ref), at(kf_ref), at(vf_ref), at(gf_ref), sf_ref.at[s], at(of_ref), c <= r, GCHUNK - 1))
        dirs.append((at(qb_ref), at(kb_ref), at(vb_ref), at(gb_ref), sb_ref.at[s], at(ob_ref), c >= r, 0))
    chains = []
    for q_ref, k_ref, v_ref, g_ref, s_ref, o_ref, mask, last in dirs:
        g = g_ref[...]
        g_hi = g.astype(BF16)
        g_lo = (g - g_hi.astype(F32)).astype(BF16)
        tri = mask.astype(BF16)
        cum = (jnp.dot(tri, g_hi, preferred_element_type=F32) + jnp.dot(tri, g_lo, preferred_element_type=F32))
        tot = cum[last:last + 1, :]
        q = q_ref[...]
        k = k_ref[...]
        qd = (q * jnp.exp(cum)).astype(BF16)
        kd = (k * jnp.exp(-cum)).astype(BF16)
        ke = (k * jnp.exp(tot - cum)).astype(BF16)
        dec = jnp.exp(tot)
        v = v_ref[...]
        for hd in range(GH):
            ks = slice(hd * GDK, (hd + 1) * GDK)
            vs = slice(hd * GDV, (hd + 1) * GDV)
            chains.append(dict(qd=qd[:, ks], kd=kd[:, ks], ke=ke[:, ks], dec=dec[:, ks], v=v[:, vs],
                               st=s_ref[hd], mask=mask))
    for ch in chains:
        sc = lax.dot_general(ch["qd"], ch["kd"], NT_DIMS, preferred_element_type=F32)
        ch["sc"] = jnp.where(ch["mask"], sc, 0.0).astype(BF16)
    for ch in chains:
        ch["out"] = (lax.dot_general(ch["qd"], ch["st"].astype(BF16), NT_DIMS, preferred_element_type=F32)
                     + jnp.dot(ch["sc"], ch["v"], preferred_element_type=F32))
    for ch in chains:
        ch["new"] = ch["st"] * ch["dec"] + lax.dot_general(ch["v"], ch["ke"], TN_DIMS, preferred_element_type=F32)
    for d, (_, _, _, _, s_ref, o_ref, _, _) in enumerate(dirs):
        o_ref[...] = jnp.concatenate([chains[d * GH + hd]["out"] for hd in range(GH)], axis=-1)
        for hd in range(GH):
            s_ref[hd] = chains[d * GH + hd]["new"]


def _bwd_chunk(c):
    return jnp.where(c < NCH_CTX, NCH_CTX - 1 - c, NCH_B + NCH_CTX - 1 - c)


def gla(q, k, v, gf, gb):
    fw = lambda w: pl.BlockSpec((1, GBS, GCHUNK, w), lambda b, c: (b, 0, c, 0))
    bw = lambda w: pl.BlockSpec((1, GBS, GCHUNK, w), lambda b, c: (b, 0, _bwd_chunk(c), 0))
    grouped = lambda a: a.reshape(B // GBS, GBS, ROWS_B, a.shape[-1])
    q, k, v, gf, gb = (grouped(a) for a in (q, k, v, gf, gb))
    oshape = jax.ShapeDtypeStruct((B // GBS, GBS, ROWS_B, GVW), F32)
    o_f, o_b = pl.pallas_call(
        _gla_body,
        grid=(B // GBS, NCH_B),
        in_specs=[fw(GKW), fw(GKW), fw(GVW), fw(GKW), bw(GKW), bw(GKW), bw(GVW), bw(GKW)],
        out_specs=[fw(GVW), bw(GVW)],
        out_shape=[oshape, oshape],
        scratch_shapes=[pltpu.VMEM((GBS, GH, GDV, GDK), F32), pltpu.VMEM((GBS, GH, GDV, GDK), F32)],
        compiler_params=_cp(("arbitrary", "arbitrary")),
        name="gla",
    )(q, k, v, gf, q, k, v, gb)
    return o_f.reshape(T_ALL, GVW), o_b.reshape(T_ALL, GVW)


def _out1_body(of0_ref, ob0_ref, go0_ref, x0_ref, of1_ref, ob1_ref, go1_ref, x1_ref,
               mod_ref, gn_ref, wo_ref, n2_ref, rw_ref, rb_ref,
               x3_ref, lrow_ref, gate_ref, cnt_ref, xs_ref):
    m = mod_ref[0]
    gn = gn_ref[...]
    tiles = ((of0_ref, ob0_ref, go0_ref, x0_ref), (of1_ref, ob1_ref, go1_ref, x1_ref))
    ons = []
    for of_ref, ob_ref, go_ref, _ in tiles:
        o = of_ref[...] + ob_ref[...]
        parts = [_rms(o[:, hd * GDV:(hd + 1) * GDV], gn) for hd in range(GH)]
        ons.append((jnp.concatenate(parts, axis=-1) * _silu(go_ref[...].astype(F32))).astype(BF16))
    ys = [jnp.dot(on, wo_ref[...], preferred_element_type=F32) for on in ons]
    h2s, outs = [], []
    for k, ((_, _, _, x_ref), y) in enumerate(zip(tiles, ys)):
        rs = pl.ds(k * TILE, TILE)
        x3 = x_ref[...] + m[2:3] * y
        x3_ref[rs, :] = x3
        h2s.append(_modulate(x3, n2_ref[...], m[3:4], m[4:5]))
        outs.append((lrow_ref.at[rs, :], gate_ref.at[rs, :], cnt_ref.at[k], xs_ref.at[pl.ds(k * SROWS, SROWS), :]))
    _route_tiles(h2s, rw_ref, rb_ref, outs)


def out1(of, ob, go, x2, mod, gng, wo, n2g, rw, rb):
    full = lambda w: pl.BlockSpec(w.shape, lambda n: (0,) * w.ndim)
    lat = lambda w, k: pl.BlockSpec((TILE, w), lambda n: (_lat_tile(2 * n + k), 0))
    return pl.pallas_call(
        _out1_body,
        grid=(NT_LAT // 2,),
        in_specs=[lat(GVW, 0), lat(GVW, 0), lat(GVW, 0), lat(D, 0), lat(GVW, 1), lat(GVW, 1), lat(GVW, 1), lat(D, 1),
                  pl.BlockSpec((1, 6, D), lambda n: (2 * n // LPB, 0, 0)),
                  full(gng), full(wo), full(n2g), full(rw), full(rb)],
        out_specs=[_pair_spec((TILE, D))] + ROUTE_PAIR_SPECS,
        out_shape=[jax.ShapeDtypeStruct((T_LAT, D), F32)] + _route_out_shapes(NT_LAT),
        compiler_params=_cp(("arbitrary",), VMEM_BIG),
        name="out1",
    )(of, ob, go, x2, of, ob, go, x2, mod, gng, wo, n2g, rw, rb)


def _final_body(dst_ref, x_ref, yb_hbm, lrow_ref, gate_ref, mod_ref, g_ref, o_ref, ybuf, sem):
    m = mod_ref[0]
    ys, rows = _combine(dst_ref, yb_hbm, ybuf, sem, lrow_ref, gate_ref)
    for q, y in enumerate(ys):
        rs = pl.ds(q * HALF, HALF)
        o_ref[rs, :] = _rms(x_ref[rs, :] + m[5:6] * y, g_ref[...])
    rows.drain()


def final(dst, x3, yb, lrow, gate, mod, g):
    grid_spec = pltpu.PrefetchScalarGridSpec(
        num_scalar_prefetch=1,
        grid=(NT_LAT // 2,),
        in_specs=[pl.BlockSpec((2 * TILE, D), lambda n, d: (n, 0)),
                  pl.BlockSpec(memory_space=pl.ANY),
                  pl.BlockSpec((2 * TILE, LANES), lambda n, d: (n, 0)),
                  pl.BlockSpec((2 * TILE, LANES), lambda n, d: (n, 0)),
                  pl.BlockSpec((1, 6, D), lambda n, d: (2 * n // LPB, 0, 0)),
                  pl.BlockSpec((1, D), lambda n, d: (0, 0))],
        out_specs=pl.BlockSpec((2 * TILE, D), lambda n, d: (n, 0)),
        scratch_shapes=[pltpu.VMEM((2, 2 * SROWS, D), BF16), pltpu.SemaphoreType.DMA((2,))],
    )
    return pl.pallas_call(
        _final_body,
        grid_spec=grid_spec,
        out_shape=jax.ShapeDtypeStruct((T_LAT, D), F32),
        compiler_params=_cp(("arbitrary",)),
        name="final",
    )(dst, x3, yb, lrow, gate, mod, g)


def _rot_cols(w):
    a, b, c, d = jnp.split(w, 4, axis=-1)
    return jnp.concatenate([-b, a, -d, c], axis=-1)


def _rope_tables():
    n = np.arange(SEQ)
    row = (n // GRID_W).astype(np.float32)
    col = (n % GRID_W).astype(np.float32)
    axis_dim = ROPE // 2
    inv = (ROPE_BASE ** (-np.arange(0, axis_dim, 2, dtype=np.float32) / axis_dim)).astype(np.float32)
    ar = row[:, None] * inv
    ac = col[:, None] * inv
    ang = np.concatenate([ar, ar, ac, ac], axis=-1).astype(np.float32)
    cos = np.concatenate([np.ones((CTX, ROPE), np.float32), np.cos(ang)], axis=0)
    sin = np.concatenate([np.zeros((CTX, ROPE), np.float32), np.sin(ang)], axis=0)
    return jnp.asarray(np.tile(cos, (1, HEADS))), jnp.asarray(np.tile(sin, (1, HEADS)))


def _dft(n, scale):
    j = np.arange(n, dtype=np.int64)
    ang = 2.0 * np.pi * ((j[:, None] * j[None, :]) % n).astype(np.float64) / n
    return (np.cos(ang) * scale), (np.sin(ang) * scale)


def _dft_tables():
    cn, sn = _dft(SEQ, SEQ ** -0.5)
    c2, s2 = _dft(CTX, CTX ** -0.5)
    cg, sg = _dft(FGD, FGD ** -0.5)
    eye = np.eye(FG)
    to = lambda a: jnp.asarray(a.astype(np.float32)).astype(BF16)
    return to(cn), to(sn), to(c2), to(s2), to(np.kron(eye, cg)), to(np.kron(eye, sg))


def _router_pad(rw, rb):
    rwp = jnp.zeros((LANES, D), F32).at[:NE].set(rw.T)
    rbp = jnp.full((LANES, 1), -1e30, F32).at[:NE, 0].set(rb)
    return rwp, rbp


def kernel(x, c, ctx, c_ctx, final_norm_g, l0_mod_w, l0_mod_b, l0_norm1_g, l0_w_in, l0_q_norm_g, l0_w_q_up, l0_kv_norm_g, l0_w_kv_up, l0_w_out, l0_norm2_g, l0_router_w, l0_router_b, l0_w_gu, l0_b_gu, l0_w_down, l0_b_down, l1_mod_w, l1_mod_b, l1_norm1_g, l1_w_in, l1_w_gk_fwd, l1_b_gk_fwd, l1_w_gk_bwd, l1_b_gk_bwd, l1_gnorm_g, l1_w_out, l1_norm2_g, l1_router_w, l1_router_b, l1_w_gu, l1_b_gu, l1_w_down, l1_b_down):
    row = lambda g: g.reshape(1, -1)
    c2 = ctx.reshape(B * CTX, D)
    x2 = x.reshape(T_LAT, D)
    cc = jnp.zeros((16, D), F32).at[:B].set(c).at[B].set(c_ctx)
    mod0 = adaln(cc, l0_mod_w, l0_mod_b)
    mod1 = adaln(cc, l1_mod_w, l1_mod_b)

    w_in = l0_w_in
    wqa = w_in[:, :Q_LORA].astype(BF16)
    wkva = w_in[:, Q_LORA:Q_LORA + KV_LORA].astype(BF16)
    wkr = w_in[:, Q_LORA + KV_LORA:Q_LORA + KV_LORA + ROPE]
    wkr2 = jnp.concatenate([wkr, _rot_cols(wkr)], axis=-1).astype(BF16)
    wf = w_in[:, Q_LORA + KV_LORA + ROPE:].astype(BF16)
    wq3 = l0_w_q_up.reshape(Q_LORA, HEADS, QK)
    wq_nope = wq3[:, :, :NOPE].reshape(Q_LORA, HEADS * NOPE)
    wq_rope = wq3[:, :, NOPE:]
    wqup = jnp.concatenate([wq_nope, wq_rope.reshape(Q_LORA, HEADS * ROPE),
                            _rot_cols(wq_rope).reshape(Q_LORA, HEADS * ROPE)], axis=-1).astype(BF16)
    wkv3 = l0_w_kv_up.reshape(KV_LORA, HEADS, NOPE + VDIM)
    wkvup = jnp.concatenate([wkv3[:, :, :NOPE].reshape(KV_LORA, HEADS * NOPE),
                             wkv3[:, :, NOPE:].reshape(KV_LORA, HEADS * VDIM)], axis=-1).astype(BF16)
    cos6, sin6 = _rope_tables()
    q, k, v, uf = even_in(c2, x2, mod0, row(l0_norm1_g), wqa, wkva, wkr2, wf, row(l0_q_norm_g), wqup,
                          row(l0_kv_norm_g), wkvup, cos6, sin6)
    att = attention(q, k, v).reshape(T_ALL, HEADS * VDIM)
    fmix = fourier(uf.reshape(B, ROWS_B, FW), *_dft_tables()).reshape(T_ALL, FW)
    rw0, rb0 = _router_pad(l0_router_w, l0_router_b)
    x1, lrow0, gate0, cnt0, xs0 = out0(c2, x2, att, fmix, mod0, l0_w_out[:HEADS * VDIM].astype(BF16),
                                       l0_w_out[HEADS * VDIM:].astype(BF16), row(l0_norm2_g), rw0, rb0)
    src0, be0, bv0, dst0, nb0 = _chunk_lists(cnt0, NT_ALL)
    yb0 = expert_ffn(xs0, src0, be0, bv0, nb0, l0_w_gu, l0_b_gu, l0_w_down, l0_b_down)

    w1 = l1_w_in
    o = 0
    wq1 = w1[:, o:o + GKW].astype(BF16); o += GKW
    wk1 = w1[:, o:o + GKW].astype(BF16); o += GKW
    wv1 = w1[:, o:o + GVW].astype(BF16); o += GVW
    wgo = w1[:, o:o + GVW].astype(BF16); o += GVW
    wgd = jnp.zeros((D, LANES), F32).at[:, :2 * GRANK].set(w1[:, o:]).astype(BF16)
    wgkf = jnp.zeros((LANES, GKW), F32).at[:GRANK].set(l1_w_gk_fwd)
    wgkb = jnp.zeros((LANES, GKW), F32).at[GRANK:2 * GRANK].set(l1_w_gk_bwd)
    x2, q1, k1, v1, go, gf, gb = odd_in(dst0, x1, yb0, lrow0, gate0, mod0, mod1, row(l1_norm1_g), wq1, wk1, wv1,
                                        wgo, wgd, wgkf, row(l1_b_gk_fwd), wgkb, row(l1_b_gk_bwd))
    o_f, o_b = gla(q1, k1, v1, gf, gb)
    rw1, rb1 = _router_pad(l1_router_w, l1_router_b)
    x3, lrow1, gate1, cnt1, xs1 = out1(o_f, o_b, go, x2, mod1, row(l1_gnorm_g), l1_w_out.astype(BF16),
                                       row(l1_norm2_g), rw1, rb1)
    src1, be1, bv1, dst1, nb1 = _chunk_lists(cnt1, NT_LAT)
    yb1 = expert_ffn(xs1, src1, be1, bv1, nb1, l1_w_gu, l1_b_gu, l1_w_down, l1_b_down)
    out = final(dst1, x3, yb1, lrow1, gate1, mod1, row(final_norm_g))
    return out.reshape(B, SEQ, D)
```
